```python
import jax, jax.numpy as jnp
from jax import lax
import numpy as np

D_MODEL = 1024
BATCH = 16
SEQ = 2048
DEPTH = 2
DEC_BATCH = 32
DEC_SEQ = 1
PAST_LEN = 16384
PAGE_SIZE = 128

N_EVEN = (DEPTH + 1) // 2
N_ODD = DEPTH // 2
HG_H = 4
HG_DK = 128
HG_DV = D_MODEL // 2 // HG_H
NSA_H = 8
NSA_DH = D_MODEL // 2 // NSA_H
NSA_KV = 2
NSA_R = NSA_H // NSA_KV
NSA_BLK = 64
NSA_TOPK = 16
NSA_WINDOW = 512
NSA_CMP_HID = NSA_DH
NSA_QB = 128
NSA_Q = NSA_H * NSA_DH
GLA_H = 4
GLA_DK = D_MODEL // 2 // GLA_H
GLA_DV = D_MODEL // GLA_H
GLA_RANK = 16
GLA_TAU = 16.0
CHUNK = 32
D_FF = ((-(-8 * D_MODEL // 3) + 255) // 256) * 256
EPS = 1e-6
NEG = -1e30
TINY = 1e-30

HG_W = 2 * HG_H * HG_DK + 2 * HG_H * HG_DV
NSA_W = NSA_Q + 6 * NSA_KV * NSA_DH + 3 * NSA_H
EVEN_W = HG_W + NSA_W
EVEN_OUT = HG_H * HG_DV + NSA_H * NSA_DH
ODD_W = 2 * GLA_H * GLA_DK + 2 * GLA_H * GLA_DV + GLA_RANK
ODD_OUT = GLA_H * GLA_DV

kernel_name = 'hybrid_hgrn2_nsa_gla_decoder_step'


def rmsnorm(x, g):
    xf = x.astype(jnp.float32)
    y = xf * lax.rsqrt(jnp.mean(xf * xf, axis=-1, keepdims=True) + EPS)
    return (y * g.astype(jnp.float32)).astype(x.dtype)


def masked_softmax(s, mask, axes):
    s = jnp.where(mask, s, NEG)
    m = jnp.max(s, axis=axes, keepdims=True)
    p = jnp.where(mask, jnp.exp(s - m), 0.0)
    return p / jnp.maximum(jnp.sum(p, axis=axes, keepdims=True), TINY)


def alibi_slopes(n):
    return jnp.asarray(2.0 ** (-8.0 * np.arange(1, n + 1) / n), dtype=jnp.float32)


def chunk_gla(q, k, v, logf, s0):
    B, T, H, K = q.shape
    V = v.shape[-1]
    C = min(CHUNK, T)
    n = -(-T // C)
    pad = n * C - T

    def prep(a):
        a = jnp.pad(a.astype(jnp.float32), ((0, 0), (0, pad), (0, 0), (0, 0)))
        return a.reshape(B, n, C, H, a.shape[-1]).transpose(1, 0, 2, 3, 4)

    qc, kc, vc, gc = prep(q), prep(k), prep(v), prep(logf)
    causal = jnp.tril(jnp.ones((C, C), bool))[None, :, :, None, None]

    def step(S, inp):
        qi, ki, vi, gi = inp
        b = jnp.cumsum(gi, axis=1)
        dec = jnp.exp(jnp.where(causal, b[:, :, None] - b[:, None, :], -jnp.inf))
        a = jnp.einsum('bthk,bshk,btshk->bhts', qi, ki, dec)
        o = jnp.einsum('bthk,bhkv->bthv', qi * jnp.exp(b), S) + jnp.einsum('bhts,bshv->bthv', a, vi)
        bl = b[:, -1]
        S = jnp.exp(bl)[..., None] * S + jnp.einsum('bshk,bshv->bhkv', ki * jnp.exp(bl[:, None] - b), vi)
        return S, o

    S, o = lax.scan(step, s0.astype(jnp.float32), (qc, kc, vc, gc))
    o = o.transpose(1, 0, 2, 3, 4).reshape(B, n * C, H, V)[:, :T]
    return o, S


def hgrn2_mixer(p, lb, gain, s0):
    B, T, _ = p.shape
    hk, hv = HG_H * HG_DK, HG_H * HG_DV
    q = jax.nn.silu(p[..., :hk].astype(jnp.float32)).reshape(B, T, HG_H, HG_DK)
    f = lb + (1.0 - lb) * jax.nn.sigmoid(p[..., hk:2 * hk].astype(jnp.float32))
    f = f.reshape(B, T, HG_H, HG_DK)
    i = p[..., 2 * hk:2 * hk + hv].reshape(B, T, HG_H, HG_DV)
    og = p[..., 2 * hk + hv:].reshape(B, T, HG_H, HG_DV)
    o, S = chunk_gla(q, 1.0 - f, i, jnp.log(f), s0)
    o = rmsnorm(o, gain) * jax.nn.sigmoid(og.astype(jnp.float32))
    return o.reshape(B, T, hv), S


def gla_mixer(p, wa2, ba, gain, s0):
    B, T, _ = p.shape
    hk, hv = GLA_H * GLA_DK, GLA_H * GLA_DV
    q = p[..., :hk].reshape(B, T, GLA_H, GLA_DK) * (GLA_DK ** -0.5)
    k = p[..., hk:2 * hk].reshape(B, T, GLA_H, GLA_DK)
    v = p[..., 2 * hk:2 * hk + hv].reshape(B, T, GLA_H, GLA_DV)
    a1 = p[..., 2 * hk + hv:2 * hk + hv + GLA_RANK]
    r = p[..., 2 * hk + hv + GLA_RANK:].reshape(B, T, GLA_H, GLA_DV)
    logf = (jax.nn.log_sigmoid((a1 @ wa2 + ba).astype(jnp.float32)) / GLA_TAU).reshape(B, T, GLA_H, GLA_DK)
    o, S = chunk_gla(q, k, v, logf, s0)
    o = rmsnorm(o, gain) * jax.nn.silu(r.astype(jnp.float32))
    return o.reshape(B, T, hv), S


def compress_blocks(x, pe, w1, w2):
    B, n, l, G, dh = x.shape
    z = (x + pe[:, None, :]).transpose(0, 1, 3, 2, 4).reshape(B, n, G, l * dh)
    return jax.nn.silu(z @ w1) @ w2


def nsa_mixer(p, pos0, past_kv, win_buf, cmp_pos, cmp_w1, cmp_w2):
    B, T, _ = p.shape
    G, R, dh = NSA_KV, NSA_R, NSA_DH
    q = p[..., :NSA_Q].reshape(B, T, G, R, dh)
    kv_new = p[..., NSA_Q:NSA_Q + 6 * G * dh].reshape(B, T, 6, G, dh)
    gates = jax.nn.sigmoid(p[..., NSA_Q + 6 * G * dh:].astype(jnp.float32)).reshape(B, T, G, R, 3)
    rows = kv_new[:, :, :4]
    win_new = kv_new[:, :, 4:]
    L = pos0 + T
    nblk = -(-L // NSA_BLK)
    parts = [] if past_kv is None else [past_kv.astype(rows.dtype)]
    parts = parts + [rows, jnp.zeros((B, nblk * NSA_BLK - L, 4, G, dh), rows.dtype)]
    blocks = jnp.concatenate(parts, axis=1).reshape(B, nblk, NSA_BLK, 4, G, dh)
    slopes = alibi_slopes(NSA_H).reshape(G, R)
    scale = dh ** -0.5
    qpos = pos0 + jnp.arange(T)

    kc = compress_blocks(blocks[:, :, :, 0], cmp_pos[0], cmp_w1[0], cmp_w2[0])
    vc = compress_blocks(blocks[:, :, :, 1], cmp_pos[1], cmp_w1[1], cmp_w2[1])
    blk = jnp.arange(nblk)
    dist_c = (qpos[:, None] - (blk * NSA_BLK + NSA_BLK - 1)[None, :]).astype(jnp.float32)
    s_c = jnp.einsum('btgrd,bngd->bgrtn', q, kc).astype(jnp.float32) * scale - slopes[:, :, None, None] * dist_c
    p_c = masked_softmax(s_c, dist_c >= 0, (-1,))
    o_cmp = jnp.einsum('bgrtn,bngd->btgrd', p_c, vc.astype(jnp.float32))

    imp = jnp.sum(p_c, axis=2)
    forced = (blk[None, :] == (qpos // NSA_BLK)[:, None]) | (blk[None, :] == 0)
    started = blk[None, :] * NSA_BLK <= qpos[:, None]
    score = jnp.where(forced, jnp.inf, jnp.where(started, imp, -jnp.inf))
    ksel = min(NSA_TOPK, nblk)
    _, idx = lax.top_k(score, ksel)

    ks = blocks[:, :, :, 2].transpose(0, 3, 1, 2, 4)
    vs = blocks[:, :, :, 3].transpose(0, 3, 1, 2, 4)
    qb_len = NSA_QB if T % NSA_QB == 0 else T
    nqb = T // qb_len
    qx = q.reshape(B * nqb, qb_len, G, R, dh)
    ix = idx.reshape(B, G, nqb, qb_len, ksel).transpose(0, 2, 1, 3, 4).reshape(B * nqb, G, qb_len, ksel)
    flat = jnp.arange(B * nqb)
    garr = jnp.arange(G)[:, None, None]

    def sel_body(args):
        b, i, qb, ib = args
        kg = ks[b][garr, ib]
        vg = vs[b][garr, ib]
        pq = pos0 + i * qb_len + jnp.arange(qb_len)
        kpos = ib[..., None] * NSA_BLK + jnp.arange(NSA_BLK)
        dist = (pq[None, :, None, None] - kpos).astype(jnp.float32)
        ss = jnp.einsum('qgrd,gqkld->grqkl', qb, kg).astype(jnp.float32) * scale - slopes[:, :, None, None, None] * dist[:, None]
        pp = masked_softmax(ss, (dist >= 0)[:, None], (-2, -1))
        return jnp.einsum('grqkl,gqkld->qgrd', pp, vg.astype(jnp.float32))

    o_slc = lax.map(sel_body, (flat // nqb, flat % nqb, qx, ix)).reshape(B, T, G, R, dh)

    if win_buf is None:
        wk, lbuf = win_new, 0
    else:
        wk, lbuf = jnp.concatenate([win_buf.astype(win_new.dtype), win_new], axis=1), win_buf.shape[1]
    new_win = wk[:, wk.shape[1] - min(NSA_WINDOW, L):]
    wkp = jnp.pad(wk, ((0, 0), (NSA_WINDOW - lbuf, 0), (0, 0), (0, 0), (0, 0)))
    span = NSA_WINDOW + qb_len
    qw = q.reshape(B, nqb, qb_len, G, R, dh).transpose(1, 0, 2, 3, 4, 5)

    def win_body(args):
        i, qb = args
        kv = lax.dynamic_slice_in_dim(wkp, i * qb_len, span, axis=1)
        pq = pos0 + i * qb_len + jnp.arange(qb_len)
        pk = pos0 - NSA_WINDOW + i * qb_len + jnp.arange(span)
        dist = pq[:, None] - pk[None, :]
        mask = (dist >= 0) & (dist < NSA_WINDOW) & (pk[None, :] >= 0)
        ss = jnp.einsum('bqgrd,bkgd->bgrqk', qb, kv[:, :, 0]).astype(jnp.float32) * scale - slopes[:, :, None, None] * dist.astype(jnp.float32)
        pp = masked_softmax(ss, mask, (-1,))
        return jnp.einsum('bgrqk,bkgd->bqgrd', pp, kv[:, :, 1].astype(jnp.float32))

    o_win = lax.map(win_body, (jnp.arange(nqb), qw)).transpose(1, 0, 2, 3, 4, 5).reshape(B, T, G, R, dh)

    o = gates[..., 0:1] * o_cmp + gates[..., 1:2] * o_slc + gates[..., 2:3] * o_win
    return o.reshape(B, T, NSA_H * dh), rows, new_win


def forward(x, pos0, pool, page_table, win_bufs, hg_states, gla_states, W):
    B, T, _ = x.shape
    lb_all = jnp.cumsum(jax.nn.softmax(W['hg_lb_logits'].astype(jnp.float32), axis=0), axis=0)
    rows_out, win_out, hg_out, gla_out = [], [], [], []
    for l in range(DEPTH):
        i = l // 2
        xn = rmsnorm(x, W['norm_mix'][l])
        if l % 2 == 0:
            if pool is None:
                past, wbuf = None, None
                s0 = jnp.zeros((B, HG_H, HG_DK, HG_DV), jnp.float32)
            else:
                past = pool[i, page_table].reshape(B, -1, 4, NSA_KV, NSA_DH)
                wbuf, s0 = win_bufs[i], hg_states[i]
            p = xn @ W['w_in_even'][i]
            o_h, S = hgrn2_mixer(p[..., :HG_W], lb_all[i], W['hg_norm'][i], s0)
            o_n, rows, new_win = nsa_mixer(p[..., HG_W:], pos0, past, wbuf,
                                           W['nsa_cmp_pos'][i], W['nsa_cmp_w1'][i], W['nsa_cmp_w2'][i])
            mix = jnp.concatenate([o_h.astype(x.dtype), o_n.astype(x.dtype)], axis=-1) @ W['w_out_even'][i]
            rows_out.append(rows)
            win_out.append(new_win)
            hg_out.append(S)
        else:
            s0 = jnp.zeros((B, GLA_H, GLA_DK, GLA_DV), jnp.float32) if gla_states is None else gla_states[i]
            p = xn @ W['w_in_odd'][i]
            o_g, S = gla_mixer(p, W['gla_wa2'][i], W['gla_ba'][i], W['gla_norm'][i], s0)
            mix = o_g.astype(x.dtype) @ W['w_out_odd'][i]
            gla_out.append(S)
        x = x + mix.astype(x.dtype)
        h = rmsnorm(x, W['norm_ffn'][l])
        gu = h @ W['ffn_w13'][l]
        x = x + ((jax.nn.silu(gu[..., :D_FF]) * gu[..., D_FF:]) @ W['ffn_w2'][l]).astype(x.dtype)
    y = rmsnorm(x, W['norm_final'])
    return y, jnp.stack(rows_out), jnp.stack(win_out), jnp.stack(hg_out), jnp.stack(gla_out)


def setup_inputs(seed: int = 0) -> dict:
    key = jax.random.key(seed)
    ks = jax.random.split(key, 24)
    n_pages = PAST_LEN // PAGE_SIZE
    n_used = DEC_BATCH * n_pages
    n_pool = n_used + max(1, n_used // 4)
    win_len = min(NSA_WINDOW, PAST_LEN)

    def nrm(k, shape, s):
        return s * jax.random.normal(k, shape, jnp.float32)

    page_table = jax.random.permutation(ks[6], n_pool)[:n_used].reshape(DEC_BATCH, n_pages).astype(jnp.int32)
    return {
        'x_prompt': nrm(ks[0], (BATCH, SEQ, D_MODEL), 1.0),
        'x_sample': nrm(ks[1], (DEC_BATCH, DEC_SEQ, D_MODEL), 1.0),
        'cache_nsa_kv': nrm(ks[2], (N_EVEN, n_pool, PAGE_SIZE, 4, NSA_KV, NSA_DH), 1.0),
        'cache_win_kv': nrm(ks[3], (N_EVEN, DEC_BATCH, win_len, 2, NSA_KV, NSA_DH), 1.0),
        'state_hgrn': nrm(ks[4], (N_EVEN, DEC_BATCH, HG_H, HG_DK, HG_DV), 0.5),
        'state_gla': nrm(ks[5], (N_ODD, DEC_BATCH, GLA_H, GLA_DK, GLA_DV), 0.5),
        'page_table': page_table,
        'w_in_even': nrm(ks[7], (N_EVEN, D_MODEL, EVEN_W), D_MODEL ** -0.5),
        'hg_lb_logits': nrm(ks[8], (N_EVEN + 1, HG_H * HG_DK), 0.5),
        'hg_norm': 1.0 + nrm(ks[9], (N_EVEN, HG_DV), 0.02),
        'nsa_cmp_pos': nrm(ks[10], (N_EVEN, 2, NSA_BLK, NSA_DH), 0.1),
        'nsa_cmp_w1': nrm(ks[11], (N_EVEN, 2, NSA_BLK * NSA_DH, NSA_CMP_HID), (NSA_BLK * NSA_DH) ** -0.5),
        'nsa_cmp_w2': nrm(ks[12], (N_EVEN, 2, NSA_CMP_HID, NSA_DH), NSA_CMP_HID ** -0.5),
        'w_out_even': nrm(ks[13], (N_EVEN, EVEN_OUT, D_MODEL), EVEN_OUT ** -0.5),
        'w_in_odd': nrm(ks[14], (N_ODD, D_MODEL, ODD_W), D_MODEL ** -0.5),
        'gla_wa2': nrm(ks[15], (N_ODD, GLA_RANK, GLA_H * GLA_DK), GLA_RANK ** -0.5),
        'gla_ba': nrm(ks[16], (N_ODD, GLA_H * GLA_DK), 0.1),
        'gla_norm': 1.0 + nrm(ks[17], (N_ODD, GLA_DV), 0.02),
        'w_out_odd': nrm(ks[18], (N_ODD, ODD_OUT, D_MODEL), ODD_OUT ** -0.5),
        'norm_mix': 1.0 + nrm(ks[19], (DEPTH, D_MODEL), 0.02),
        'norm_ffn': 1.0 + nrm(ks[20], (DEPTH, D_MODEL), 0.02),
        'norm_final': 1.0 + nrm(ks[21], (D_MODEL,), 0.02),
        'ffn_w13': nrm(ks[22], (DEPTH, D_MODEL, 2 * D_FF), D_MODEL ** -0.5),
        'ffn_w2': nrm(ks[23], (DEPTH, D_FF, D_MODEL), D_FF ** -0.5),
    }


def reference(x_prompt, x_sample, cache_nsa_kv, cache_win_kv, state_hgrn, state_gla, page_table,
              w_in_even, hg_lb_logits, hg_norm, nsa_cmp_pos, nsa_cmp_w1, nsa_cmp_w2, w_out_even,
              w_in_odd, gla_wa2, gla_ba, gla_norm, w_out_odd, norm_mix, norm_ffn, norm_final,
              ffn_w13, ffn_w2):
    W = dict(w_in_even=w_in_even, hg_lb_logits=hg_lb_logits, hg_norm=hg_norm, nsa_cmp_pos=nsa_cmp_pos,
             nsa_cmp_w1=nsa_cmp_w1, nsa_cmp_w2=nsa_cmp_w2, w_out_even=w_out_even, w_in_odd=w_in_odd,
             gla_wa2=gla_wa2, gla_ba=gla_ba, gla_norm=gla_norm, w_out_odd=w_out_odd, norm_mix=norm_mix,
             norm_ffn=norm_ffn, norm_final=norm_final, ffn_w13=ffn_w13, ffn_w2=ffn_w2)
    past_len = page_table.shape[1] * cache_nsa_kv.shape[2]
    y_prompt, kv_p, win_p, hg_p, gla_p = forward(x_prompt, 0, None, None, None, None, None, W)
    y_sample, kv_s, win_s, hg_s, gla_s = forward(x_sample, past_len, cache_nsa_kv, page_table,
                                                 cache_win_kv, state_hgrn, state_gla, W)
    return (y_prompt, y_sample, kv_p, kv_s, win_p, win_s, hg_p, hg_s, gla_p, gla_s)
```

```python
import functools

import jax
import jax.numpy as jnp
from jax import lax
from jax.experimental import pallas as pl
from jax.experimental.pallas import tpu as pltpu

F32 = jnp.float32
BF16 = jnp.bfloat16
I32 = jnp.int32

D_MODEL = 1024
HG_H, HG_DK, HG_DV = 4, 128, 128
NSA_H, NSA_DH, NSA_G, NSA_R = 8, 64, 2, 4
NSA_BLK = 64
NSA_TOPK = 16
NSA_WINDOW = 512
NSA_QB = 128
GLA_H, GLA_DK, GLA_DV = 4, 128, 256
GLA_RANK = 16
GLA_TAU = 16.0
D_FF = 2816
EPS = 1e-6
NEG = -1e30
TINY = 1e-30
PAGE = 128

PROJ_W = 3584
EV_Q, EV_F, EV_I, EV_OG = 0, 512, 1024, 1536
EV_NQ, EV_CMP, EV_SLC, EV_WIN, EV_GATE = 2048, 2560, 2816, 3072, 3328
OD_Q, OD_K, OD_V, OD_R, OD_A = 0, 512, 1024, 2048, 3072

LANES = 128
V7X_VMEM_LIMIT = 56 * 1024 * 1024
FF_TILE = 256
CHUNK = 128
SUB = 16

NT_DIMS = (((1,), (1,)), ((), ()))
TN_DIMS = (((0,), (0,)), ((), ()))


def _cparams(sem):
    return pltpu.CompilerParams(dimension_semantics=sem, vmem_limit_bytes=V7X_VMEM_LIMIT)


def _rms(x, g):
    return x * lax.rsqrt(jnp.mean(x * x, axis=-1, keepdims=True) + EPS) * g


def _sigmoid(x):
    return 1.0 / (1.0 + jnp.exp(-x))


def _silu(x):
    return x * _sigmoid(x)


def _iota(shape, dim):
    return lax.broadcasted_iota(I32, shape, dim)


def _norm_proj_kernel(x_ref, g_ref, w_ref, o_ref, xn_ref):
    @pl.when(pl.program_id(1) == 0)
    def _():
        xn_ref[...] = _rms(x_ref[...], g_ref[...]).astype(BF16)

    o_ref[...] = jnp.dot(xn_ref[...], w_ref[...], preferred_element_type=F32)


def _norm_proj(x2d, gain, w, tm, tn):
    m, k = x2d.shape
    n = w.shape[1]
    return pl.pallas_call(
        _norm_proj_kernel,
        grid=(m // tm, n // tn),
        in_specs=[pl.BlockSpec((tm, k), lambda i, j: (i, 0)),
                  pl.BlockSpec((1, k), lambda i, j: (0, 0)),
                  pl.BlockSpec((k, tn), lambda i, j: (0, j))],
        out_specs=pl.BlockSpec((tm, tn), lambda i, j: (i, j)),
        out_shape=jax.ShapeDtypeStruct((m, n), F32),
        scratch_shapes=[pltpu.VMEM((tm, k), BF16)],
        compiler_params=_cparams(("parallel", "arbitrary")),
        name="norm_proj",
    )(x2d, gain.reshape(1, k), w)


def _out_ffn_kernel(*refs, n_mix, final_norm):
    mix_refs = refs[:n_mix]
    res_ref = refs[n_mix]
    wo_refs = refs[n_mix + 1:2 * n_mix + 1]
    g_ref, w13_ref, w2_ref = refs[2 * n_mix + 1:2 * n_mix + 4]
    pos = 2 * n_mix + 4
    gf_ref = refs[pos] if final_norm else None
    pos += 1 if final_norm else 0
    o_ref, x1_ref, h_ref, acc_ref = refs[pos:pos + 4]
    j = pl.program_id(1)

    @pl.when(j == 0)
    def _():
        x1 = res_ref[...]
        for m_ref, w_ref in zip(mix_refs, wo_refs):
            x1 = x1 + jnp.dot(m_ref[...].astype(BF16), w_ref[...], preferred_element_type=F32)
        x1_ref[...] = x1
        h_ref[...] = _rms(x1, g_ref[...]).astype(BF16)
        acc_ref[...] = jnp.zeros_like(acc_ref)

    gu = jnp.dot(h_ref[...], w13_ref[...], preferred_element_type=F32)
    act = _silu(gu[:, :FF_TILE]) * gu[:, FF_TILE:]
    acc_ref[...] += jnp.dot(act.astype(BF16), w2_ref[...], preferred_element_type=F32)

    @pl.when(j == pl.num_programs(1) - 1)
    def _():
        y = x1_ref[...] + acc_ref[...]
        if final_norm:
            y = _rms(y, gf_ref[...])
        o_ref[...] = y


def _out_ffn(mixes, res, wos, g_ffn, w13t, w2, g_final, tm):
    m, d = res.shape
    n_mix = len(mixes)
    nj = D_FF // FF_TILE
    final_norm = g_final is not None
    in_specs = [pl.BlockSpec((tm, mx.shape[1]), lambda i, j: (i, 0)) for mx in mixes]
    in_specs.append(pl.BlockSpec((tm, d), lambda i, j: (i, 0)))
    in_specs += [pl.BlockSpec(w.shape, lambda i, j: (0, 0)) for w in wos]
    in_specs += [pl.BlockSpec((1, d), lambda i, j: (0, 0)),
                 pl.BlockSpec((d, 2 * FF_TILE), lambda i, j: (0, j)),
                 pl.BlockSpec((FF_TILE, d), lambda i, j: (j, 0))]
    args = list(mixes) + [res] + list(wos) + [g_ffn.reshape(1, d), w13t, w2]
    if final_norm:
        in_specs.append(pl.BlockSpec((1, d), lambda i, j: (0, 0)))
        args.append(g_final.reshape(1, d))
    return pl.pallas_call(
        functools.partial(_out_ffn_kernel, n_mix=n_mix, final_norm=final_norm),
        grid=(m // tm, nj),
        in_specs=in_specs,
        out_specs=pl.BlockSpec((tm, d), lambda i, j: (i, 0)),
        out_shape=jax.ShapeDtypeStruct((m, d), F32),
        scratch_shapes=[pltpu.VMEM((tm, d), F32), pltpu.VMEM((tm, d), BF16), pltpu.VMEM((tm, d), F32)],
        compiler_params=_cparams(("parallel", "arbitrary")),
        name="out_ffn",
    )(*args)


def _gla_chunk(q, k, v, g, st):
    c = q.shape[0]
    tri = (_iota((c, c), 0) >= _iota((c, c), 1)).astype(F32)
    b = jnp.dot(tri, g, precision=lax.Precision.HIGHEST, preferred_element_type=F32)
    o = lax.dot_general((q * jnp.exp(b)).astype(BF16), st.astype(BF16), NT_DIMS, preferred_element_type=F32)
    lane = _iota((SUB, c), 1)
    row = _iota((SUB, c), 0)
    rows = []
    for blk in range(c // SUB):
        lo = blk * SUB
        b_i, q_i, k_i = b[lo:lo + SUB], q[lo:lo + SUB], k[lo:lo + SUB]
        a_blk = jnp.zeros((SUB, c), F32)
        for s in range(SUB):
            e = jnp.exp(jnp.minimum(b_i - b_i[s:s + 1], 0.0))
            a = jnp.sum(q_i * e * k_i[s:s + 1], axis=1, keepdims=True)
            a_blk = jnp.where(lane == lo + s, a, a_blk)
        a_blk = jnp.where(lane <= lo + row, a_blk, 0.0)
        if blk > 0:
            r = b[lo - 1:lo]
            qt = q_i * jnp.exp(b_i - r)
            kt = k * jnp.exp(jnp.minimum(r - b, 0.0))
            a_off = lax.dot_general(qt.astype(BF16), kt.astype(BF16), NT_DIMS, preferred_element_type=F32)
            a_blk = jnp.where(lane < lo, a_off, a_blk)
        rows.append(a_blk)
    a_full = jnp.concatenate(rows, axis=0)
    o = o + jnp.dot(a_full.astype(BF16), v.astype(BF16), preferred_element_type=F32)
    b_last = b[c - 1:c]
    kd = k * jnp.exp(b_last - b)
    st_new = st * jnp.exp(b_last) + lax.dot_general(v.astype(BF16), kd.astype(BF16), TN_DIMS,
                                                    preferred_element_type=F32)
    return o, st_new


def _lower_bound(lbl_ref, col):
    l = lbl_ref[:, col:col + HG_DK]
    e = jnp.exp(l - jnp.max(l, axis=0, keepdims=True))
    return e[0:1] / jnp.sum(e, axis=0, keepdims=True)


def _hgrn_kernel(q_ref, f_ref, i_ref, og_ref, lbl_ref, gain_ref, o_ref, s_ref, st_ref, *, nch):
    t = pl.program_id(1)

    @pl.when(t == 0)
    def _():
        st_ref[...] = jnp.zeros_like(st_ref)

    def body(ci, carry):
        r0 = pl.multiple_of(ci * CHUNK, CHUNK)
        rs = pl.ds(r0, CHUNK)
        for h in range(HG_H):
            cs = slice(h * HG_DK, (h + 1) * HG_DK)
            lb = _lower_bound(lbl_ref, h * HG_DK)
            q = _silu(q_ref[rs, cs])
            f = lb + (1.0 - lb) * _sigmoid(f_ref[rs, cs])
            o, st_new = _gla_chunk(q, 1.0 - f, i_ref[rs, cs], jnp.log(f), st_ref[h])
            st_ref[h] = st_new
            o = _rms(o, gain_ref[...]) * _sigmoid(og_ref[rs, cs])
            o_ref[rs, cs] = o.astype(o_ref.dtype)
        return carry

    lax.fori_loop(0, nch, body, 0)

    @pl.when(t == pl.num_programs(1) - 1)
    def _():
        for h in range(HG_H):
            s_ref[0, h] = st_ref[h].T


def _hgrn_prompt(p2d, lb_logits, gain, bsz, seq, tb):
    nt = seq // tb
    hk = HG_H * HG_DK
    row = lambda b, t: b * nt + t
    col_spec = lambda c: pl.BlockSpec((tb, hk), lambda b, t, c=c: (row(b, t), c))
    return pl.pallas_call(
        functools.partial(_hgrn_kernel, nch=tb // CHUNK),
        grid=(bsz, nt),
        in_specs=[col_spec(EV_Q // hk), col_spec(EV_F // hk), col_spec(EV_I // hk), col_spec(EV_OG // hk),
                  pl.BlockSpec(lb_logits.shape, lambda b, t: (0, 0)),
                  pl.BlockSpec((1, HG_DV), lambda b, t: (0, 0))],
        out_specs=[pl.BlockSpec((tb, hk), lambda b, t: (row(b, t), 0)),
                   pl.BlockSpec((1, HG_H, HG_DK, HG_DV), lambda b, t: (b, 0, 0, 0))],
        out_shape=[jax.ShapeDtypeStruct((bsz * seq, hk), BF16),
                   jax.ShapeDtypeStruct((bsz, HG_H, HG_DK, HG_DV), F32)],
        scratch_shapes=[pltpu.VMEM((HG_H, HG_DV, HG_DK), F32)],
        compiler_params=_cparams(("parallel", "arbitrary")),
        name="hgrn_chunk",
    )(p2d, p2d, p2d, p2d, lb_logits, gain.reshape(1, HG_DV))


def _log_sigmoid(x):
    return jnp.minimum(x, 0.0) - jnp.log(1.0 + jnp.exp(-jnp.abs(x)))


def _gla_kernel(q_ref, k_ref, v_ref, r_ref, a_ref, wa_ref, ba_ref, gain_ref, o_ref, s_ref, st_ref, *, nch):
    t = pl.program_id(1)

    @pl.when(t == 0)
    def _():
        st_ref[...] = jnp.zeros_like(st_ref)

    def body(ci, carry):
        r0 = pl.multiple_of(ci * CHUNK, CHUNK)
        rs = pl.ds(r0, CHUNK)
        gate = jnp.dot(a_ref[rs, :].astype(BF16), wa_ref[...], preferred_element_type=F32) + ba_ref[...]
        logf = _log_sigmoid(gate) * (1.0 / GLA_TAU)
        for h in range(GLA_H):
            ks = slice(h * GLA_DK, (h + 1) * GLA_DK)
            vs = slice(h * GLA_DV, (h + 1) * GLA_DV)
            q = q_ref[rs, ks] * (GLA_DK ** -0.5)
            o, st_new = _gla_chunk(q, k_ref[rs, ks], v_ref[rs, vs], logf[:, ks], st_ref[h])
            st_ref[h] = st_new
            o = _rms(o, gain_ref[...]) * _silu(r_ref[rs, vs])
            o_ref[rs, vs] = o.astype(o_ref.dtype)
        return carry

    lax.fori_loop(0, nch, body, 0)

    @pl.when(t == pl.num_programs(1) - 1)
    def _():
        for h in range(GLA_H):
            s_ref[0, h] = st_ref[h].T


def _gla_prompt(p2d, wa2p, ba, gain, bsz, seq, tb):
    nt = seq // tb
    hk, hv = GLA_H * GLA_DK, GLA_H * GLA_DV
    row = lambda b, t: b * nt + t
    return pl.pallas_call(
        functools.partial(_gla_kernel, nch=tb // CHUNK),
        grid=(bsz, nt),
        in_specs=[pl.BlockSpec((tb, hk), lambda b, t: (row(b, t), OD_Q // hk)),
                  pl.BlockSpec((tb, hk), lambda b, t: (row(b, t), OD_K // hk)),
                  pl.BlockSpec((tb, hv), lambda b, t: (row(b, t), OD_V // hv)),
                  pl.BlockSpec((tb, hv), lambda b, t: (row(b, t), OD_R // hv)),
                  pl.BlockSpec((tb, LANES), lambda b, t: (row(b, t), OD_A // LANES)),
                  pl.BlockSpec(wa2p.shape, lambda b, t: (0, 0)),
                  pl.BlockSpec((1, hk), lambda b, t: (0, 0)),
                  pl.BlockSpec((1, GLA_DV), lambda b, t: (0, 0))],
        out_specs=[pl.BlockSpec((tb, hv), lambda b, t: (row(b, t), 0)),
                   pl.BlockSpec((1, GLA_H, GLA_DK, GLA_DV), lambda b, t: (b, 0, 0, 0))],
        out_shape=[jax.ShapeDtypeStruct((bsz * seq, hv), BF16),
                   jax.ShapeDtypeStruct((bsz, GLA_H, GLA_DK, GLA_DV), F32)],
        scratch_shapes=[pltpu.VMEM((GLA_H, GLA_DV, GLA_DK), F32)],
        compiler_params=_cparams(("parallel", "arbitrary")),
        name="gla_chunk",
    )(p2d, p2d, p2d, p2d, p2d, wa2p, ba.reshape(1, hk), gain.reshape(1, GLA_DV))


def _to_columns(x):
    bsz = x.shape[0]
    if bsz < LANES:
        x = jnp.concatenate([x, jnp.zeros((LANES - bsz, x.shape[1]), x.dtype)], axis=0)
    return x.T


def _decode_update(q, k, v, g, s_ref, so_ref, o_scr):
    bsz = q.shape[0]
    qt, kt, et = _to_columns(q), _to_columns(k), _to_columns(jnp.exp(g))
    for b in range(bsz):
        s_new = et[:, b:b + 1] * s_ref[b, 0] + kt[:, b:b + 1] * v[b:b + 1, :]
        so_ref[b, 0] = s_new
        o_scr[b:b + 1, :] = jnp.sum(qt[:, b:b + 1] * s_new, axis=0, keepdims=True)


def _hgrn_decode_kernel(q_ref, f_ref, i_ref, og_ref, lbl_ref, gain_ref, s_ref, o_ref, so_ref, o_scr):
    l = lbl_ref[0]
    e = jnp.exp(l - jnp.max(l, axis=0, keepdims=True))
    lb = e[0:1] / jnp.sum(e, axis=0, keepdims=True)
    f = lb + (1.0 - lb) * _sigmoid(f_ref[...])
    _decode_update(_silu(q_ref[...]), 1.0 - f, i_ref[...], jnp.log(f), s_ref, so_ref, o_scr)
    o_ref[...] = _rms(o_scr[...], gain_ref[...]) * _sigmoid(og_ref[...])


def _hgrn_decode(p2d, lb_logits, gain, state):
    bsz = p2d.shape[0]
    col = lambda c: pl.BlockSpec((bsz, HG_DK), lambda h, c=c: (0, c + h))
    lbl3 = lb_logits.reshape(lb_logits.shape[0], HG_H, HG_DK).transpose(1, 0, 2)
    st_spec = pl.BlockSpec((bsz, 1, HG_DK, HG_DV), lambda h: (0, h, 0, 0))
    return pl.pallas_call(
        _hgrn_decode_kernel,
        grid=(HG_H,),
        in_specs=[col(EV_Q // HG_DK), col(EV_F // HG_DK), col(EV_I // HG_DK), col(EV_OG // HG_DK),
                  pl.BlockSpec((1,) + lbl3.shape[1:], lambda h: (h, 0, 0)),
                  pl.BlockSpec((1, HG_DV), lambda h: (0, 0)),
                  st_spec],
        out_specs=[pl.BlockSpec((bsz, HG_DV), lambda h: (0, h)), st_spec],
        out_shape=[jax.ShapeDtypeStruct((bsz, HG_H * HG_DV), F32),
                   jax.ShapeDtypeStruct(state.shape, F32)],
        scratch_shapes=[pltpu.VMEM((bsz, HG_DV), F32)],
        compiler_params=_cparams(("arbitrary",)),
        name="hgrn_decode",
    )(p2d, p2d, p2d, p2d, lbl3, gain.reshape(1, HG_DV), state)


def _gla_decode_kernel(q_ref, k_ref, v_ref, r_ref, a_ref, wa_ref, ba_ref, gain_ref, s_ref, o_ref, so_ref, o_scr):
    gate = jnp.dot(a_ref[...].astype(BF16), wa_ref[...], preferred_element_type=F32) + ba_ref[...]
    logf = _log_sigmoid(gate) * (1.0 / GLA_TAU)
    _decode_update(q_ref[...] * (GLA_DK ** -0.5), k_ref[...], v_ref[...], logf, s_ref, so_ref, o_scr)
    o_ref[...] = _rms(o_scr[...], gain_ref[...]) * _silu(r_ref[...])


def _gla_decode(p2d, wa2p, ba, gain, state):
    bsz = p2d.shape[0]
    hk = GLA_H * GLA_DK
    st_spec = pl.BlockSpec((bsz, 1, GLA_DK, GLA_DV), lambda h: (0, h, 0, 0))
    return pl.pallas_call(
        _gla_decode_kernel,
        grid=(GLA_H,),
        in_specs=[pl.BlockSpec((bsz, GLA_DK), lambda h: (0, OD_Q // GLA_DK + h)),
                  pl.BlockSpec((bsz, GLA_DK), lambda h: (0, OD_K // GLA_DK + h)),
                  pl.BlockSpec((bsz, GLA_DV), lambda h: (0, OD_V // GLA_DV + h)),
                  pl.BlockSpec((bsz, GLA_DV), lambda h: (0, OD_R // GLA_DV + h)),
                  pl.BlockSpec((bsz, LANES), lambda h: (0, OD_A // LANES)),
                  pl.BlockSpec((LANES, GLA_DK), lambda h: (0, h)),
                  pl.BlockSpec((1, GLA_DK), lambda h: (0, h)),
                  pl.BlockSpec((1, GLA_DV), lambda h: (0, 0)),
                  st_spec],
        out_specs=[pl.BlockSpec((bsz, GLA_DV), lambda h: (0, h)), st_spec],
        out_shape=[jax.ShapeDtypeStruct((bsz, GLA_H * GLA_DV), F32),
                   jax.ShapeDtypeStruct(state.shape, F32)],
        scratch_shapes=[pltpu.VMEM((bsz, GLA_DV), F32)],
        compiler_params=_cparams(("arbitrary",)),
        name="gla_decode",
    )(p2d, p2d, p2d, p2d, p2d, wa2p, ba.reshape(1, hk), gain.reshape(1, GLA_DV), state)


def _compress_kernel(x_ref, pe_ref, w1_ref, w2_ref, o_ref, acc_ref):
    l = pl.program_id(0)

    @pl.when(l == 0)
    def _():
        acc_ref[...] = jnp.zeros_like(acc_ref)

    x = x_ref[...] + pe_ref[pl.ds(l, 1), :]
    acc_ref[...] += jnp.dot(x.astype(BF16), w1_ref[...], preferred_element_type=F32)

    @pl.when(l == pl.num_programs(0) - 1)
    def _():
        o_ref[...] = jnp.dot(_silu(acc_ref[...]).astype(BF16), w2_ref[...], preferred_element_type=F32)


def _compress_prompt(p2d, pe4, w1bd, w2bd):
    nblk = p2d.shape[0] // NSA_BLK
    cw = 4 * NSA_DH
    x2 = p2d.reshape(nblk, NSA_BLK * PROJ_W)
    per_row = PROJ_W // cw
    return pl.pallas_call(
        _compress_kernel,
        grid=(NSA_BLK,),
        in_specs=[pl.BlockSpec((nblk, cw), lambda l: (0, l * per_row + EV_CMP // cw)),
                  pl.BlockSpec(pe4.shape, lambda l: (0, 0)),
                  pl.BlockSpec((None, cw, cw), lambda l: (l, 0, 0)),
                  pl.BlockSpec((cw, cw), lambda l: (0, 0))],
        out_specs=pl.BlockSpec((nblk, cw), lambda l: (0, 0)),
        out_shape=jax.ShapeDtypeStruct((nblk, cw), F32),
        scratch_shapes=[pltpu.VMEM((nblk, cw), F32)],
        compiler_params=_cparams(("arbitrary",)),
        name="nsa_compress",
    )(x2, pe4, w1bd, w2bd)


def _slopes_lane(g, lanes):
    r = _iota((1, lanes), 1) >> 7
    out = jnp.zeros((1, lanes), F32)
    for rr in range(NSA_R):
        out = jnp.where(r == rr, 2.0 ** (-(g * NSA_R + rr + 1)), out)
    return out


def _online_tile(s, valid, vt, carry):
    m, l, acc = carry
    m_new = jnp.maximum(m, jnp.max(jnp.where(valid, s, NEG), axis=0, keepdims=True))
    alpha = jnp.exp(m - m_new)
    p = jnp.where(valid, jnp.exp(s - m_new), 0.0)
    l = alpha * l + jnp.sum(p, axis=0, keepdims=True)
    acc = alpha * acc + jnp.dot(vt, p.astype(BF16), preferred_element_type=F32)
    return m_new, l, acc


def _nsa_attn_kernel(q_ref, kc_ref, vct_ref, ks_ref, vst_ref, kw_ref, vwt_ref, gt_ref, o_ref, *, nblk):
    qb = pl.program_id(1)
    q0 = qb * NSA_QB
    nq = NSA_R * NSA_QB
    pq = q0 + (_iota((1, nq), 1) & (NSA_QB - 1))
    gates_t = _sigmoid(gt_ref[...]).T
    ksel = min(NSA_TOPK, nblk)
    scale = NSA_DH ** -0.5
    init = (jnp.full((1, nq), NEG, F32), jnp.zeros((1, nq), F32), jnp.zeros((NSA_DH, nq), F32))

    for g in range(NSA_G):
        slope = _slopes_lane(g, nq)
        qg = q_ref[:, g * NSA_R * NSA_DH:(g + 1) * NSA_R * NSA_DH] * scale
        qs = jnp.concatenate([qg[:, r * NSA_DH:(r + 1) * NSA_DH] for r in range(NSA_R)], axis=0).astype(BF16)

        sc = lax.dot_general(kc_ref[0, g], qs, NT_DIMS, preferred_element_type=F32)
        dist_c = pq - (_iota((nblk, nq), 0) * NSA_BLK + NSA_BLK - 1)
        mask_c = dist_c >= 0
        sc = jnp.where(mask_c, sc - slope * dist_c.astype(F32), NEG)
        pc = jnp.where(mask_c, jnp.exp(sc - jnp.max(sc, axis=0, keepdims=True)), 0.0)
        pc = pc / jnp.maximum(jnp.sum(pc, axis=0, keepdims=True), TINY)
        o_cmp = jnp.dot(vct_ref[0, g], pc.astype(BF16), preferred_element_type=F32)
        imp = pc[:, 0:NSA_QB]
        for r in range(1, NSA_R):
            imp = imp + pc[:, r * NSA_QB:(r + 1) * NSA_QB]

        n_io = _iota((nblk, NSA_QB), 0)
        tpos = q0 + _iota((nblk, NSA_QB), 1)
        forced = (n_io == (tpos >> 6)) | (n_io == 0)
        score = jnp.where(forced, jnp.inf, jnp.where(n_io * NSA_BLK <= tpos, imp, -jnp.inf))
        rank = jnp.zeros((nblk, NSA_QB), I32)
        for i in range(nblk):
            row = score[i:i + 1, :]
            rank = rank + ((row > score) | ((row == score) & (i < n_io))).astype(I32)
        sel = (rank < ksel).astype(BF16)

        def slc_body(kt, carry, g=g, qs=qs, slope=slope, sel=sel):
            k0 = pl.multiple_of(kt * 256, 256)
            s = lax.dot_general(ks_ref[0, g, pl.ds(k0, 256), :], qs, NT_DIMS, preferred_element_type=F32)
            kpos = k0 + _iota((256, nq), 0)
            s = s + slope * kpos.astype(F32)
            expand = (((k0 + _iota((256, nblk), 0)) >> 6) == _iota((256, nblk), 1)).astype(BF16)
            selx = jnp.dot(expand, sel, preferred_element_type=F32)
            selx = jnp.concatenate([selx] * NSA_R, axis=1)
            valid = (selx > 0.5) & (kpos <= pq)
            return _online_tile(s, valid, vst_ref[0, g, :, pl.ds(k0, 256)], carry)

        m, l, acc = lax.fori_loop(0, qb // 2 + 1, slc_body, init)
        o_slc = acc / jnp.maximum(l, TINY)

        def win_body(kt, carry, g=g, qs=qs, slope=slope):
            k0 = pl.multiple_of(kt * NSA_QB, NSA_QB)
            s = lax.dot_general(kw_ref[0, g, pl.ds(k0, NSA_QB), :], qs, NT_DIMS, preferred_element_type=F32)
            kpos = k0 + _iota((NSA_QB, nq), 0)
            s = s + slope * kpos.astype(F32)
            d = pq - kpos
            valid = (d >= 0) & (d < NSA_WINDOW)
            return _online_tile(s, valid, vwt_ref[0, g, :, pl.ds(k0, NSA_QB)], carry)

        m, l, acc = lax.fori_loop(jnp.maximum(qb - NSA_WINDOW // NSA_QB, 0), qb + 1, win_body, init)
        o_win = acc / jnp.maximum(l, TINY)

        def gate_row(j, g=g):
            return jnp.concatenate([gates_t[g * 12 + r * 3 + j:g * 12 + r * 3 + j + 1, :] for r in range(NSA_R)],
                                   axis=1)

        o_t = gate_row(0) * o_cmp + gate_row(1) * o_slc + gate_row(2) * o_win
        o_st = jnp.concatenate([o_t[:, r * NSA_QB:(r + 1) * NSA_QB] for r in range(NSA_R)], axis=0)
        o_ref[:, g * NSA_R * NSA_DH:(g + 1) * NSA_R * NSA_DH] = o_st.T.astype(o_ref.dtype)


def _nsa_attn_prompt(p2d, kc, vct, ks, vst, kw, vwt, bsz, seq):
    nqb = seq // NSA_QB
    nblk = seq // NSA_BLK
    qw = NSA_H * NSA_DH
    row = lambda b, i: b * nqb + i
    full = lambda a: pl.BlockSpec((1,) + a.shape[1:], lambda b, i: (b, 0, 0, 0))
    return pl.pallas_call(
        functools.partial(_nsa_attn_kernel, nblk=nblk),
        grid=(bsz, nqb),
        in_specs=[pl.BlockSpec((NSA_QB, qw), lambda b, i: (row(b, i), EV_NQ // qw)),
                  full(kc), full(vct), full(ks), full(vst), full(kw), full(vwt),
                  pl.BlockSpec((NSA_QB, LANES), lambda b, i: (row(b, i), EV_GATE // LANES))],
        out_specs=pl.BlockSpec((NSA_QB, qw), lambda b, i: (row(b, i), 0)),
        out_shape=jax.ShapeDtypeStruct((bsz * seq, qw), BF16),
        compiler_params=_cparams(("parallel", "arbitrary")),
        name="nsa_attn",
    )(p2d, kc, vct, ks, vst, kw, vwt, p2d)


QROWS = 16


def _head_slopes(rows):
    r = _iota((rows, 1), 0)
    out = jnp.zeros((rows, 1), F32)
    for h in range(NSA_H):
        out = jnp.where(r == h, 2.0 ** (-(h + 1)), out)
    return out


def _softmax_rows(s, valid):
    s = jnp.where(valid, s, NEG)
    p = jnp.where(valid, jnp.exp(s - jnp.max(s, axis=1, keepdims=True)), 0.0)
    return p / jnp.maximum(jnp.sum(p, axis=1, keepdims=True), TINY)


def _padded_queries(q_ref):
    q = q_ref[0] * (NSA_DH ** -0.5)
    return jnp.concatenate([q, jnp.zeros((QROWS - NSA_H, NSA_DH), F32)], axis=0).astype(BF16)


def _dec_cmp_kernel(pt_ref, cache_ref, pe_ref, w1_ref, w2_ref, q_ref, o_ref, sel_ref, buf, acc_ref, sem,
                    *, npages, nsel):
    b = pl.program_id(0)
    nb = npages * (PAGE // NSA_BLK)
    cw = 4 * NSA_DH

    def page_copy(pg, hf):
        return pltpu.make_async_copy(cache_ref.at[pt_ref[b * npages + pg], :, pl.ds(hf * LANES, LANES)],
                                     buf.at[hf, pl.ds(pg * PAGE, PAGE), :], sem.at[0])

    def start(pg, c):
        page_copy(pg, 0).start()
        page_copy(pg, 1).start()
        return c

    def wait(pg, c):
        page_copy(pg, 0).wait()
        page_copy(pg, 1).wait()
        return c

    lax.fori_loop(0, npages, start, 0)
    lax.fori_loop(0, npages, wait, 0)

    acc_ref[...] = jnp.zeros_like(acc_ref)
    for l in range(NSA_BLK):
        rows = pl.ds(l, nb, stride=NSA_BLK)
        x = jnp.concatenate([buf[0, rows, :], buf[1, rows, :]], axis=1) + pe_ref[l:l + 1, :]
        acc_ref[...] += jnp.dot(x.astype(BF16), w1_ref[l], preferred_element_type=F32)
    cmp = jnp.dot(_silu(acc_ref[...]).astype(BF16), w2_ref[...], preferred_element_type=F32)

    pos = nb * NSA_BLK
    q16 = _padded_queries(q_ref)
    slope = _head_slopes(QROWS)
    row_grp = _iota((QROWS, 1), 0) >> 2
    dist = pos - (_iota((QROWS, nb), 1) * NSA_BLK + NSA_BLK - 1)
    ri, ci = _iota((nb, nb), 0), _iota((nb, nb), 1)
    o_all = jnp.zeros((QROWS, NSA_DH), F32)
    for g in range(NSA_G):
        kc = cmp[:, g * NSA_DH:(g + 1) * NSA_DH].astype(BF16)
        vc = cmp[:, (NSA_G + g) * NSA_DH:(NSA_G + g + 1) * NSA_DH].astype(BF16)
        s = lax.dot_general(q16, kc, NT_DIMS, preferred_element_type=F32)
        p = _softmax_rows(s - slope * dist.astype(F32), dist >= 0)
        o_all = jnp.where(row_grp == g, jnp.dot(p.astype(BF16), vc, preferred_element_type=F32), o_all)
        imp = jnp.sum(jnp.where(row_grp == g, p, 0.0), axis=0, keepdims=True)
        score_row = jnp.where(_iota((1, nb), 1) == 0, jnp.inf, imp)
        score_col = jnp.sum(jnp.where(ri == ci, jnp.broadcast_to(score_row, (nb, nb)), 0.0),
                            axis=1, keepdims=True)
        beats = (score_col > score_row) | ((score_col == score_row) & (ri < ci))
        rank = jnp.sum(beats.astype(I32), axis=0, keepdims=True)
        sel_ref[0, g:g + 1, :] = (rank < nsel).astype(F32)
    o_ref[0] = o_all[:NSA_H]


def _dec_cmp(page_table, cache3, pe4, w1bd, w2bd, q3, nsel):
    bsz, npages = page_table.shape
    nb = npages * (PAGE // NSA_BLK)
    cw = 4 * NSA_DH
    grid_spec = pltpu.PrefetchScalarGridSpec(
        num_scalar_prefetch=1,
        grid=(bsz,),
        in_specs=[pl.BlockSpec(memory_space=pl.ANY),
                  pl.BlockSpec(pe4.shape, lambda b, pt: (0, 0)),
                  pl.BlockSpec(w1bd.shape, lambda b, pt: (0, 0, 0)),
                  pl.BlockSpec(w2bd.shape, lambda b, pt: (0, 0)),
                  pl.BlockSpec((1, NSA_H, NSA_DH), lambda b, pt: (b, 0, 0))],
        out_specs=[pl.BlockSpec((1, NSA_H, NSA_DH), lambda b, pt: (b, 0, 0)),
                   pl.BlockSpec((1, NSA_G, nb), lambda b, pt: (b, 0, 0))],
        scratch_shapes=[pltpu.VMEM((cw // LANES, npages * PAGE, LANES), F32), pltpu.VMEM((nb, cw), F32),
                        pltpu.SemaphoreType.DMA((1,))])
    return pl.pallas_call(
        functools.partial(_dec_cmp_kernel, npages=npages, nsel=nsel),
        grid_spec=grid_spec,
        out_shape=[jax.ShapeDtypeStruct((bsz, NSA_H, NSA_DH), F32),
                   jax.ShapeDtypeStruct((bsz, NSA_G, nb), F32)],
        compiler_params=_cparams(("arbitrary",)),
        name="nsa_dec_cmp",
    )(page_table.reshape(-1), cache3, pe4, w1bd, w2bd, q3)


def _dec_attn_kernel(pt_ref, idx_ref, cache_ref, q_ref, oc_ref, kn_ref, win_ref, gt_ref, o_ref, sbuf, wbuf, sem,
                     *, npages, nsel, wlen):
    b = pl.program_id(0)
    nb = npages * (PAGE // NSA_BLK)
    pos = nb * NSA_BLK
    nk = nsel * NSA_BLK
    nkp = sbuf.shape[1]
    half = 2 * NSA_G * NSA_DH

    copies = []
    for g in range(NSA_G):
        for j in range(nsel):
            blk = idx_ref[(b * NSA_G + g) * NSA_TOPK + j]
            page = pt_ref[b * npages + (blk >> 1)]
            copies.append(pltpu.make_async_copy(
                cache_ref.at[page, pl.ds((blk & 1) * NSA_BLK, NSA_BLK), pl.ds(half, half)],
                sbuf.at[g, pl.ds(j * NSA_BLK, NSA_BLK), :], sem.at[0]))
    for cp in copies:
        cp.start()

    kn = kn_ref[0]
    tail = _iota((nkp - nk, half), 0) == 0
    for g in range(NSA_G):
        sbuf[g, nk:nkp, :] = jnp.where(tail, kn[0:1, :], 0.0)
    wbuf[0:wlen, :] = win_ref[0]
    wbuf[wlen:wlen + LANES, :] = jnp.where(_iota((LANES, half), 0) == 0, kn[1:2, :], 0.0)
    for cp in copies:
        cp.wait()

    q16 = _padded_queries(q_ref)
    slope = _head_slopes(QROWS)
    row_grp = _iota((QROWS, 1), 0) >> 2
    lane = _iota((1, nkp), 1)
    wdist = wlen - _iota((1, wlen + LANES), 1)
    o_slc = jnp.zeros((QROWS, NSA_DH), F32)
    o_win = jnp.zeros((QROWS, NSA_DH), F32)
    for g in range(NSA_G):
        ksl = slice(g * NSA_DH, (g + 1) * NSA_DH)
        vsl = slice((NSA_G + g) * NSA_DH, (NSA_G + g + 1) * NSA_DH)
        blk_lane = jnp.full((1, nkp), nb, I32)
        for j in range(nsel):
            blk_lane = jnp.where((lane >> 6) == j, idx_ref[(b * NSA_G + g) * NSA_TOPK + j], blk_lane)
        kpos = blk_lane * NSA_BLK + (lane & (NSA_BLK - 1))
        slab = sbuf[g]
        s = lax.dot_general(q16, slab[:, ksl].astype(BF16), NT_DIMS, preferred_element_type=F32)
        p = _softmax_rows(s - slope * (pos - kpos).astype(F32), lane <= nk)
        o_slc = jnp.where(row_grp == g, jnp.dot(p.astype(BF16), slab[:, vsl].astype(BF16),
                                                preferred_element_type=F32), o_slc)
        w = wbuf[...]
        s = lax.dot_general(q16, w[:, ksl].astype(BF16), NT_DIMS, preferred_element_type=F32)
        p = _softmax_rows(s - slope * wdist.astype(F32), (wdist >= 0) & (wdist < NSA_WINDOW))
        o_win = jnp.where(row_grp == g, jnp.dot(p.astype(BF16), w[:, vsl].astype(BF16),
                                                preferred_element_type=F32), o_win)

    gates = _sigmoid(gt_ref[pl.ds(b, 1), :])
    r_io = _iota((QROWS, LANES), 0)
    l_io = _iota((QROWS, LANES), 1)

    def gate_col(j):
        return jnp.sum(jnp.where(l_io == 3 * r_io + j, gates, 0.0), axis=1, keepdims=True)

    o_cmp = jnp.concatenate([oc_ref[0], jnp.zeros((QROWS - NSA_H, NSA_DH), F32)], axis=0)
    o = gate_col(0) * o_cmp + gate_col(1) * o_slc + gate_col(2) * o_win
    o_ref[0] = o[:NSA_H]


def _dec_attn(page_table, idx, cache3, q3, ocmp, knew, win3, gates, nsel):
    bsz, npages = page_table.shape
    wlen = win3.shape[1]
    nkp = (nsel + 1) * NSA_BLK
    half = 2 * NSA_G * NSA_DH
    hd = pl.BlockSpec((1, NSA_H, NSA_DH), lambda b, pt, ix: (b, 0, 0))
    grid_spec = pltpu.PrefetchScalarGridSpec(
        num_scalar_prefetch=2,
        grid=(bsz,),
        in_specs=[pl.BlockSpec(memory_space=pl.ANY), hd, hd,
                  pl.BlockSpec((1,) + knew.shape[1:], lambda b, pt, ix: (b, 0, 0)),
                  pl.BlockSpec((1,) + win3.shape[1:], lambda b, pt, ix: (b, 0, 0)),
                  pl.BlockSpec(gates.shape, lambda b, pt, ix: (0, 0))],
        out_specs=hd,
        scratch_shapes=[pltpu.VMEM((NSA_G, nkp, half), F32), pltpu.VMEM((wlen + LANES, half), F32),
                        pltpu.SemaphoreType.DMA((1,))])
    return pl.pallas_call(
        functools.partial(_dec_attn_kernel, npages=npages, nsel=nsel, wlen=wlen),
        grid_spec=grid_spec,
        out_shape=jax.ShapeDtypeStruct((bsz, NSA_H, NSA_DH), F32),
        compiler_params=_cparams(("arbitrary",)),
        name="nsa_dec_attn",
    )(page_table.reshape(-1), idx.reshape(-1), cache3, q3, ocmp, knew, win3, gates)


def _prep_weights(w_in_even, nsa_cmp_pos, nsa_cmp_w1, nsa_cmp_w2, w_out_even, w_in_odd, gla_wa2, w_out_odd,
                  ffn_w13, ffn_w2):
    d = D_MODEL
    wie = jnp.pad(w_in_even[0], ((0, 0), (0, PROJ_W - w_in_even.shape[2]))).astype(BF16)
    a0 = 2 * GLA_H * GLA_DK + GLA_H * GLA_DV
    wo = w_in_odd[0]
    wio = jnp.concatenate([wo[:, :a0], wo[:, a0 + GLA_RANK:], wo[:, a0:a0 + GLA_RANK]], axis=1)
    wio = jnp.pad(wio, ((0, 0), (0, PROJ_W - wio.shape[1]))).astype(BF16)
    wa2p = jnp.pad(gla_wa2[0], ((0, LANES - GLA_RANK), (0, 0))).astype(BF16)
    hv = HG_H * HG_DV
    woe_h, woe_n = w_out_even[0, :hv].astype(BF16), w_out_even[0, hv:].astype(BF16)
    woo = w_out_odd[0].astype(BF16)
    nj = D_FF // FF_TILE
    w13t, w2b = [], []
    for l in range(ffn_w13.shape[0]):
        gate = ffn_w13[l, :, :D_FF].reshape(d, nj, FF_TILE)
        up = ffn_w13[l, :, D_FF:].reshape(d, nj, FF_TILE)
        w13t.append(jnp.concatenate([gate, up], axis=2).reshape(d, 2 * D_FF).astype(BF16))
        w2b.append(ffn_w2[l].astype(BF16))
    w1 = nsa_cmp_w1[0].reshape(2, NSA_BLK, NSA_DH, NSA_DH)
    w2 = nsa_cmp_w2[0]
    cw = 4 * NSA_DH
    w1bd = jnp.zeros((NSA_BLK, cw, cw), F32)
    w2bd = jnp.zeros((cw, cw), F32)
    for c in range(2):
        for g in range(NSA_G):
            s = slice((c * NSA_G + g) * NSA_DH, (c * NSA_G + g + 1) * NSA_DH)
            w1bd = w1bd.at[:, s, s].set(w1[c])
            w2bd = w2bd.at[s, s].set(w2[c])
    pe = nsa_cmp_pos[0]
    pe4 = jnp.concatenate([pe[0], pe[0], pe[1], pe[1]], axis=1)
    return dict(wie=wie, wio=wio, wa2p=wa2p, woe_h=woe_h, woe_n=woe_n, woo=woo, w13t=w13t, w2b=w2b,
                w1bd=w1bd.astype(BF16), w2bd=w2bd.astype(BF16), pe4=pe4)


def _group_heads(x2d, col, bsz, seq):
    return x2d[:, col:col + NSA_G * NSA_DH].reshape(bsz, seq, NSA_G, NSA_DH).transpose(0, 2, 1, 3)


def _forward_prompt(x, pw, hg_lb_logits, hg_norm, gla_ba, gla_norm, norm_mix, norm_ffn, norm_final):
    bsz, seq, d = x.shape
    n = bsz * seq
    x2d = x.reshape(n, d)
    tm = min(512, n)
    tmf = min(1024, n)
    tb = min(512, seq)

    p = _norm_proj(x2d, norm_mix[0], pw["wie"], tm, PROJ_W // 2)
    o_h, hg_state = _hgrn_prompt(p, hg_lb_logits, hg_norm[0], bsz, seq, tb)
    cmp = _compress_prompt(p, pw["pe4"], pw["w1bd"], pw["w2bd"])
    nblk = seq // NSA_BLK
    cmp4 = cmp.reshape(bsz, nblk, 2 * NSA_G, NSA_DH).transpose(0, 2, 1, 3)
    kc = cmp4[:, :NSA_G].astype(BF16)
    vct = cmp4[:, NSA_G:].transpose(0, 1, 3, 2).astype(BF16)
    ks = _group_heads(p, EV_SLC, bsz, seq).astype(BF16)
    vst = _group_heads(p, EV_SLC + NSA_G * NSA_DH, bsz, seq).transpose(0, 1, 3, 2).astype(BF16)
    kw = _group_heads(p, EV_WIN, bsz, seq).astype(BF16)
    vwt = _group_heads(p, EV_WIN + NSA_G * NSA_DH, bsz, seq).transpose(0, 1, 3, 2).astype(BF16)
    o_n = _nsa_attn_prompt(p, kc, vct, ks, vst, kw, vwt, bsz, seq)
    x1 = _out_ffn([o_h, o_n], x2d, [pw["woe_h"], pw["woe_n"]], norm_ffn[0], pw["w13t"][0], pw["w2b"][0], None, tmf)

    rows = p[:, EV_CMP:EV_WIN].reshape(1, bsz, seq, 4, NSA_G, NSA_DH)
    wn = min(NSA_WINDOW, seq)
    win = p[:, EV_WIN:EV_GATE].reshape(bsz, seq, 2, NSA_G, NSA_DH)[None, :, seq - wn:]

    p2 = _norm_proj(x1, norm_mix[1], pw["wio"], tm, PROJ_W // 2)
    o_g, gla_state = _gla_prompt(p2, pw["wa2p"], gla_ba[0], gla_norm[0], bsz, seq, tb)
    y = _out_ffn([o_g], x1, [pw["woo"]], norm_ffn[1], pw["w13t"][1], pw["w2b"][1], norm_final, tmf)
    return y.reshape(bsz, seq, d), rows, win, hg_state[None], gla_state[None]


def _forward_sample(x, pw, cache_nsa_kv, cache_win_kv, state_hgrn, state_gla, page_table, hg_lb_logits, hg_norm,
                    gla_ba, gla_norm, norm_mix, norm_ffn, norm_final):
    bsz, seq, d = x.shape
    x2d = x.reshape(bsz, d)
    npages = page_table.shape[1]
    nblk_total = npages * (PAGE // NSA_BLK) + 1
    nsel = min(NSA_TOPK, nblk_total) - 1

    p = _norm_proj(x2d, norm_mix[0], pw["wie"], bsz, PROJ_W // 2)
    o_h, hg_state = _hgrn_decode(p, hg_lb_logits, hg_norm[0], state_hgrn[0])

    cache3 = cache_nsa_kv[0].reshape(cache_nsa_kv.shape[1], PAGE, 4 * NSA_G * NSA_DH)
    q3 = p[:, EV_NQ:EV_CMP].reshape(bsz, NSA_H, NSA_DH)
    ocmp, sel = _dec_cmp(page_table, cache3, pw["pe4"], pw["w1bd"], pw["w2bd"], q3, nsel)
    idx = jnp.argsort(-sel, axis=-1, stable=True)[..., :NSA_TOPK].astype(I32)
    knew = p[:, EV_SLC:EV_GATE].reshape(bsz, 2, 2 * NSA_G * NSA_DH)
    win3 = cache_win_kv[0].reshape(bsz, cache_win_kv.shape[2], 2 * NSA_G * NSA_DH)
    o_n = _dec_attn(page_table, idx, cache3, q3, ocmp, knew, win3, p[:, EV_GATE:EV_GATE + LANES], nsel)
    o_n = o_n.reshape(bsz, NSA_H * NSA_DH)
    x1 = _out_ffn([o_h, o_n], x2d, [pw["woe_h"], pw["woe_n"]], norm_ffn[0], pw["w13t"][0], pw["w2b"][0], None, bsz)

    rows = p[:, EV_CMP:EV_WIN].reshape(1, bsz, 1, 4, NSA_G, NSA_DH)
    win_new = p[:, EV_WIN:EV_GATE].reshape(1, bsz, 1, 2, NSA_G, NSA_DH)
    wk = jnp.concatenate([cache_win_kv[:1], win_new], axis=2)
    wn = min(NSA_WINDOW, npages * PAGE + 1)
    win = wk[:, :, wk.shape[2] - wn:]

    p2 = _norm_proj(x1, norm_mix[1], pw["wio"], bsz, PROJ_W // 2)
    o_g, gla_state = _gla_decode(p2, pw["wa2p"], gla_ba[0], gla_norm[0], state_gla[0])
    y = _out_ffn([o_g], x1, [pw["woo"]], norm_ffn[1], pw["w13t"][1], pw["w2b"][1], norm_final, bsz)
    return y.reshape(bsz, 1, d), rows, win, hg_state[None], gla_state[None]


def kernel(x_prompt, x_sample, cache_nsa_kv, cache_win_kv, state_hgrn, state_gla, page_table, w_in_even,
           hg_lb_logits, hg_norm, nsa_cmp_pos, nsa_cmp_w1, nsa_cmp_w2, w_out_even, w_in_odd, gla_wa2, gla_ba,
           gla_norm, w_out_odd, norm_mix, norm_ffn, norm_final, ffn_w13, ffn_w2):
    pw = _prep_weights(w_in_even, nsa_cmp_pos, nsa_cmp_w1, nsa_cmp_w2, w_out_even, w_in_odd, gla_wa2, w_out_odd,
                       ffn_w13, ffn_w2)
    y_p, kv_p, win_p, hg_p, gla_p = _forward_prompt(x_prompt, pw, hg_lb_logits, hg_norm, gla_ba, gla_norm,
                                                    norm_mix, norm_ffn, norm_final)
    y_s, kv_s, win_s, hg_s, gla_s = _forward_sample(x_sample, pw, cache_nsa_kv, cache_win_kv, state_hgrn,
                                                    state_gla, page_table, hg_lb_logits, hg_norm, gla_ba,
                                                    gla_norm, norm_mix, norm_ffn, norm_final)
    return (y_p, y_s, kv_p, kv_s, win_p, win_s, hg_p, hg_s, gla_p, gla_s)
```

```python
import functools

import jax
import jax.numpy as jnp
from jax import lax
from jax.experimental import pallas as pl
from jax.experimental.pallas import tpu as pltpu

F32 = jnp.float32
BF16 = jnp.bfloat16
I32 = jnp.int32

D_MODEL = 1024
HG_H, HG_DK, HG_DV = 4, 128, 128
NSA_H, NSA_DH, NSA_G, NSA_R = 8, 64, 2, 4
NSA_BLK = 64
NSA_TOPK = 16
NSA_WINDOW = 512
NSA_QB = 128
GLA_H, GLA_DK, GLA_DV = 4, 128, 256
GLA_RANK = 16
GLA_TAU = 16.0
D_FF = 2816
EPS = 1e-6
NEG = -1e30
TINY = 1e-30
PAGE = 128

PROJ_W = 3584
EV_Q, EV_F, EV_I, EV_OG = 0, 512, 1024, 1536
EV_NQ, EV_CMP, EV_SLC, EV_WIN, EV_GATE = 2048, 2560, 2816, 3072, 3328
OD_Q, OD_K, OD_V, OD_R, OD_A = 0, 512, 1024, 2048, 3072

LANES = 128
V7X_VMEM_LIMIT = 56 * 1024 * 1024
FF_TILE = 256
CHUNK = 128
SUB = 16

NT_DIMS = (((1,), (1,)), ((), ()))
TN_DIMS = (((0,), (0,)), ((), ()))


def _cparams(sem):
    return pltpu.CompilerParams(dimension_semantics=sem, vmem_limit_bytes=V7X_VMEM_LIMIT)


def _rms(x, g):
    return x * lax.rsqrt(jnp.mean(x * x, axis=-1, keepdims=True) + EPS) * g


def _sigmoid(x):
    return 1.0 / (1.0 + jnp.exp(-x))


def _silu(x):
    return x * _sigmoid(x)


def _iota(shape, dim):
    return lax.broadcasted_iota(I32, shape, dim)


def _norm_proj_kernel(x_ref, g_ref, w_ref, o_ref, xn_ref):
    @pl.when(pl.program_id(1) == 0)
    def _():
        xn_ref[...] = _rms(x_ref[...], g_ref[...]).astype(BF16)

    o_ref[...] = jnp.dot(xn_ref[...], w_ref[...], preferred_element_type=F32)


def _norm_proj(x2d, gain, w, tm, tn):
    m, k = x2d.shape
    n = w.shape[1]
    return pl.pallas_call(
        _norm_proj_kernel,
        grid=(m // tm, n // tn),
        in_specs=[pl.BlockSpec((tm, k), lambda i, j: (i, 0)),
                  pl.BlockSpec((1, k), lambda i, j: (0, 0)),
                  pl.BlockSpec((k, tn), lambda i, j: (0, j))],
        out_specs=pl.BlockSpec((tm, tn), lambda i, j: (i, j)),
        out_shape=jax.ShapeDtypeStruct((m, n), F32),
        scratch_shapes=[pltpu.VMEM((tm, k), BF16)],
        compiler_params=_cparams(("parallel", "arbitrary")),
        name="norm_proj",
    )(x2d, gain.reshape(1, k), w)


PM_W = 2304
PM_CMP = 2048
T_Q, T_ROWS, T_WIN, T_GATE, T_END = 0, 512, 1024, 1280, 1312


def _norm_proj_t_kernel(x_ref, g_ref, w_ref, wt_ref, o_ref, q_ref, rows_ref, win_ref, gate_ref):
    xn = _rms(x_ref[...], g_ref[...]).astype(BF16)
    o_ref[...] = jnp.dot(xn, w_ref[...], preferred_element_type=F32)
    t = lax.dot_general(wt_ref[...], xn, NT_DIMS, preferred_element_type=F32)
    q_ref[0] = t[T_Q:T_ROWS]
    rows_ref[0] = t[T_ROWS:T_WIN]
    win_ref[0] = t[T_WIN:T_GATE]
    gate_ref[0] = t[T_GATE:T_END]


def _norm_proj_t(x2d, gain, w, wt, bsz, seq, tm):
    k = x2d.shape[1]
    nt = seq // tm
    tr = lambda rows: pl.BlockSpec((1, rows, tm), lambda b, t: (b, 0, t))
    sizes = (T_ROWS - T_Q, T_WIN - T_ROWS, T_GATE - T_WIN, T_END - T_GATE)
    return pl.pallas_call(
        _norm_proj_t_kernel,
        grid=(bsz, nt),
        in_specs=[pl.BlockSpec((tm, k), lambda b, t: (b * nt + t, 0)),
                  pl.BlockSpec((1, k), lambda b, t: (0, 0)),
                  pl.BlockSpec(w.shape, lambda b, t: (0, 0)),
                  pl.BlockSpec(wt.shape, lambda b, t: (0, 0))],
        out_specs=[pl.BlockSpec((tm, PM_W), lambda b, t: (b * nt + t, 0))] + [tr(r) for r in sizes],
        out_shape=[jax.ShapeDtypeStruct((bsz * seq, PM_W), F32)]
        + [jax.ShapeDtypeStruct((bsz, r, seq), F32) for r in sizes],
        compiler_params=_cparams(("parallel", "arbitrary")),
        name="norm_proj_t",
    )(x2d, gain.reshape(1, k), w, wt)


def _out_ffn_kernel(*refs, n_mix, final_norm):
    mix_refs = refs[:n_mix]
    res_ref = refs[n_mix]
    wo_refs = refs[n_mix + 1:2 * n_mix + 1]
    g_ref, w13_ref, w2_ref = refs[2 * n_mix + 1:2 * n_mix + 4]
    pos = 2 * n_mix + 4
    gf_ref = refs[pos] if final_norm else None
    pos += 1 if final_norm else 0
    o_ref, x1_ref, h_ref, acc_ref = refs[pos:pos + 4]
    j = pl.program_id(1)

    @pl.when(j == 0)
    def _():
        x1 = res_ref[...]
        for m_ref, w_ref in zip(mix_refs, wo_refs):
            x1 = x1 + jnp.dot(m_ref[...].astype(BF16), w_ref[...], preferred_element_type=F32)
        x1_ref[...] = x1
        h_ref[...] = _rms(x1, g_ref[...]).astype(BF16)
        acc_ref[...] = jnp.zeros_like(acc_ref)

    gu = jnp.dot(h_ref[...], w13_ref[...], preferred_element_type=F32)
    act = _silu(gu[:, :FF_TILE]) * gu[:, FF_TILE:]
    acc_ref[...] += jnp.dot(act.astype(BF16), w2_ref[...], preferred_element_type=F32)

    @pl.when(j == pl.num_programs(1) - 1)
    def _():
        y = x1_ref[...] + acc_ref[...]
        if final_norm:
            y = _rms(y, gf_ref[...])
        o_ref[...] = y


def _out_ffn(mixes, res, wos, g_ffn, w13t, w2, g_final, tm):
    m, d = res.shape
    n_mix = len(mixes)
    nj = D_FF // FF_TILE
    final_norm = g_final is not None
    in_specs = [pl.BlockSpec((tm, mx.shape[1]), lambda i, j: (i, 0)) for mx in mixes]
    in_specs.append(pl.BlockSpec((tm, d), lambda i, j: (i, 0)))
    in_specs += [pl.BlockSpec(w.shape, lambda i, j: (0, 0)) for w in wos]
    in_specs += [pl.BlockSpec((1, d), lambda i, j: (0, 0)),
                 pl.BlockSpec((d, 2 * FF_TILE), lambda i, j: (0, j)),
                 pl.BlockSpec((FF_TILE, d), lambda i, j: (j, 0))]
    args = list(mixes) + [res] + list(wos) + [g_ffn.reshape(1, d), w13t, w2]
    if final_norm:
        in_specs.append(pl.BlockSpec((1, d), lambda i, j: (0, 0)))
        args.append(g_final.reshape(1, d))
    return pl.pallas_call(
        functools.partial(_out_ffn_kernel, n_mix=n_mix, final_norm=final_norm),
        grid=(m // tm, nj),
        in_specs=in_specs,
        out_specs=pl.BlockSpec((tm, d), lambda i, j: (i, 0)),
        out_shape=jax.ShapeDtypeStruct((m, d), F32),
        scratch_shapes=[pltpu.VMEM((tm, d), F32), pltpu.VMEM((tm, d), BF16), pltpu.VMEM((tm, d), F32)],
        compiler_params=_cparams(("parallel", "arbitrary")),
        name="out_ffn",
    )(*args)


def _gla_chunk(q, k, v, g, st):
    c = q.shape[0]
    tri = (_iota((c, c), 0) >= _iota((c, c), 1)).astype(F32)
    b = jnp.dot(tri, g, precision=lax.Precision.HIGHEST, preferred_element_type=F32)
    o = lax.dot_general((q * jnp.exp(b)).astype(BF16), st.astype(BF16), NT_DIMS, preferred_element_type=F32)
    lane = _iota((SUB, c), 1)
    row = _iota((SUB, c), 0)
    rows = []
    for blk in range(c // SUB):
        lo = blk * SUB
        b_i, q_i, k_i = b[lo:lo + SUB], q[lo:lo + SUB], k[lo:lo + SUB]
        a_blk = jnp.zeros((SUB, c), F32)
        for s in range(SUB):
            e = jnp.exp(jnp.minimum(b_i - b_i[s:s + 1], 0.0))
            a = jnp.sum(q_i * e * k_i[s:s + 1], axis=1, keepdims=True)
            a_blk = jnp.where(lane == lo + s, a, a_blk)
        a_blk = jnp.where(lane <= lo + row, a_blk, 0.0)
        if blk > 0:
            r = b[lo - 1:lo]
            qt = q_i * jnp.exp(b_i - r)
            kt = k * jnp.exp(jnp.minimum(r - b, 0.0))
            a_off = lax.dot_general(qt.astype(BF16), kt.astype(BF16), NT_DIMS, preferred_element_type=F32)
            a_blk = jnp.where(lane < lo, a_off, a_blk)
        rows.append(a_blk)
    a_full = jnp.concatenate(rows, axis=0)
    o = o + jnp.dot(a_full.astype(BF16), v.astype(BF16), preferred_element_type=F32)
    b_last = b[c - 1:c]
    kd = k * jnp.exp(b_last - b)
    st_new = st * jnp.exp(b_last) + lax.dot_general(v.astype(BF16), kd.astype(BF16), TN_DIMS,
                                                    preferred_element_type=F32)
    return o, st_new


def _lower_bound(lbl_ref, col):
    l = lbl_ref[:, col:col + HG_DK]
    e = jnp.exp(l - jnp.max(l, axis=0, keepdims=True))
    return e[0:1] / jnp.sum(e, axis=0, keepdims=True)


def _hgrn_kernel(q_ref, f_ref, i_ref, og_ref, lbl_ref, gain_ref, o_ref, s_ref, st_ref, *, nch):
    t = pl.program_id(1)

    @pl.when(t == 0)
    def _():
        st_ref[...] = jnp.zeros_like(st_ref)

    def body(ci, carry):
        r0 = pl.multiple_of(ci * CHUNK, CHUNK)
        rs = pl.ds(r0, CHUNK)
        for h in range(HG_H):
            cs = slice(h * HG_DK, (h + 1) * HG_DK)
            lb = _lower_bound(lbl_ref, h * HG_DK)
            q = _silu(q_ref[rs, cs])
            f = lb + (1.0 - lb) * _sigmoid(f_ref[rs, cs])
            o, st_new = _gla_chunk(q, 1.0 - f, i_ref[rs, cs], jnp.log(f), st_ref[h])
            st_ref[h] = st_new
            o = _rms(o, gain_ref[...]) * _sigmoid(og_ref[rs, cs])
            o_ref[rs, cs] = o.astype(o_ref.dtype)
        return carry

    lax.fori_loop(0, nch, body, 0)

    @pl.when(t == pl.num_programs(1) - 1)
    def _():
        for h in range(HG_H):
            s_ref[0, h] = st_ref[h].T


def _hgrn_prompt(p2d, lb_logits, gain, bsz, seq, tb):
    nt = seq // tb
    hk = HG_H * HG_DK
    row = lambda b, t: b * nt + t
    col_spec = lambda c: pl.BlockSpec((tb, hk), lambda b, t, c=c: (row(b, t), c))
    return pl.pallas_call(
        functools.partial(_hgrn_kernel, nch=tb // CHUNK),
        grid=(bsz, nt),
        in_specs=[col_spec(EV_Q // hk), col_spec(EV_F // hk), col_spec(EV_I // hk), col_spec(EV_OG // hk),
                  pl.BlockSpec(lb_logits.shape, lambda b, t: (0, 0)),
                  pl.BlockSpec((1, HG_DV), lambda b, t: (0, 0))],
        out_specs=[pl.BlockSpec((tb, hk), lambda b, t: (row(b, t), 0)),
                   pl.BlockSpec((1, HG_H, HG_DK, HG_DV), lambda b, t: (b, 0, 0, 0))],
        out_shape=[jax.ShapeDtypeStruct((bsz * seq, hk), BF16),
                   jax.ShapeDtypeStruct((bsz, HG_H, HG_DK, HG_DV), F32)],
        scratch_shapes=[pltpu.VMEM((HG_H, HG_DV, HG_DK), F32)],
        compiler_params=_cparams(("parallel", "arbitrary")),
        name="hgrn_chunk",
    )(p2d, p2d, p2d, p2d, lb_logits, gain.reshape(1, HG_DV))


def _log_sigmoid(x):
    return jnp.minimum(x, 0.0) - jnp.log(1.0 + jnp.exp(-jnp.abs(x)))


def _gla_kernel(q_ref, k_ref, v_ref, r_ref, a_ref, wa_ref, ba_ref, gain_ref, o_ref, s_ref, st_ref, *, nch):
    t = pl.program_id(1)

    @pl.when(t == 0)
    def _():
        st_ref[...] = jnp.zeros_like(st_ref)

    def body(ci, carry):
        r0 = pl.multiple_of(ci * CHUNK, CHUNK)
        rs = pl.ds(r0, CHUNK)
        gate = jnp.dot(a_ref[rs, :].astype(BF16), wa_ref[...], preferred_element_type=F32) + ba_ref[...]
        logf = _log_sigmoid(gate) * (1.0 / GLA_TAU)
        for h in range(GLA_H):
            ks = slice(h * GLA_DK, (h + 1) * GLA_DK)
            vs = slice(h * GLA_DV, (h + 1) * GLA_DV)
            q = q_ref[rs, ks] * (GLA_DK ** -0.5)
            o, st_new = _gla_chunk(q, k_ref[rs, ks], v_ref[rs, vs], logf[:, ks], st_ref[h])
            st_ref[h] = st_new
            o = _rms(o, gain_ref[...]) * _silu(r_ref[rs, vs])
            o_ref[rs, vs] = o.astype(o_ref.dtype)
        return carry

    lax.fori_loop(0, nch, body, 0)

    @pl.when(t == pl.num_programs(1) - 1)
    def _():
        for h in range(GLA_H):
            s_ref[0, h] = st_ref[h].T


def _gla_prompt(p2d, wa2p, ba, gain, bsz, seq, tb):
    nt = seq // tb
    hk, hv = GLA_H * GLA_DK, GLA_H * GLA_DV
    row = lambda b, t: b * nt + t
    return pl.pallas_call(
        functools.partial(_gla_kernel, nch=tb // CHUNK),
        grid=(bsz, nt),
        in_specs=[pl.BlockSpec((tb, hk), lambda b, t: (row(b, t), OD_Q // hk)),
                  pl.BlockSpec((tb, hk), lambda b, t: (row(b, t), OD_K // hk)),
                  pl.BlockSpec((tb, hv), lambda b, t: (row(b, t), OD_V // hv)),
                  pl.BlockSpec((tb, hv), lambda b, t: (row(b, t), OD_R // hv)),
                  pl.BlockSpec((tb, LANES), lambda b, t: (row(b, t), OD_A // LANES)),
                  pl.BlockSpec(wa2p.shape, lambda b, t: (0, 0)),
                  pl.BlockSpec((1, hk), lambda b, t: (0, 0)),
                  pl.BlockSpec((1, GLA_DV), lambda b, t: (0, 0))],
        out_specs=[pl.BlockSpec((tb, hv), lambda b, t: (row(b, t), 0)),
                   pl.BlockSpec((1, GLA_H, GLA_DK, GLA_DV), lambda b, t: (b, 0, 0, 0))],
        out_shape=[jax.ShapeDtypeStruct((bsz * seq, hv), BF16),
                   jax.ShapeDtypeStruct((bsz, GLA_H, GLA_DK, GLA_DV), F32)],
        scratch_shapes=[pltpu.VMEM((GLA_H, GLA_DV, GLA_DK), F32)],
        compiler_params=_cparams(("parallel", "arbitrary")),
        name="gla_chunk",
    )(p2d, p2d, p2d, p2d, p2d, wa2p, ba.reshape(1, hk), gain.reshape(1, GLA_DV))


def _to_columns(x):
    bsz = x.shape[0]
    if bsz < LANES:
        x = jnp.concatenate([x, jnp.zeros((LANES - bsz, x.shape[1]), x.dtype)], axis=0)
    return x.T


def _decode_update(q, k, v, g, s_ref, so_ref, o_scr):
    bsz = q.shape[0]
    qt, kt, et = _to_columns(q), _to_columns(k), _to_columns(jnp.exp(g))
    for b in range(bsz):
        s_new = et[:, b:b + 1] * s_ref[b, 0] + kt[:, b:b + 1] * v[b:b + 1, :]
        so_ref[b, 0] = s_new
        o_scr[b:b + 1, :] = jnp.sum(qt[:, b:b + 1] * s_new, axis=0, keepdims=True)


def _hgrn_decode_kernel(q_ref, f_ref, i_ref, og_ref, lbl_ref, gain_ref, s_ref, o_ref, so_ref, o_scr):
    l = lbl_ref[0]
    e = jnp.exp(l - jnp.max(l, axis=0, keepdims=True))
    lb = e[0:1] / jnp.sum(e, axis=0, keepdims=True)
    f = lb + (1.0 - lb) * _sigmoid(f_ref[...])
    _decode_update(_silu(q_ref[...]), 1.0 - f, i_ref[...], jnp.log(f), s_ref, so_ref, o_scr)
    o_ref[...] = _rms(o_scr[...], gain_ref[...]) * _sigmoid(og_ref[...])


def _hgrn_decode(p2d, lb_logits, gain, state):
    bsz = p2d.shape[0]
    col = lambda c: pl.BlockSpec((bsz, HG_DK), lambda h, c=c: (0, c + h))
    lbl3 = lb_logits.reshape(lb_logits.shape[0], HG_H, HG_DK).transpose(1, 0, 2)
    st_spec = pl.BlockSpec((bsz, 1, HG_DK, HG_DV), lambda h: (0, h, 0, 0))
    return pl.pallas_call(
        _hgrn_decode_kernel,
        grid=(HG_H,),
        in_specs=[col(EV_Q // HG_DK), col(EV_F // HG_DK), col(EV_I // HG_DK), col(EV_OG // HG_DK),
                  pl.BlockSpec((1,) + lbl3.shape[1:], lambda h: (h, 0, 0)),
                  pl.BlockSpec((1, HG_DV), lambda h: (0, 0)),
                  st_spec],
        out_specs=[pl.BlockSpec((bsz, HG_DV), lambda h: (0, h)), st_spec],
        out_shape=[jax.ShapeDtypeStruct((bsz, HG_H * HG_DV), F32),
                   jax.ShapeDtypeStruct(state.shape, F32)],
        scratch_shapes=[pltpu.VMEM((bsz, HG_DV), F32)],
        compiler_params=_cparams(("arbitrary",)),
        name="hgrn_decode",
    )(p2d, p2d, p2d, p2d, lbl3, gain.reshape(1, HG_DV), state)


def _gla_decode_kernel(q_ref, k_ref, v_ref, r_ref, a_ref, wa_ref, ba_ref, gain_ref, s_ref, o_ref, so_ref, o_scr):
    gate = jnp.dot(a_ref[...].astype(BF16), wa_ref[...], preferred_element_type=F32) + ba_ref[...]
    logf = _log_sigmoid(gate) * (1.0 / GLA_TAU)
    _decode_update(q_ref[...] * (GLA_DK ** -0.5), k_ref[...], v_ref[...], logf, s_ref, so_ref, o_scr)
    o_ref[...] = _rms(o_scr[...], gain_ref[...]) * _silu(r_ref[...])


def _gla_decode(p2d, wa2p, ba, gain, state):
    bsz = p2d.shape[0]
    hk = GLA_H * GLA_DK
    st_spec = pl.BlockSpec((bsz, 1, GLA_DK, GLA_DV), lambda h: (0, h, 0, 0))
    return pl.pallas_call(
        _gla_decode_kernel,
        grid=(GLA_H,),
        in_specs=[pl.BlockSpec((bsz, GLA_DK), lambda h: (0, OD_Q // GLA_DK + h)),
                  pl.BlockSpec((bsz, GLA_DK), lambda h: (0, OD_K // GLA_DK + h)),
                  pl.BlockSpec((bsz, GLA_DV), lambda h: (0, OD_V // GLA_DV + h)),
                  pl.BlockSpec((bsz, GLA_DV), lambda h: (0, OD_R // GLA_DV + h)),
                  pl.BlockSpec((bsz, LANES), lambda h: (0, OD_A // LANES)),
                  pl.BlockSpec((LANES, GLA_DK), lambda h: (0, h)),
                  pl.BlockSpec((1, GLA_DK), lambda h: (0, h)),
                  pl.BlockSpec((1, GLA_DV), lambda h: (0, 0)),
                  st_spec],
        out_specs=[pl.BlockSpec((bsz, GLA_DV), lambda h: (0, h)), st_spec],
        out_shape=[jax.ShapeDtypeStruct((bsz, GLA_H * GLA_DV), F32),
                   jax.ShapeDtypeStruct(state.shape, F32)],
        scratch_shapes=[pltpu.VMEM((bsz, GLA_DV), F32)],
        compiler_params=_cparams(("arbitrary",)),
        name="gla_decode",
    )(p2d, p2d, p2d, p2d, p2d, wa2p, ba.reshape(1, hk), gain.reshape(1, GLA_DV), state)


def _compress_kernel(xa_ref, xb_ref, pe_ref, w1_ref, w2_ref, o_ref, acc_ref):
    nb = o_ref.shape[0]
    acc_ref[...] = jnp.zeros_like(acc_ref)
    for l in range(NSA_BLK):
        rows = pl.ds(l, nb, stride=NSA_BLK)
        x = jnp.concatenate([xa_ref[rows, :], xb_ref[rows, :]], axis=1) + pe_ref[l:l + 1, :]
        acc_ref[...] += jnp.dot(x.astype(BF16), w1_ref[l], preferred_element_type=F32)
    o_ref[...] = jnp.dot(_silu(acc_ref[...]).astype(BF16), w2_ref[...], preferred_element_type=F32)


def _compress_prompt(p2d, pe4, w1bd, w2bd):
    n = p2d.shape[0]
    rc = min(8192, n)
    cw = 4 * NSA_DH
    c0 = PM_CMP // LANES
    return pl.pallas_call(
        _compress_kernel,
        grid=(n // rc,),
        in_specs=[pl.BlockSpec((rc, LANES), lambda i: (i, c0)),
                  pl.BlockSpec((rc, LANES), lambda i: (i, c0 + 1)),
                  pl.BlockSpec(pe4.shape, lambda i: (0, 0)),
                  pl.BlockSpec(w1bd.shape, lambda i: (0, 0, 0)),
                  pl.BlockSpec((cw, cw), lambda i: (0, 0))],
        out_specs=pl.BlockSpec((rc // NSA_BLK, cw), lambda i: (i, 0)),
        out_shape=jax.ShapeDtypeStruct((n // NSA_BLK, cw), F32),
        scratch_shapes=[pltpu.VMEM((rc // NSA_BLK, cw), F32)],
        compiler_params=_cparams(("arbitrary",)),
        name="nsa_compress",
    )(p2d, p2d, pe4, w1bd, w2bd)


AQB = 256
AQB_SHIFT = 8
AUG = 128
ONES_ROWS = 16


def _slopes_lane(g, lanes):
    r = _iota((1, lanes), 1) >> AQB_SHIFT
    out = jnp.zeros((1, lanes), F32)
    for rr in range(NSA_R):
        out = jnp.where(r == rr, 2.0 ** (-(g * NSA_R + rr + 1)), out)
    return out


def _build_key_features(src_ref, row0, dst_ref, g, seq, with_onehot):
    r = _iota((NSA_DH, LANES), 0)
    for cb in range(seq // LANES):
        kpos = cb * LANES + _iota((NSA_DH, LANES), 1)
        feat = jnp.where(r == 32, (kpos >> 6).astype(F32),
                         jnp.where(r == 33, (kpos & 63).astype(F32),
                                   jnp.where((r == 34) | (r == 35), 1.0, 0.0)))
        if with_onehot:
            feat = jnp.where((kpos >> 6) == r, 1.0, feat)
        kt = src_ref[0, row0:row0 + NSA_DH, cb * LANES:(cb + 1) * LANES]
        dst_ref[g, cb * LANES:(cb + 1) * LANES, :] = jnp.concatenate([kt, feat], axis=0).T.astype(BF16)


def _attn_tile(k_ref, g, k0, v_ref, v_row0, qa_ref, mask, m_ref, acc_ref, stream):
    s = jnp.dot(k_ref[g, pl.ds(k0, AQB), :], qa_ref[g], preferred_element_type=F32)
    if mask is not None:
        s = jnp.where(mask, s, NEG)
    m_old = m_ref[stream]
    m_new = jnp.maximum(m_old, jnp.max(s, axis=0, keepdims=True))
    p = jnp.exp(s - m_new).astype(BF16)
    vt = v_ref[0, v_row0:v_row0 + NSA_DH, pl.ds(k0, AQB)].astype(BF16)
    vt = jnp.concatenate([vt, jnp.ones((ONES_ROWS, AQB), BF16)], axis=0)
    acc_ref[stream] = jnp.exp(m_old - m_new) * acc_ref[stream] + jnp.dot(vt, p, preferred_element_type=F32)
    m_ref[stream] = m_new


def _nsa_attn_kernel(qt_ref, rows_ref, win_ref, gt_ref, kc_ref, vct_ref, o_ref, ks_s, kw_s, qa_s, m_s, acc_s,
                     *, nblk):
    qb = pl.program_id(1)
    seq = rows_ref.shape[2]
    q0 = qb * AQB
    nq = NSA_R * AQB
    half = NSA_G * NSA_DH

    @pl.when(qb == 0)
    def _():
        for g in range(NSA_G):
            _build_key_features(rows_ref, 2 * half + g * NSA_DH, ks_s, g, seq, True)
            _build_key_features(win_ref, g * NSA_DH, kw_s, g, seq, False)

    pq = q0 + (_iota((1, nq), 1) & (AQB - 1))
    gates = _sigmoid(gt_ref[0])
    ksel = min(NSA_TOPK, nblk)
    scale = NSA_DH ** -0.5
    key_i = _iota((AQB, nq), 0)
    qry_t = _iota((AQB, nq), 1) & (AQB - 1)
    causal = key_i <= qry_t
    beyond = key_i > qry_t
    pq_f = q0 + _iota((32, AQB), 1)
    feat_row = _iota((32, AQB), 0)

    m_s[...] = jnp.full(m_s.shape, NEG, F32)
    acc_s[...] = jnp.zeros_like(acc_s)

    def result(i):
        return acc_s[i, 0:NSA_DH, :] / jnp.maximum(acc_s[i, NSA_DH:NSA_DH + 1, :], TINY)

    o_cmp = []
    for g in range(NSA_G):
        slope = _slopes_lane(g, nq)

        qt_g = [qt_ref[0, (g * NSA_R + r) * NSA_DH:(g * NSA_R + r + 1) * NSA_DH, :] * scale for r in range(NSA_R)]
        qs_t = jnp.concatenate(qt_g, axis=1).astype(BF16)
        sc = jnp.dot(kc_ref[0, g], qs_t, preferred_element_type=F32)
        dist_c = pq - (_iota((nblk, nq), 0) * NSA_BLK + NSA_BLK - 1)
        mask_c = dist_c >= 0
        sc = jnp.where(mask_c, sc - slope * dist_c.astype(F32), NEG)
        pc = jnp.where(mask_c, jnp.exp(sc - jnp.max(sc, axis=0, keepdims=True)), 0.0)
        pc = pc / jnp.maximum(jnp.sum(pc, axis=0, keepdims=True), TINY)
        o_cmp.append(jnp.dot(vct_ref[0, g], pc.astype(BF16), preferred_element_type=F32))
        imp = pc[:, 0:AQB]
        for r in range(1, NSA_R):
            imp = imp + pc[:, r * AQB:(r + 1) * AQB]

        n_io = _iota((nblk, AQB), 0)
        tpos = q0 + _iota((nblk, AQB), 1)
        forced = (n_io == (tpos >> 6)) | (n_io == 0)
        started = n_io * NSA_BLK <= tpos
        score = jnp.where(forced, jnp.inf, jnp.where(started, imp, -jnp.inf))
        rank = jnp.zeros((nblk, AQB), I32)
        for i in range(nblk):
            row = score[i:i + 1, :]
            rank = rank + ((row > score) | ((row == score) & (i < n_io))).astype(I32)
        sel_bias = jnp.where((rank < ksel) & started, 0.0, NEG)
        if nblk < 32:
            sel_bias = jnp.concatenate([sel_bias, jnp.zeros((32 - nblk, AQB), F32)], axis=0)

        cols = []
        for r in range(NSA_R):
            sl = 2.0 ** (-(g * NSA_R + r + 1))
            pos_feat = jnp.where(feat_row == 0, 64.0 * sl,
                                 jnp.where(feat_row == 1, sl,
                                           jnp.where(feat_row == 2, (-64.0 * sl) * (pq_f >> 6).astype(F32),
                                                     jnp.where(feat_row == 3, (-sl) * (pq_f & 63).astype(F32), 0.0))))
            cols.append(jnp.concatenate([qt_g[r], sel_bias, pos_feat], axis=0))
        qa_s[g] = jnp.concatenate(cols, axis=1).astype(BF16)

    def slc_tile(k0, mask):
        for g in range(NSA_G):
            _attn_tile(ks_s, g, k0, rows_ref, 3 * half + g * NSA_DH, qa_s, mask, m_s, acc_s, g)

    def win_tile(k0, mask):
        for g in range(NSA_G):
            _attn_tile(kw_s, g, k0, win_ref, half + g * NSA_DH, qa_s, mask, m_s, acc_s, NSA_G + g)

    def slc_body(kt, c):
        slc_tile(pl.multiple_of(kt * AQB, AQB), None)
        return c

    lax.fori_loop(0, qb, slc_body, 0)

    @pl.when(qb >= 2)
    def _():
        win_tile(pl.multiple_of(q0 - 2 * AQB, AQB), beyond)

    @pl.when(qb >= 1)
    def _():
        win_tile(pl.multiple_of(q0 - AQB, AQB), None)

    slc_tile(pl.multiple_of(q0, AQB), causal)
    win_tile(pl.multiple_of(q0, AQB), causal)

    for g in range(NSA_G):
        def gate_row(j, g=g):
            return jnp.concatenate([gates[g * 12 + r * 3 + j:g * 12 + r * 3 + j + 1, :] for r in range(NSA_R)],
                                   axis=1)

        o_t = gate_row(0) * o_cmp[g] + gate_row(1) * result(g) + gate_row(2) * result(NSA_G + g)
        o_st = jnp.concatenate([o_t[:, r * AQB:(r + 1) * AQB] for r in range(NSA_R)], axis=0)
        o_ref[:, g * NSA_R * NSA_DH:(g + 1) * NSA_R * NSA_DH] = o_st.T.astype(o_ref.dtype)


def _nsa_attn_prompt(qt, rows_t, win_t, gate_t, kc, vct, bsz, seq):
    assert NSA_WINDOW == 2 * AQB and seq % AQB == 0 and seq // NSA_BLK <= 32
    nqb = seq // AQB
    nblk = seq // NSA_BLK
    qw = NSA_H * NSA_DH
    nq = NSA_R * AQB
    per_b = lambda a: pl.BlockSpec((1,) + a.shape[1:], lambda b, i: (b,) + (0,) * (a.ndim - 1))
    per_q = lambda a: pl.BlockSpec((1, a.shape[1], AQB), lambda b, i: (b, 0, i))
    return pl.pallas_call(
        functools.partial(_nsa_attn_kernel, nblk=nblk),
        grid=(bsz, nqb),
        in_specs=[per_q(qt), per_b(rows_t), per_b(win_t), per_q(gate_t), per_b(kc), per_b(vct)],
        out_specs=pl.BlockSpec((AQB, qw), lambda b, i: (b * nqb + i, 0)),
        out_shape=jax.ShapeDtypeStruct((bsz * seq, qw), BF16),
        scratch_shapes=[pltpu.VMEM((NSA_G, seq, AUG), BF16), pltpu.VMEM((NSA_G, seq, AUG), BF16),
                        pltpu.VMEM((NSA_G, AUG, nq), BF16),
                        pltpu.VMEM((2 * NSA_G, 1, nq), F32), pltpu.VMEM((2 * NSA_G, NSA_DH + ONES_ROWS, nq), F32)],
        compiler_params=_cparams(("parallel", "arbitrary")),
        name="nsa_attn",
    )(qt, rows_t, win_t, gate_t, kc, vct)


QROWS = 16


def _head_slopes(rows):
    r = _iota((rows, 1), 0)
    out = jnp.zeros((rows, 1), F32)
    for h in range(NSA_H):
        out = jnp.where(r == h, 2.0 ** (-(h + 1)), out)
    return out


def _softmax_rows(s, valid):
    s = jnp.where(valid, s, NEG)
    p = jnp.where(valid, jnp.exp(s - jnp.max(s, axis=1, keepdims=True)), 0.0)
    return p / jnp.maximum(jnp.sum(p, axis=1, keepdims=True), TINY)


def _padded_queries(q_ref):
    q = q_ref[0] * (NSA_DH ** -0.5)
    return jnp.concatenate([q, jnp.zeros((QROWS - NSA_H, NSA_DH), F32)], axis=0).astype(BF16)


def _dec_cmp_kernel(pt_ref, cache_ref, pe_ref, w1_ref, w2_ref, q_ref, o_ref, sel_ref, buf, acc_ref, sem,
                    *, npages, nsel):
    b = pl.program_id(0)
    nb = npages * (PAGE // NSA_BLK)
    cw = 4 * NSA_DH

    def page_copy(pg, hf):
        return pltpu.make_async_copy(cache_ref.at[pt_ref[b * npages + pg], :, pl.ds(hf * LANES, LANES)],
                                     buf.at[hf, pl.ds(pg * PAGE, PAGE), :], sem.at[0])

    def start(pg, c):
        page_copy(pg, 0).start()
        page_copy(pg, 1).start()
        return c

    def wait(pg, c):
        page_copy(pg, 0).wait()
        page_copy(pg, 1).wait()
        return c

    lax.fori_loop(0, npages, start, 0)
    lax.fori_loop(0, npages, wait, 0)

    acc_ref[...] = jnp.zeros_like(acc_ref)
    for l in range(NSA_BLK):
        rows = pl.ds(l, nb, stride=NSA_BLK)
        x = jnp.concatenate([buf[0, rows, :], buf[1, rows, :]], axis=1) + pe_ref[l:l + 1, :]
        acc_ref[...] += jnp.dot(x.astype(BF16), w1_ref[l], preferred_element_type=F32)
    cmp = jnp.dot(_silu(acc_ref[...]).astype(BF16), w2_ref[...], preferred_element_type=F32)

    pos = nb * NSA_BLK
    q16 = _padded_queries(q_ref)
    slope = _head_slopes(QROWS)
    row_grp = _iota((QROWS, 1), 0) >> 2
    dist = pos - (_iota((QROWS, nb), 1) * NSA_BLK + NSA_BLK - 1)
    ri, ci = _iota((nb, nb), 0), _iota((nb, nb), 1)
    o_all = jnp.zeros((QROWS, NSA_DH), F32)
    for g in range(NSA_G):
        kc = cmp[:, g * NSA_DH:(g + 1) * NSA_DH].astype(BF16)
        vc = cmp[:, (NSA_G + g) * NSA_DH:(NSA_G + g + 1) * NSA_DH].astype(BF16)
        s = lax.dot_general(q16, kc, NT_DIMS, preferred_element_type=F32)
        p = _softmax_rows(s - slope * dist.astype(F32), dist >= 0)
        o_all = jnp.where(row_grp == g, jnp.dot(p.astype(BF16), vc, preferred_element_type=F32), o_all)
        imp = jnp.sum(jnp.where(row_grp == g, p, 0.0), axis=0, keepdims=True)
        score_row = jnp.where(_iota((1, nb), 1) == 0, jnp.inf, imp)
        score_col = jnp.sum(jnp.where(ri == ci, jnp.broadcast_to(score_row, (nb, nb)), 0.0),
                            axis=1, keepdims=True)
        beats = (score_col > score_row) | ((score_col == score_row) & (ri < ci))
        rank = jnp.sum(beats.astype(I32), axis=0, keepdims=True)
        sel_ref[0, g:g + 1, :] = (rank < nsel).astype(F32)
    o_ref[0] = o_all[:NSA_H]


def _dec_cmp(page_table, cache3, pe4, w1bd, w2bd, q3, nsel):
    bsz, npages = page_table.shape
    nb = npages * (PAGE // NSA_BLK)
    cw = 4 * NSA_DH
    grid_spec = pltpu.PrefetchScalarGridSpec(
        num_scalar_prefetch=1,
        grid=(bsz,),
        in_specs=[pl.BlockSpec(memory_space=pl.ANY),
                  pl.BlockSpec(pe4.shape, lambda b, pt: (0, 0)),
                  pl.BlockSpec(w1bd.shape, lambda b, pt: (0, 0, 0)),
                  pl.BlockSpec(w2bd.shape, lambda b, pt: (0, 0)),
                  pl.BlockSpec((1, NSA_H, NSA_DH), lambda b, pt: (b, 0, 0))],
        out_specs=[pl.BlockSpec((1, NSA_H, NSA_DH), lambda b, pt: (b, 0, 0)),
                   pl.BlockSpec((1, NSA_G, nb), lambda b, pt: (b, 0, 0))],
        scratch_shapes=[pltpu.VMEM((cw // LANES, npages * PAGE, LANES), F32), pltpu.VMEM((nb, cw), F32),
                        pltpu.SemaphoreType.DMA((1,))])
    return pl.pallas_call(
        functools.partial(_dec_cmp_kernel, npages=npages, nsel=nsel),
        grid_spec=grid_spec,
        out_shape=[jax.ShapeDtypeStruct((bsz, NSA_H, NSA_DH), F32),
                   jax.ShapeDtypeStruct((bsz, NSA_G, nb), F32)],
        compiler_params=_cparams(("arbitrary",)),
        name="nsa_dec_cmp",
    )(page_table.reshape(-1), cache3, pe4, w1bd, w2bd, q3)


def _dec_attn_kernel(pt_ref, idx_ref, cache_ref, q_ref, oc_ref, kn_ref, win_ref, gt_ref, o_ref, sbuf, wbuf, sem,
                     *, npages, nsel, wlen):
    b = pl.program_id(0)
    nb = npages * (PAGE // NSA_BLK)
    pos = nb * NSA_BLK
    nk = nsel * NSA_BLK
    nkp = sbuf.shape[1]
    half = 2 * NSA_G * NSA_DH

    copies = []
    for g in range(NSA_G):
        for j in range(nsel):
            blk = idx_ref[(b * NSA_G + g) * NSA_TOPK + j]
            page = pt_ref[b * npages + (blk >> 1)]
            copies.append(pltpu.make_async_copy(
                cache_ref.at[page, pl.ds((blk & 1) * NSA_BLK, NSA_BLK), pl.ds(half, half)],
                sbuf.at[g, pl.ds(j * NSA_BLK, NSA_BLK), :], sem.at[0]))
    for cp in copies:
        cp.start()

    kn = kn_ref[0]
    tail = _iota((nkp - nk, half), 0) == 0
    for g in range(NSA_G):
        sbuf[g, nk:nkp, :] = jnp.where(tail, kn[0:1, :], 0.0)
    wbuf[0:wlen, :] = win_ref[0]
    wbuf[wlen:wlen + LANES, :] = jnp.where(_iota((LANES, half), 0) == 0, kn[1:2, :], 0.0)
    for cp in copies:
        cp.wait()

    q16 = _padded_queries(q_ref)
    slope = _head_slopes(QROWS)
    row_grp = _iota((QROWS, 1), 0) >> 2
    lane = _iota((1, nkp), 1)
    wdist = wlen - _iota((1, wlen + LANES), 1)
    o_slc = jnp.zeros((QROWS, NSA_DH), F32)
    o_win = jnp.zeros((QROWS, NSA_DH), F32)
    for g in range(NSA_G):
        ksl = slice(g * NSA_DH, (g + 1) * NSA_DH)
        vsl = slice((NSA_G + g) * NSA_DH, (NSA_G + g + 1) * NSA_DH)
        blk_lane = jnp.full((1, nkp), nb, I32)
        for j in range(nsel):
            blk_lane = jnp.where((lane >> 6) == j, idx_ref[(b * NSA_G + g) * NSA_TOPK + j], blk_lane)
        kpos = blk_lane * NSA_BLK + (lane & (NSA_BLK - 1))
        slab = sbuf[g]
        s = lax.dot_general(q16, slab[:, ksl].astype(BF16), NT_DIMS, preferred_element_type=F32)
        p = _softmax_rows(s - slope * (pos - kpos).astype(F32), lane <= nk)
        o_slc = jnp.where(row_grp == g, jnp.dot(p.astype(BF16), slab[:, vsl].astype(BF16),
                                                preferred_element_type=F32), o_slc)
        w = wbuf[...]
        s = lax.dot_general(q16, w[:, ksl].astype(BF16), NT_DIMS, preferred_element_type=F32)
        p = _softmax_rows(s - slope * wdist.astype(F32), (wdist >= 0) & (wdist < NSA_WINDOW))
        o_win = jnp.where(row_grp == g, jnp.dot(p.astype(BF16), w[:, vsl].astype(BF16),
                                                preferred_element_type=F32), o_win)

    gates = _sigmoid(gt_ref[pl.ds(b, 1), :])
    r_io = _iota((QROWS, LANES), 0)
    l_io = _iota((QROWS, LANES), 1)

    def gate_col(j):
        return jnp.sum(jnp.where(l_io == 3 * r_io + j, gates, 0.0), axis=1, keepdims=True)

    o_cmp = jnp.concatenate([oc_ref[0], jnp.zeros((QROWS - NSA_H, NSA_DH), F32)], axis=0)
    o = gate_col(0) * o_cmp + gate_col(1) * o_slc + gate_col(2) * o_win
    o_ref[0] = o[:NSA_H]


def _dec_attn(page_table, idx, cache3, q3, ocmp, knew, win3, gates, nsel):
    bsz, npages = page_table.shape
    wlen = win3.shape[1]
    nkp = (nsel + 1) * NSA_BLK
    half = 2 * NSA_G * NSA_DH
    hd = pl.BlockSpec((1, NSA_H, NSA_DH), lambda b, pt, ix: (b, 0, 0))
    grid_spec = pltpu.PrefetchScalarGridSpec(
        num_scalar_prefetch=2,
        grid=(bsz,),
        in_specs=[pl.BlockSpec(memory_space=pl.ANY), hd, hd,
                  pl.BlockSpec((1,) + knew.shape[1:], lambda b, pt, ix: (b, 0, 0)),
                  pl.BlockSpec((1,) + win3.shape[1:], lambda b, pt, ix: (b, 0, 0)),
                  pl.BlockSpec(gates.shape, lambda b, pt, ix: (0, 0))],
        out_specs=hd,
        scratch_shapes=[pltpu.VMEM((NSA_G, nkp, half), F32), pltpu.VMEM((wlen + LANES, half), F32),
                        pltpu.SemaphoreType.DMA((1,))])
    return pl.pallas_call(
        functools.partial(_dec_attn_kernel, npages=npages, nsel=nsel, wlen=wlen),
        grid_spec=grid_spec,
        out_shape=jax.ShapeDtypeStruct((bsz, NSA_H, NSA_DH), F32),
        compiler_params=_cparams(("arbitrary",)),
        name="nsa_dec_attn",
    )(page_table.reshape(-1), idx.reshape(-1), cache3, q3, ocmp, knew, win3, gates)


def _prep_weights(w_in_even, nsa_cmp_pos, nsa_cmp_w1, nsa_cmp_w2, w_out_even, w_in_odd, gla_wa2, w_out_odd,
                  ffn_w13, ffn_w2):
    d = D_MODEL
    wie = jnp.pad(w_in_even[0], ((0, 0), (0, PROJ_W - w_in_even.shape[2]))).astype(BF16)
    a0 = 2 * GLA_H * GLA_DK + GLA_H * GLA_DV
    wo = w_in_odd[0]
    wio = jnp.concatenate([wo[:, :a0], wo[:, a0 + GLA_RANK:], wo[:, a0:a0 + GLA_RANK]], axis=1)
    wio = jnp.pad(wio, ((0, 0), (0, PROJ_W - wio.shape[1]))).astype(BF16)
    wa2p = jnp.pad(gla_wa2[0], ((0, LANES - GLA_RANK), (0, 0))).astype(BF16)
    hv = HG_H * HG_DV
    woe_h, woe_n = w_out_even[0, :hv].astype(BF16), w_out_even[0, hv:].astype(BF16)
    woo = w_out_odd[0].astype(BF16)
    nj = D_FF // FF_TILE
    w13t, w2b = [], []
    for l in range(ffn_w13.shape[0]):
        gate = ffn_w13[l, :, :D_FF].reshape(d, nj, FF_TILE)
        up = ffn_w13[l, :, D_FF:].reshape(d, nj, FF_TILE)
        w13t.append(jnp.concatenate([gate, up], axis=2).reshape(d, 2 * D_FF).astype(BF16))
        w2b.append(ffn_w2[l].astype(BF16))
    w1 = nsa_cmp_w1[0].reshape(2, NSA_BLK, NSA_DH, NSA_DH)
    w2 = nsa_cmp_w2[0]
    cw = 4 * NSA_DH
    eye = jnp.eye(2 * NSA_G, dtype=F32).reshape(2, NSA_G, 2, NSA_G)
    w1bd = jnp.einsum("cldj,cgCG->lcgdCGj", w1, eye).reshape(NSA_BLK, cw, cw)
    w2bd = jnp.einsum("cjd,cgCG->cgjCGd", w2, eye).reshape(cw, cw)
    pe = nsa_cmp_pos[0]
    pe4 = jnp.concatenate([pe[0], pe[0], pe[1], pe[1]], axis=1)
    we = w_in_even[0]
    hgw = 2 * HG_H * HG_DK + 2 * HG_H * HG_DV
    nq0 = hgw + NSA_H * NSA_DH
    w_main = jnp.concatenate([we[:, :hgw], we[:, nq0:nq0 + cw]], axis=1).astype(BF16)
    w_tr = jnp.pad(we[:, hgw:].T, ((0, T_END - (we.shape[1] - hgw)), (0, 0))).astype(BF16)
    return dict(wie=wie, wio=wio, wa2p=wa2p, woe_h=woe_h, woe_n=woe_n, woo=woo, w13t=w13t, w2b=w2b,
                w1bd=w1bd.astype(BF16), w2bd=w2bd.astype(BF16), pe4=pe4, w_main=w_main, w_tr=w_tr)


def _forward_prompt(x, pw, hg_lb_logits, hg_norm, gla_ba, gla_norm, norm_mix, norm_ffn, norm_final):
    bsz, seq, d = x.shape
    n = bsz * seq
    x2d = x.reshape(n, d)
    tm = 512 if seq % 512 == 0 else 256
    tmf = next(t for t in (1024, 512, 256) if n % t == 0)
    tb = tm

    p, qt, rows_t, win_t, gate_t = _norm_proj_t(x2d, norm_mix[0], pw["w_main"], pw["w_tr"], bsz, seq, tm)
    o_h, hg_state = _hgrn_prompt(p, hg_lb_logits, hg_norm[0], bsz, seq, tb)
    cmp = _compress_prompt(p, pw["pe4"], pw["w1bd"], pw["w2bd"])
    nblk = seq // NSA_BLK
    cmp4 = cmp.reshape(bsz, nblk, 2 * NSA_G, NSA_DH).transpose(0, 2, 1, 3)
    kc = cmp4[:, :NSA_G].astype(BF16)
    vct = cmp4[:, NSA_G:].transpose(0, 1, 3, 2).astype(BF16)
    o_n = _nsa_attn_prompt(qt, rows_t, win_t, gate_t, kc, vct, bsz, seq)
    x1 = _out_ffn([o_h, o_n], x2d, [pw["woe_h"], pw["woe_n"]], norm_ffn[0], pw["w13t"][0], pw["w2b"][0], None, tmf)

    rows = rows_t.reshape(bsz, 4, NSA_G, NSA_DH, seq).transpose(0, 4, 1, 2, 3)[None]
    wn = min(NSA_WINDOW, seq)
    win = win_t[:, :, seq - wn:].reshape(bsz, 2, NSA_G, NSA_DH, wn).transpose(0, 4, 1, 2, 3)[None]

    p2 = _norm_proj(x1, norm_mix[1], pw["wio"], tm, PROJ_W // 2)
    o_g, gla_state = _gla_prompt(p2, pw["wa2p"], gla_ba[0], gla_norm[0], bsz, seq, tb)
    y = _out_ffn([o_g], x1, [pw["woo"]], norm_ffn[1], pw["w13t"][1], pw["w2b"][1], norm_final, tmf)
    return y.reshape(bsz, seq, d), rows, win, hg_state[None], gla_state[None]


def _forward_sample(x, pw, cache_nsa_kv, cache_win_kv, state_hgrn, state_gla, page_table, hg_lb_logits, hg_norm,
                    gla_ba, gla_norm, norm_mix, norm_ffn, norm_final):
    bsz, seq, d = x.shape
    x2d = x.reshape(bsz, d)
    npages = page_table.shape[1]
    nblk_total = npages * (PAGE // NSA_BLK) + 1
    nsel = min(NSA_TOPK, nblk_total) - 1

    p = _norm_proj(x2d, norm_mix[0], pw["wie"], bsz, PROJ_W // 2)
    o_h, hg_state = _hgrn_decode(p, hg_lb_logits, hg_norm[0], state_hgrn[0])

    cache3 = cache_nsa_kv[0].reshape(cache_nsa_kv.shape[1], PAGE, 4 * NSA_G * NSA_DH)
    q3 = p[:, EV_NQ:EV_CMP].reshape(bsz, NSA_H, NSA_DH)
    ocmp, sel = _dec_cmp(page_table, cache3, pw["pe4"], pw["w1bd"], pw["w2bd"], q3, nsel)
    idx = jnp.argsort(-sel, axis=-1, stable=True)[..., :NSA_TOPK].astype(I32)
    knew = p[:, EV_SLC:EV_GATE].reshape(bsz, 2, 2 * NSA_G * NSA_DH)
    win3 = cache_win_kv[0].reshape(bsz, cache_win_kv.shape[2], 2 * NSA_G * NSA_DH)
    o_n = _dec_attn(page_table, idx, cache3, q3, ocmp, knew, win3, p[:, EV_GATE:EV_GATE + LANES], nsel)
    o_n = o_n.reshape(bsz, NSA_H * NSA_DH)
    x1 = _out_ffn([o_h, o_n], x2d, [pw["woe_h"], pw["woe_n"]], norm_ffn[0], pw["w13t"][0], pw["w2b"][0], None, bsz)

    rows = p[:, EV_CMP:EV_WIN].reshape(1, bsz, 1, 4, NSA_G, NSA_DH)
    win_new = p[:, EV_WIN:EV_GATE].reshape(1, bsz, 1, 2, NSA_G, NSA_DH)
    wk = jnp.concatenate([cache_win_kv[:1], win_new], axis=2)
    wn = min(NSA_WINDOW, npages * PAGE + 1)
    win = wk[:, :, wk.shape[2] - wn:]

    p2 = _norm_proj(x1, norm_mix[1], pw["wio"], bsz, PROJ_W // 2)
    o_g, gla_state = _gla_decode(p2, pw["wa2p"], gla_ba[0], gla_norm[0], state_gla[0])
    y = _out_ffn([o_g], x1, [pw["woo"]], norm_ffn[1], pw["w13t"][1], pw["w2b"][1], norm_final, bsz)
    return y.reshape(bsz, 1, d), rows, win, hg_state[None], gla_state[None]


def kernel(x_prompt, x_sample, cache_nsa_kv, cache_win_kv, state_hgrn, state_gla, page_table, w_in_even,
           hg_lb_logits, hg_norm, nsa_cmp_pos, nsa_cmp_w1, nsa_cmp_w2, w_out_even, w_in_odd, gla_wa2, gla_ba,
           gla_norm, w_out_odd, norm_mix, norm_ffn, norm_final, ffn_w13, ffn_w2):
    pw = _prep_weights(w_in_even, nsa_cmp_pos, nsa_cmp_w1, nsa_cmp_w2, w_out_even, w_in_odd, gla_wa2, w_out_odd,
                       ffn_w13, ffn_w2)
    y_p, kv_p, win_p, hg_p, gla_p = _forward_prompt(x_prompt, pw, hg_lb_logits, hg_norm, gla_ba, gla_norm,
                                                    norm_mix, norm_ffn, norm_final)
    y_s, kv_s, win_s, hg_s, gla_s = _forward_sample(x_sample, pw, cache_nsa_kv, cache_win_kv, state_hgrn,
                                                    state_gla, page_table, hg_lb_logits, hg_norm, gla_ba,
                                                    gla_norm, norm_mix, norm_ffn, norm_final)
    return (y_p, y_s, kv_p, kv_s, win_p, win_s, hg_p, hg_s, gla_p, gla_s)
```

```python
import functools

import jax
import jax.numpy as jnp
from jax import lax
from jax.experimental import pallas as pl
from jax.experimental.pallas import tpu as pltpu

F32 = jnp.float32
BF16 = jnp.bfloat16
I32 = jnp.int32

D_MODEL = 1024
HG_H, HG_DK, HG_DV = 4, 128, 128
NSA_H, NSA_DH, NSA_G, NSA_R = 8, 64, 2, 4
NSA_BLK = 64
NSA_TOPK = 16
NSA_WINDOW = 512
NSA_QB = 128
GLA_H, GLA_DK, GLA_DV = 4, 128, 256
GLA_RANK = 16
GLA_TAU = 16.0
D_FF = 2816
EPS = 1e-6
NEG = -1e30
TINY = 1e-30
PAGE = 128

PROJ_W = 3584
ODD_W = 3200
EV_Q, EV_F, EV_I, EV_OG = 0, 512, 1024, 1536
EV_NQ, EV_CMP, EV_SLC, EV_WIN, EV_GATE = 2048, 2560, 2816, 3072, 3328
OD_Q, OD_K, OD_V, OD_R, OD_A = 0, 512, 1024, 2048, 3072

LANES = 128
V7X_VMEM_LIMIT = 56 * 1024 * 1024
FF_TILE = 256
CHUNK = 128
SUB = 16

NT_DIMS = (((1,), (1,)), ((), ()))
TN_DIMS = (((0,), (0,)), ((), ()))


def _cparams(sem):
    return pltpu.CompilerParams(dimension_semantics=sem, vmem_limit_bytes=V7X_VMEM_LIMIT)


def _rms(x, g):
    return x * lax.rsqrt(jnp.mean(x * x, axis=-1, keepdims=True) + EPS) * g


def _sigmoid(x):
    return 1.0 / (1.0 + jnp.exp(-x))


def _silu(x):
    return x * _sigmoid(x)


def _iota(shape, dim):
    return lax.broadcasted_iota(I32, shape, dim)


def _norm_proj_kernel(x_ref, g_ref, w_ref, o_ref, xn_ref):
    @pl.when(pl.program_id(1) == 0)
    def _():
        xn_ref[...] = _rms(x_ref[...], g_ref[...]).astype(BF16)

    o_ref[...] = jnp.dot(xn_ref[...], w_ref[...], preferred_element_type=F32)


def _norm_proj(x2d, gain, w, tm, tn):
    m, k = x2d.shape
    n = w.shape[1]
    return pl.pallas_call(
        _norm_proj_kernel,
        grid=(m // tm, n // tn),
        in_specs=[pl.BlockSpec((tm, k), lambda i, j: (i, 0)),
                  pl.BlockSpec((1, k), lambda i, j: (0, 0)),
                  pl.BlockSpec((k, tn), lambda i, j: (0, j))],
        out_specs=pl.BlockSpec((tm, tn), lambda i, j: (i, j)),
        out_shape=jax.ShapeDtypeStruct((m, n), F32),
        scratch_shapes=[pltpu.VMEM((tm, k), BF16)],
        compiler_params=_cparams(("parallel", "arbitrary")),
        name="norm_proj",
    )(x2d, gain.reshape(1, k), w)


PM_W = 2304
PM_CMP = 2048
T_Q, T_ROWS, T_WIN, T_GATE, T_END = 0, 512, 1024, 1280, 1312


def _norm_proj_t_kernel(x_ref, g_ref, w_ref, wt_ref, o_ref, q_ref, rows_ref, win_ref, gate_ref):
    xn = _rms(x_ref[...], g_ref[...]).astype(BF16)
    o_ref[...] = jnp.dot(xn, w_ref[...], preferred_element_type=F32)
    t = lax.dot_general(wt_ref[...], xn, NT_DIMS, preferred_element_type=F32)
    q_ref[0] = t[T_Q:T_ROWS]
    rows_ref[0] = t[T_ROWS:T_WIN]
    win_ref[0] = t[T_WIN:T_GATE]
    gate_ref[0] = t[T_GATE:T_END]


def _norm_proj_t(x2d, gain, w, wt, bsz, seq, tm):
    k = x2d.shape[1]
    nt = seq // tm
    tr = lambda rows: pl.BlockSpec((1, rows, tm), lambda b, t: (b, 0, t))
    sizes = (T_ROWS - T_Q, T_WIN - T_ROWS, T_GATE - T_WIN, T_END - T_GATE)
    return pl.pallas_call(
        _norm_proj_t_kernel,
        grid=(bsz, nt),
        in_specs=[pl.BlockSpec((tm, k), lambda b, t: (b * nt + t, 0)),
                  pl.BlockSpec((1, k), lambda b, t: (0, 0)),
                  pl.BlockSpec(w.shape, lambda b, t: (0, 0)),
                  pl.BlockSpec(wt.shape, lambda b, t: (0, 0))],
        out_specs=[pl.BlockSpec((tm, PM_W), lambda b, t: (b * nt + t, 0))] + [tr(r) for r in sizes],
        out_shape=[jax.ShapeDtypeStruct((bsz * seq, PM_W), F32)]
        + [jax.ShapeDtypeStruct((bsz, r, seq), F32) for r in sizes],
        compiler_params=_cparams(("parallel", "arbitrary")),
        name="norm_proj_t",
    )(x2d, gain.reshape(1, k), w, wt)


def _out_ffn_kernel(*refs, n_mix, final_norm):
    mix_refs = refs[:n_mix]
    res_ref = refs[n_mix]
    wo_refs = refs[n_mix + 1:2 * n_mix + 1]
    g_ref, w13_ref, w2_ref = refs[2 * n_mix + 1:2 * n_mix + 4]
    pos = 2 * n_mix + 4
    gf_ref = refs[pos] if final_norm else None
    pos += 1 if final_norm else 0
    o_ref, x1_ref, h_ref, acc_ref = refs[pos:pos + 4]
    j = pl.program_id(1)

    @pl.when(j == 0)
    def _():
        x1 = res_ref[...]
        for m_ref, w_ref in zip(mix_refs, wo_refs):
            x1 = x1 + jnp.dot(m_ref[...].astype(BF16), w_ref[...], preferred_element_type=F32)
        x1_ref[...] = x1
        h_ref[...] = _rms(x1, g_ref[...]).astype(BF16)
        acc_ref[...] = jnp.zeros_like(acc_ref)

    gu = jnp.dot(h_ref[...], w13_ref[...], preferred_element_type=F32)
    act = _silu(gu[:, :FF_TILE]) * gu[:, FF_TILE:]
    acc_ref[...] += jnp.dot(act.astype(BF16), w2_ref[...], preferred_element_type=F32)

    @pl.when(j == pl.num_programs(1) - 1)
    def _():
        y = x1_ref[...] + acc_ref[...]
        if final_norm:
            y = _rms(y, gf_ref[...])
        o_ref[...] = y


def _out_ffn(mixes, res, wos, g_ffn, w13t, w2, g_final, tm):
    m, d = res.shape
    n_mix = len(mixes)
    nj = D_FF // FF_TILE
    final_norm = g_final is not None
    in_specs = [pl.BlockSpec((tm, mx.shape[1]), lambda i, j: (i, 0)) for mx in mixes]
    in_specs.append(pl.BlockSpec((tm, d), lambda i, j: (i, 0)))
    in_specs += [pl.BlockSpec(w.shape, lambda i, j: (0, 0)) for w in wos]
    in_specs += [pl.BlockSpec((1, d), lambda i, j: (0, 0)),
                 pl.BlockSpec((d, 2 * FF_TILE), lambda i, j: (0, j)),
                 pl.BlockSpec((FF_TILE, d), lambda i, j: (j, 0))]
    args = list(mixes) + [res] + list(wos) + [g_ffn.reshape(1, d), w13t, w2]
    if final_norm:
        in_specs.append(pl.BlockSpec((1, d), lambda i, j: (0, 0)))
        args.append(g_final.reshape(1, d))
    return pl.pallas_call(
        functools.partial(_out_ffn_kernel, n_mix=n_mix, final_norm=final_norm),
        grid=(m // tm, nj),
        in_specs=in_specs,
        out_specs=pl.BlockSpec((tm, d), lambda i, j: (i, 0)),
        out_shape=jax.ShapeDtypeStruct((m, d), F32),
        scratch_shapes=[pltpu.VMEM((tm, d), F32), pltpu.VMEM((tm, d), BF16), pltpu.VMEM((tm, d), F32)],
        compiler_params=_cparams(("parallel", "arbitrary")),
        name="out_ffn",
    )(*args)


def _gla_chunk(q, k, v, g, st):
    c = q.shape[0]
    tri = (_iota((c, c), 0) >= _iota((c, c), 1)).astype(F32)
    b = jnp.dot(tri, g, precision=lax.Precision.HIGHEST, preferred_element_type=F32)
    o = lax.dot_general((q * jnp.exp(b)).astype(BF16), st.astype(BF16), NT_DIMS, preferred_element_type=F32)
    lane = _iota((SUB, c), 1)
    row = _iota((SUB, c), 0)
    rows = []
    for blk in range(c // SUB):
        lo = blk * SUB
        b_i, q_i, k_i = b[lo:lo + SUB], q[lo:lo + SUB], k[lo:lo + SUB]
        a_blk = jnp.zeros((SUB, c), F32)
        for s in range(SUB):
            e = jnp.exp(jnp.minimum(b_i - b_i[s:s + 1], 0.0))
            a = jnp.sum(q_i * e * k_i[s:s + 1], axis=1, keepdims=True)
            a_blk = jnp.where(lane == lo + s, a, a_blk)
        a_blk = jnp.where(lane <= lo + row, a_blk, 0.0)
        if blk > 0:
            r = b[lo - 1:lo]
            qt = q_i * jnp.exp(b_i - r)
            kt = k * jnp.exp(jnp.minimum(r - b, 0.0))
            a_off = lax.dot_general(qt.astype(BF16), kt.astype(BF16), NT_DIMS, preferred_element_type=F32)
            a_blk = jnp.where(lane < lo, a_off, a_blk)
        rows.append(a_blk)
    a_full = jnp.concatenate(rows, axis=0)
    o = o + jnp.dot(a_full.astype(BF16), v.astype(BF16), preferred_element_type=F32)
    b_last = b[c - 1:c]
    kd = k * jnp.exp(b_last - b)
    st_new = st * jnp.exp(b_last) + lax.dot_general(v.astype(BF16), kd.astype(BF16), TN_DIMS,
                                                    preferred_element_type=F32)
    return o, st_new


def _lower_bound(lbl_ref, col):
    l = lbl_ref[:, col:col + HG_DK]
    e = jnp.exp(l - jnp.max(l, axis=0, keepdims=True))
    return e[0:1] / jnp.sum(e, axis=0, keepdims=True)


def _hgrn_kernel(q_ref, f_ref, i_ref, og_ref, lbl_ref, gain_ref, o_ref, s_ref, st_ref, *, nch):
    t = pl.program_id(1)

    @pl.when(t == 0)
    def _():
        st_ref[...] = jnp.zeros_like(st_ref)

    def body(ci, carry):
        r0 = pl.multiple_of(ci * CHUNK, CHUNK)
        rs = pl.ds(r0, CHUNK)
        for h in range(HG_H):
            cs = slice(h * HG_DK, (h + 1) * HG_DK)
            lb = _lower_bound(lbl_ref, h * HG_DK)
            q = _silu(q_ref[rs, cs])
            f = lb + (1.0 - lb) * _sigmoid(f_ref[rs, cs])
            o, st_new = _gla_chunk(q, 1.0 - f, i_ref[rs, cs], jnp.log(f), st_ref[h])
            st_ref[h] = st_new
            o = _rms(o, gain_ref[...]) * _sigmoid(og_ref[rs, cs])
            o_ref[rs, cs] = o.astype(o_ref.dtype)
        return carry

    lax.fori_loop(0, nch, body, 0)

    @pl.when(t == pl.num_programs(1) - 1)
    def _():
        for h in range(HG_H):
            s_ref[0, h] = st_ref[h].T


def _hgrn_prompt(p2d, lb_logits, gain, bsz, seq, tb):
    nt = seq // tb
    hk = HG_H * HG_DK
    row = lambda b, t: b * nt + t
    col_spec = lambda c: pl.BlockSpec((tb, hk), lambda b, t, c=c: (row(b, t), c))
    return pl.pallas_call(
        functools.partial(_hgrn_kernel, nch=tb // CHUNK),
        grid=(bsz, nt),
        in_specs=[col_spec(EV_Q // hk), col_spec(EV_F // hk), col_spec(EV_I // hk), col_spec(EV_OG // hk),
                  pl.BlockSpec(lb_logits.shape, lambda b, t: (0, 0)),
                  pl.BlockSpec((1, HG_DV), lambda b, t: (0, 0))],
        out_specs=[pl.BlockSpec((tb, hk), lambda b, t: (row(b, t), 0)),
                   pl.BlockSpec((1, HG_H, HG_DK, HG_DV), lambda b, t: (b, 0, 0, 0))],
        out_shape=[jax.ShapeDtypeStruct((bsz * seq, hk), BF16),
                   jax.ShapeDtypeStruct((bsz, HG_H, HG_DK, HG_DV), F32)],
        scratch_shapes=[pltpu.VMEM((HG_H, HG_DV, HG_DK), F32)],
        compiler_params=_cparams(("parallel", "arbitrary")),
        name="hgrn_chunk",
    )(p2d, p2d, p2d, p2d, lb_logits, gain.reshape(1, HG_DV))


def _log_sigmoid(x):
    return jnp.minimum(x, 0.0) - jnp.log(1.0 + jnp.exp(-jnp.abs(x)))


def _gla_kernel(q_ref, k_ref, v_ref, r_ref, a_ref, wa_ref, ba_ref, gain_ref, o_ref, s_ref, st_ref, *, nch):
    t = pl.program_id(1)

    @pl.when(t == 0)
    def _():
        st_ref[...] = jnp.zeros_like(st_ref)

    def body(ci, carry):
        r0 = pl.multiple_of(ci * CHUNK, CHUNK)
        rs = pl.ds(r0, CHUNK)
        gate = jnp.dot(a_ref[rs, :].astype(BF16), wa_ref[...], preferred_element_type=F32) + ba_ref[...]
        logf = _log_sigmoid(gate) * (1.0 / GLA_TAU)
        for h in range(GLA_H):
            ks = slice(h * GLA_DK, (h + 1) * GLA_DK)
            vs = slice(h * GLA_DV, (h + 1) * GLA_DV)
            q = q_ref[rs, ks] * (GLA_DK ** -0.5)
            o, st_new = _gla_chunk(q, k_ref[rs, ks], v_ref[rs, vs], logf[:, ks], st_ref[h])
            st_ref[h] = st_new
            o = _rms(o, gain_ref[...]) * _silu(r_ref[rs, vs])
            o_ref[rs, vs] = o.astype(o_ref.dtype)
        return carry

    lax.fori_loop(0, nch, body, 0)

    @pl.when(t == pl.num_programs(1) - 1)
    def _():
        for h in range(GLA_H):
            s_ref[0, h] = st_ref[h].T


def _gla_prompt(p2d, wa2p, ba, gain, bsz, seq, tb):
    nt = seq // tb
    hk, hv = GLA_H * GLA_DK, GLA_H * GLA_DV
    row = lambda b, t: b * nt + t
    return pl.pallas_call(
        functools.partial(_gla_kernel, nch=tb // CHUNK),
        grid=(bsz, nt),
        in_specs=[pl.BlockSpec((tb, hk), lambda b, t: (row(b, t), OD_Q // hk)),
                  pl.BlockSpec((tb, hk), lambda b, t: (row(b, t), OD_K // hk)),
                  pl.BlockSpec((tb, hv), lambda b, t: (row(b, t), OD_V // hv)),
                  pl.BlockSpec((tb, hv), lambda b, t: (row(b, t), OD_R // hv)),
                  pl.BlockSpec((tb, LANES), lambda b, t: (row(b, t), OD_A // LANES)),
                  pl.BlockSpec(wa2p.shape, lambda b, t: (0, 0)),
                  pl.BlockSpec((1, hk), lambda b, t: (0, 0)),
                  pl.BlockSpec((1, GLA_DV), lambda b, t: (0, 0))],
        out_specs=[pl.BlockSpec((tb, hv), lambda b, t: (row(b, t), 0)),
                   pl.BlockSpec((1, GLA_H, GLA_DK, GLA_DV), lambda b, t: (b, 0, 0, 0))],
        out_shape=[jax.ShapeDtypeStruct((bsz * seq, hv), BF16),
                   jax.ShapeDtypeStruct((bsz, GLA_H, GLA_DK, GLA_DV), F32)],
        scratch_shapes=[pltpu.VMEM((GLA_H, GLA_DV, GLA_DK), F32)],
        compiler_params=_cparams(("parallel", "arbitrary")),
        name="gla_chunk",
    )(p2d, p2d, p2d, p2d, p2d, wa2p, ba.reshape(1, hk), gain.reshape(1, GLA_DV))


def _to_columns(x):
    bsz = x.shape[0]
    if bsz < LANES:
        x = jnp.concatenate([x, jnp.zeros((LANES - bsz, x.shape[1]), x.dtype)], axis=0)
    return x.T


def _decode_update(q, k, v, g, s_ref, so_ref, o_scr):
    bsz = q.shape[0]
    qt, kt, et = _to_columns(q), _to_columns(k), _to_columns(jnp.exp(g))
    for b in range(bsz):
        s_new = et[:, b:b + 1] * s_ref[b, 0] + kt[:, b:b + 1] * v[b:b + 1, :]
        so_ref[b, 0] = s_new
        o_scr[b:b + 1, :] = jnp.sum(qt[:, b:b + 1] * s_new, axis=0, keepdims=True)


def _hgrn_decode_kernel(q_ref, f_ref, i_ref, og_ref, lbl_ref, gain_ref, s_ref, o_ref, so_ref, o_scr):
    l = lbl_ref[0]
    e = jnp.exp(l - jnp.max(l, axis=0, keepdims=True))
    lb = e[0:1] / jnp.sum(e, axis=0, keepdims=True)
    f = lb + (1.0 - lb) * _sigmoid(f_ref[...])
    _decode_update(_silu(q_ref[...]), 1.0 - f, i_ref[...], jnp.log(f), s_ref, so_ref, o_scr)
    o_ref[...] = _rms(o_scr[...], gain_ref[...]) * _sigmoid(og_ref[...])


def _hgrn_decode(p2d, lb_logits, gain, state):
    bsz = p2d.shape[0]
    col = lambda c: pl.BlockSpec((bsz, HG_DK), lambda h, c=c: (0, c + h))
    lbl3 = lb_logits.reshape(lb_logits.shape[0], HG_H, HG_DK).transpose(1, 0, 2)
    st_spec = pl.BlockSpec((bsz, 1, HG_DK, HG_DV), lambda h: (0, h, 0, 0))
    return pl.pallas_call(
        _hgrn_decode_kernel,
        grid=(HG_H,),
        in_specs=[col(EV_Q // HG_DK), col(EV_F // HG_DK), col(EV_I // HG_DK), col(EV_OG // HG_DK),
                  pl.BlockSpec((1,) + lbl3.shape[1:], lambda h: (h, 0, 0)),
                  pl.BlockSpec((1, HG_DV), lambda h: (0, 0)),
                  st_spec],
        out_specs=[pl.BlockSpec((bsz, HG_DV), lambda h: (0, h)), st_spec],
        out_shape=[jax.ShapeDtypeStruct((bsz, HG_H * HG_DV), F32),
                   jax.ShapeDtypeStruct(state.shape, F32)],
        scratch_shapes=[pltpu.VMEM((bsz, HG_DV), F32)],
        compiler_params=_cparams(("arbitrary",)),
        name="hgrn_decode",
    )(p2d, p2d, p2d, p2d, lbl3, gain.reshape(1, HG_DV), state)


def _gla_decode_kernel(q_ref, k_ref, v_ref, r_ref, a_ref, wa_ref, ba_ref, gain_ref, s_ref, o_ref, so_ref, o_scr):
    gate = jnp.dot(a_ref[...].astype(BF16), wa_ref[...], preferred_element_type=F32) + ba_ref[...]
    logf = _log_sigmoid(gate) * (1.0 / GLA_TAU)
    _decode_update(q_ref[...] * (GLA_DK ** -0.5), k_ref[...], v_ref[...], logf, s_ref, so_ref, o_scr)
    o_ref[...] = _rms(o_scr[...], gain_ref[...]) * _silu(r_ref[...])


def _gla_decode(p2d, wa2p, ba, gain, state):
    bsz = p2d.shape[0]
    hk = GLA_H * GLA_DK
    st_spec = pl.BlockSpec((bsz, 1, GLA_DK, GLA_DV), lambda h: (0, h, 0, 0))
    return pl.pallas_call(
        _gla_decode_kernel,
        grid=(GLA_H,),
        in_specs=[pl.BlockSpec((bsz, GLA_DK), lambda h: (0, OD_Q // GLA_DK + h)),
                  pl.BlockSpec((bsz, GLA_DK), lambda h: (0, OD_K // GLA_DK + h)),
                  pl.BlockSpec((bsz, GLA_DV), lambda h: (0, OD_V // GLA_DV + h)),
                  pl.BlockSpec((bsz, GLA_DV), lambda h: (0, OD_R // GLA_DV + h)),
                  pl.BlockSpec((bsz, LANES), lambda h: (0, OD_A // LANES)),
                  pl.BlockSpec((LANES, GLA_DK), lambda h: (0, h)),
                  pl.BlockSpec((1, GLA_DK), lambda h: (0, h)),
                  pl.BlockSpec((1, GLA_DV), lambda h: (0, 0)),
                  st_spec],
        out_specs=[pl.BlockSpec((bsz, GLA_DV), lambda h: (0, h)), st_spec],
        out_shape=[jax.ShapeDtypeStruct((bsz, GLA_H * GLA_DV), F32),
                   jax.ShapeDtypeStruct(state.shape, F32)],
        scratch_shapes=[pltpu.VMEM((bsz, GLA_DV), F32)],
        compiler_params=_cparams(("arbitrary",)),
        name="gla_decode",
    )(p2d, p2d, p2d, p2d, p2d, wa2p, ba.reshape(1, hk), gain.reshape(1, GLA_DV), state)


def _compress_kernel(xa_ref, xb_ref, pe_ref, w1_ref, w2_ref, o_ref, acc_ref):
    nb = o_ref.shape[0]
    acc_ref[...] = jnp.zeros_like(acc_ref)
    for l in range(NSA_BLK):
        rows = pl.ds(l, nb, stride=NSA_BLK)
        x = jnp.concatenate([xa_ref[rows, :], xb_ref[rows, :]], axis=1) + pe_ref[l:l + 1, :]
        acc_ref[...] += jnp.dot(x.astype(BF16), w1_ref[l], preferred_element_type=F32)
    o_ref[...] = jnp.dot(_silu(acc_ref[...]).astype(BF16), w2_ref[...], preferred_element_type=F32)


def _compress_prompt(p2d, pe4, w1bd, w2bd):
    n = p2d.shape[0]
    rc = min(8192, n)
    cw = 4 * NSA_DH
    c0 = PM_CMP // LANES
    return pl.pallas_call(
        _compress_kernel,
        grid=(n // rc,),
        in_specs=[pl.BlockSpec((rc, LANES), lambda i: (i, c0)),
                  pl.BlockSpec((rc, LANES), lambda i: (i, c0 + 1)),
                  pl.BlockSpec(pe4.shape, lambda i: (0, 0)),
                  pl.BlockSpec(w1bd.shape, lambda i: (0, 0, 0)),
                  pl.BlockSpec((cw, cw), lambda i: (0, 0))],
        out_specs=pl.BlockSpec((rc // NSA_BLK, cw), lambda i: (i, 0)),
        out_shape=jax.ShapeDtypeStruct((n // NSA_BLK, cw), F32),
        scratch_shapes=[pltpu.VMEM((rc // NSA_BLK, cw), F32)],
        compiler_params=_cparams(("arbitrary",)),
        name="nsa_compress",
    )(p2d, p2d, pe4, w1bd, w2bd)


AQB = 256
AQB_SHIFT = 8
AUG = 128
ONES_ROWS = 16


def _slopes_lane(g, lanes):
    r = _iota((1, lanes), 1) >> AQB_SHIFT
    out = jnp.zeros((1, lanes), F32)
    for rr in range(NSA_R):
        out = jnp.where(r == rr, 2.0 ** (-(g * NSA_R + rr + 1)), out)
    return out


def _build_key_features(src_ref, row0, dst_ref, g, seq, with_onehot):
    r = _iota((NSA_DH, LANES), 0)
    for cb in range(seq // LANES):
        kpos = cb * LANES + _iota((NSA_DH, LANES), 1)
        feat = jnp.where(r == 32, (kpos >> 6).astype(F32),
                         jnp.where(r == 33, (kpos & 63).astype(F32),
                                   jnp.where((r == 34) | (r == 35), 1.0, 0.0)))
        if with_onehot:
            feat = jnp.where((kpos >> 6) == r, 1.0, feat)
        kt = src_ref[0, row0:row0 + NSA_DH, cb * LANES:(cb + 1) * LANES]
        dst_ref[g, cb * LANES:(cb + 1) * LANES, :] = jnp.concatenate([kt, feat], axis=0).T.astype(BF16)


def _attn_tile(k_ref, g, k0, v_ref, v_row0, qa_ref, mask, m_ref, acc_ref, stream):
    s = jnp.dot(k_ref[g, pl.ds(k0, AQB), :], qa_ref[g], preferred_element_type=F32)
    if mask is not None:
        s = jnp.where(mask, s, NEG)
    m_old = m_ref[stream]
    m_new = jnp.maximum(m_old, jnp.max(s, axis=0, keepdims=True))
    p = jnp.exp(s - m_new).astype(BF16)
    vt = v_ref[0, v_row0:v_row0 + NSA_DH, pl.ds(k0, AQB)].astype(BF16)
    vt = jnp.concatenate([vt, jnp.ones((ONES_ROWS, AQB), BF16)], axis=0)
    acc_ref[stream] = jnp.exp(m_old - m_new) * acc_ref[stream] + jnp.dot(vt, p, preferred_element_type=F32)
    m_ref[stream] = m_new


def _nsa_attn_kernel(qt_ref, rows_ref, win_ref, gt_ref, kc_ref, vct_ref, o_ref, ks_s, kw_s, qa_s, m_s, acc_s,
                     *, nblk):
    qb = pl.program_id(1)
    seq = rows_ref.shape[2]
    q0 = qb * AQB
    nq = NSA_R * AQB
    half = NSA_G * NSA_DH

    @pl.when(qb == 0)
    def _():
        for g in range(NSA_G):
            _build_key_features(rows_ref, 2 * half + g * NSA_DH, ks_s, g, seq, True)
            _build_key_features(win_ref, g * NSA_DH, kw_s, g, seq, False)

    pq = q0 + (_iota((1, nq), 1) & (AQB - 1))
    gates = _sigmoid(gt_ref[0])
    ksel = min(NSA_TOPK, nblk)
    scale = NSA_DH ** -0.5
    key_i = _iota((AQB, nq), 0)
    qry_t = _iota((AQB, nq), 1) & (AQB - 1)
    causal = key_i <= qry_t
    beyond = key_i > qry_t
    pq_f = q0 + _iota((32, AQB), 1)
    feat_row = _iota((32, AQB), 0)

    m_s[...] = jnp.full(m_s.shape, NEG, F32)
    acc_s[...] = jnp.zeros_like(acc_s)

    def result(i):
        return acc_s[i, 0:NSA_DH, :] / jnp.maximum(acc_s[i, NSA_DH:NSA_DH + 1, :], TINY)

    o_cmp = []
    for g in range(NSA_G):
        slope = _slopes_lane(g, nq)

        qt_g = [qt_ref[0, (g * NSA_R + r) * NSA_DH:(g * NSA_R + r + 1) * NSA_DH, :] * scale for r in range(NSA_R)]
        qs_t = jnp.concatenate(qt_g, axis=1).astype(BF16)
        sc = jnp.dot(kc_ref[0, g], qs_t, preferred_element_type=F32)
        dist_c = pq - (_iota((nblk, nq), 0) * NSA_BLK + NSA_BLK - 1)
        mask_c = dist_c >= 0
        sc = jnp.where(mask_c, sc - slope * dist_c.astype(F32), NEG)
        pc = jnp.where(mask_c, jnp.exp(sc - jnp.max(sc, axis=0, keepdims=True)), 0.0)
        pc = pc / jnp.maximum(jnp.sum(pc, axis=0, keepdims=True), TINY)
        o_cmp.append(jnp.dot(vct_ref[0, g], pc.astype(BF16), preferred_element_type=F32))
        imp = pc[:, 0:AQB]
        for r in range(1, NSA_R):
            imp = imp + pc[:, r * AQB:(r + 1) * AQB]

        n_io = _iota((nblk, AQB), 0)
        tpos = q0 + _iota((nblk, AQB), 1)
        forced = (n_io == (tpos >> 6)) | (n_io == 0)
        started = n_io * NSA_BLK <= tpos
        score = jnp.where(forced, jnp.inf, jnp.where(started, imp, -jnp.inf))
        rank = jnp.zeros((nblk, AQB), I32)
        for i in range(nblk):
            row = score[i:i + 1, :]
            rank = rank + ((row > score) | ((row == score) & (i < n_io))).astype(I32)
        sel_bias = jnp.where((rank < ksel) & started, 0.0, NEG)
        if nblk < 32:
            sel_bias = jnp.concatenate([sel_bias, jnp.zeros((32 - nblk, AQB), F32)], axis=0)

        cols = []
        for r in range(NSA_R):
            sl = 2.0 ** (-(g * NSA_R + r + 1))
            pos_feat = jnp.where(feat_row == 0, 64.0 * sl,
                                 jnp.where(feat_row == 1, sl,
                                           jnp.where(feat_row == 2, (-64.0 * sl) * (pq_f >> 6).astype(F32),
                                                     jnp.where(feat_row == 3, (-sl) * (pq_f & 63).astype(F32), 0.0))))
            cols.append(jnp.concatenate([qt_g[r], sel_bias, pos_feat], axis=0))
        qa_s[g] = jnp.concatenate(cols, axis=1).astype(BF16)

    def slc_tile(k0, mask):
        for g in range(NSA_G):
            _attn_tile(ks_s, g, k0, rows_ref, 3 * half + g * NSA_DH, qa_s, mask, m_s, acc_s, g)

    def win_tile(k0, mask):
        for g in range(NSA_G):
            _attn_tile(kw_s, g, k0, win_ref, half + g * NSA_DH, qa_s, mask, m_s, acc_s, NSA_G + g)

    def slc_body(kt, c):
        slc_tile(pl.multiple_of(kt * AQB, AQB), None)
        return c

    lax.fori_loop(0, qb, slc_body, 0)

    @pl.when(qb >= 2)
    def _():
        win_tile(pl.multiple_of(q0 - 2 * AQB, AQB), beyond)

    @pl.when(qb >= 1)
    def _():
        win_tile(pl.multiple_of(q0 - AQB, AQB), None)

    slc_tile(pl.multiple_of(q0, AQB), causal)
    win_tile(pl.multiple_of(q0, AQB), causal)

    for g in range(NSA_G):
        def gate_row(j, g=g):
            return jnp.concatenate([gates[g * 12 + r * 3 + j:g * 12 + r * 3 + j + 1, :] for r in range(NSA_R)],
                                   axis=1)

        o_t = gate_row(0) * o_cmp[g] + gate_row(1) * result(g) + gate_row(2) * result(NSA_G + g)
        o_st = jnp.concatenate([o_t[:, r * AQB:(r + 1) * AQB] for r in range(NSA_R)], axis=0)
        o_ref[:, g * NSA_R * NSA_DH:(g + 1) * NSA_R * NSA_DH] = o_st.T.astype(o_ref.dtype)


def _nsa_attn_prompt(qt, rows_t, win_t, gate_t, kc, vct, bsz, seq):
    assert NSA_WINDOW == 2 * AQB and seq % AQB == 0 and seq // NSA_BLK <= 32
    nqb = seq // AQB
    nblk = seq // NSA_BLK
    qw = NSA_H * NSA_DH
    nq = NSA_R * AQB
    per_b = lambda a: pl.BlockSpec((1,) + a.shape[1:], lambda b, i: (b,) + (0,) * (a.ndim - 1))
    per_q = lambda a: pl.BlockSpec((1, a.shape[1], AQB), lambda b, i: (b, 0, i))
    return pl.pallas_call(
        functools.partial(_nsa_attn_kernel, nblk=nblk),
        grid=(bsz, nqb),
        in_specs=[per_q(qt), per_b(rows_t), per_b(win_t), per_q(gate_t), per_b(kc), per_b(vct)],
        out_specs=pl.BlockSpec((AQB, qw), lambda b, i: (b * nqb + i, 0)),
        out_shape=jax.ShapeDtypeStruct((bsz * seq, qw), BF16),
        scratch_shapes=[pltpu.VMEM((NSA_G, seq, AUG), BF16), pltpu.VMEM((NSA_G, seq, AUG), BF16),
                        pltpu.VMEM((NSA_G, AUG, nq), BF16),
                        pltpu.VMEM((2 * NSA_G, 1, nq), F32), pltpu.VMEM((2 * NSA_G, NSA_DH + ONES_ROWS, nq), F32)],
        compiler_params=_cparams(("parallel", "arbitrary")),
        name="nsa_attn",
    )(qt, rows_t, win_t, gate_t, kc, vct)


QROWS = 16


def _head_slopes(rows):
    r = _iota((rows, 1), 0)
    out = jnp.zeros((rows, 1), F32)
    for h in range(NSA_H):
        out = jnp.where(r == h, 2.0 ** (-(h + 1)), out)
    return out


def _softmax_rows(s, valid):
    s = jnp.where(valid, s, NEG)
    p = jnp.where(valid, jnp.exp(s - jnp.max(s, axis=1, keepdims=True)), 0.0)
    return p / jnp.maximum(jnp.sum(p, axis=1, keepdims=True), TINY)


def _padded_queries(q_ref):
    q = q_ref[0] * (NSA_DH ** -0.5)
    return jnp.concatenate([q, jnp.zeros((QROWS - NSA_H, NSA_DH), F32)], axis=0).astype(BF16)


PLANES = 4 * NSA_G
PAGE_ROWS = PLANES * NSA_DH
CMP_PLANES = 2 * NSA_G


def _dec_cmp_kernel(pt_ref, cache_ref, pe_ref, w1_ref, w2_ref, q_ref, o_ref, sel_ref, buf, acc_ref, cmp_ref, sem,
                    *, npages, nsel):
    b = pl.program_id(0)
    nb = npages * (PAGE // NSA_BLK)
    nrow = npages * CMP_PLANES
    crows = CMP_PLANES * NSA_DH

    def page_copy(pg):
        src = pl.multiple_of(pt_ref[b * npages + pg] * PAGE_ROWS, PAGE_ROWS)
        return pltpu.make_async_copy(cache_ref.at[pl.ds(src, crows), :],
                                     buf.at[pl.ds(pl.multiple_of(pg * crows, crows), crows), :], sem.at[0])

    def start(pg, c):
        page_copy(pg).start()
        return c

    def wait(pg, c):
        page_copy(pg).wait()
        return c

    lax.fori_loop(0, npages, start, 0)
    lax.fori_loop(0, npages, wait, 0)

    is_k = ((_iota((nrow, 1), 0) >> 1) & 1) == 0
    acc_ref[...] = jnp.zeros_like(acc_ref)
    for d in range(NSA_DH):
        x = buf[pl.ds(d, nrow, stride=NSA_DH), :]
        x = x + jnp.where(is_k, pe_ref[d, 0:1, :], pe_ref[d, 1:2, :])
        acc_ref[...] += jnp.dot(x.astype(BF16), w1_ref[d], preferred_element_type=F32)
    acc = acc_ref[...]
    h = _silu(jnp.where(is_k, acc[:, :LANES], acc[:, LANES:]))
    c2 = jnp.dot(h.astype(BF16), w2_ref[...], preferred_element_type=F32)
    cmp_ref[...] = jnp.where(is_k, c2[:, :LANES], c2[:, LANES:])

    pos = nb * NSA_BLK
    q16 = _padded_queries(q_ref)
    slope = _head_slopes(QROWS)
    row_grp = _iota((QROWS, 1), 0) >> 2
    lane = _iota((1, nb), 1)
    blk_of = jnp.where(lane < npages, 2 * lane, 2 * (lane - npages) + 1)
    dist = (pos - (blk_of * NSA_BLK + NSA_BLK - 1)).astype(F32)
    ri, ci = _iota((nb, nb), 0), _iota((nb, nb), 1)
    blk_r = jnp.where(ri < npages, 2 * ri, 2 * (ri - npages) + 1)
    blk_c = jnp.where(ci < npages, 2 * ci, 2 * (ci - npages) + 1)
    o_all = jnp.zeros((QROWS, NSA_DH), F32)
    for g in range(NSA_G):
        kc = cmp_ref[pl.ds(g, npages, stride=CMP_PLANES), :].astype(BF16)
        vc = cmp_ref[pl.ds(NSA_G + g, npages, stride=CMP_PLANES), :].astype(BF16)
        s = jnp.concatenate([lax.dot_general(q16, kc[:, j * NSA_DH:(j + 1) * NSA_DH], NT_DIMS,
                                             preferred_element_type=F32) for j in range(2)], axis=1)
        p = _softmax_rows(s - slope * dist, dist >= 0)
        pb = p.astype(BF16)
        o_g = sum(jnp.dot(pb[:, j * npages:(j + 1) * npages], vc[:, j * NSA_DH:(j + 1) * NSA_DH],
                          preferred_element_type=F32) for j in range(2))
        o_all = jnp.where(row_grp == g, o_g, o_all)
        imp = jnp.sum(jnp.where(row_grp == g, p, 0.0), axis=0, keepdims=True)
        score_row = jnp.where(blk_of == 0, jnp.inf, imp)
        score_col = jnp.sum(jnp.where(ri == ci, jnp.broadcast_to(score_row, (nb, nb)), 0.0),
                            axis=1, keepdims=True)
        beats = (score_col > score_row) | ((score_col == score_row) & (blk_r < blk_c))
        rank = jnp.sum(beats.astype(I32), axis=0, keepdims=True)
        sel_ref[0, g:g + 1, :] = (rank < nsel).astype(F32)
    o_ref[0] = o_all[:NSA_H]


def _dec_cmp(page_table, cache2, pe_dec, w1dec, w2dec, q3, nsel):
    bsz, npages = page_table.shape
    nb = npages * (PAGE // NSA_BLK)
    nrow = npages * CMP_PLANES
    grid_spec = pltpu.PrefetchScalarGridSpec(
        num_scalar_prefetch=1,
        grid=(bsz,),
        in_specs=[pl.BlockSpec(memory_space=pl.ANY),
                  pl.BlockSpec(pe_dec.shape, lambda b, pt: (0, 0, 0)),
                  pl.BlockSpec(w1dec.shape, lambda b, pt: (0, 0, 0)),
                  pl.BlockSpec(w2dec.shape, lambda b, pt: (0, 0)),
                  pl.BlockSpec((1, NSA_H, NSA_DH), lambda b, pt: (b, 0, 0))],
        out_specs=[pl.BlockSpec((1, NSA_H, NSA_DH), lambda b, pt: (b, 0, 0)),
                   pl.BlockSpec((1, NSA_G, nb), lambda b, pt: (b, 0, 0))],
        scratch_shapes=[pltpu.VMEM((nrow * NSA_DH, LANES), F32), pltpu.VMEM((nrow, 2 * LANES), F32),
                        pltpu.VMEM((nrow, LANES), F32), pltpu.SemaphoreType.DMA((1,))])
    return pl.pallas_call(
        functools.partial(_dec_cmp_kernel, npages=npages, nsel=nsel),
        grid_spec=grid_spec,
        out_shape=[jax.ShapeDtypeStruct((bsz, NSA_H, NSA_DH), F32),
                   jax.ShapeDtypeStruct((bsz, NSA_G, nb), F32)],
        compiler_params=_cparams(("arbitrary",)),
        name="nsa_dec_cmp",
    )(page_table.reshape(-1), cache2, pe_dec, w1dec, w2dec, q3)


def _attend_with_self(q16, slope, kt, vt, dist, valid, k_self, v_self):
    s = jnp.dot(q16, kt.astype(BF16), preferred_element_type=F32) - slope * dist
    s_self = jnp.sum(q16.astype(F32) * k_self.astype(BF16).astype(F32), axis=1, keepdims=True)
    m = jnp.maximum(jnp.max(jnp.where(valid, s, NEG), axis=1, keepdims=True), s_self)
    p = jnp.where(valid, jnp.exp(s - m), 0.0)
    p_self = jnp.exp(s_self - m)
    num = lax.dot_general(p.astype(BF16), vt.astype(BF16), NT_DIMS, preferred_element_type=F32) + p_self * v_self
    return num / jnp.maximum(jnp.sum(p, axis=1, keepdims=True) + p_self, TINY)


def _dec_attn_kernel(pt_ref, idx_ref, cache_ref, q_ref, oc_ref, kn_ref, win_ref, gt_ref, o_ref, kbuf, vbuf, sem,
                     *, npages, nsel, wlen):
    b = pl.program_id(0)
    pos = npages * PAGE

    copies = []
    for g in range(NSA_G):
        for j in range(nsel):
            blk = idx_ref[(b * NSA_G + g) * NSA_TOPK + j]
            base = pt_ref[b * npages + (blk >> 1)] * PAGE_ROWS
            dst = pl.ds(j * PAGE, PAGE)
            k_rows = pl.ds(pl.multiple_of(base + (2 * NSA_G + g) * NSA_DH, NSA_DH), NSA_DH)
            v_rows = pl.ds(pl.multiple_of(base + (3 * NSA_G + g) * NSA_DH, NSA_DH), NSA_DH)
            copies.append(pltpu.make_async_copy(cache_ref.at[k_rows, :], kbuf.at[g, :, dst], sem.at[0]))
            copies.append(pltpu.make_async_copy(cache_ref.at[v_rows, :], vbuf.at[g, :, dst], sem.at[0]))
    for cp in copies:
        cp.start()
    for cp in copies:
        cp.wait()

    q16 = _padded_queries(q_ref)
    slope = _head_slopes(QROWS)
    row_grp = _iota((QROWS, 1), 0) >> 2
    kn = kn_ref[0]
    lane = _iota((1, nsel * PAGE), 1)
    within = lane & (PAGE - 1)
    wdist = wlen - _iota((1, wlen), 1)
    o_slc = jnp.zeros((QROWS, NSA_DH), F32)
    o_win = jnp.zeros((QROWS, NSA_DH), F32)
    for g in range(NSA_G):
        gs = slice(g * NSA_DH, (g + 1) * NSA_DH)
        blk_lane = jnp.zeros((1, nsel * PAGE), I32)
        for j in range(nsel):
            blk_lane = jnp.where((lane >> 7) == j, idx_ref[(b * NSA_G + g) * NSA_TOPK + j], blk_lane)
        valid = (within >> 6) == (blk_lane & 1)
        kpos = (blk_lane >> 1) * PAGE + within
        o_g = _attend_with_self(q16, slope, kbuf[g], vbuf[g], (pos - kpos).astype(F32), valid,
                                kn[0:1, gs], kn[1:2, gs])
        o_slc = jnp.where(row_grp == g, o_g, o_slc)
        o_g = _attend_with_self(q16, slope, win_ref[0, g * NSA_DH:(g + 1) * NSA_DH, :],
                                win_ref[0, (NSA_G + g) * NSA_DH:(NSA_G + g + 1) * NSA_DH, :],
                                wdist.astype(F32), wdist < NSA_WINDOW, kn[2:3, gs], kn[3:4, gs])
        o_win = jnp.where(row_grp == g, o_g, o_win)

    gates = _sigmoid(gt_ref[pl.ds(b, 1), :])
    r_io = _iota((QROWS, LANES), 0)
    l_io = _iota((QROWS, LANES), 1)

    def gate_col(j):
        return jnp.sum(jnp.where(l_io == 3 * r_io + j, gates, 0.0), axis=1, keepdims=True)

    o_cmp = jnp.concatenate([oc_ref[0], jnp.zeros((QROWS - NSA_H, NSA_DH), F32)], axis=0)
    o = gate_col(0) * o_cmp + gate_col(1) * o_slc + gate_col(2) * o_win
    o_ref[0] = o[:NSA_H]


def _dec_attn(page_table, idx, cache2, q3, ocmp, knew, win_t, gates, nsel):
    bsz, npages = page_table.shape
    wlen = win_t.shape[2]
    hd = pl.BlockSpec((1, NSA_H, NSA_DH), lambda b, pt, ix: (b, 0, 0))
    grid_spec = pltpu.PrefetchScalarGridSpec(
        num_scalar_prefetch=2,
        grid=(bsz,),
        in_specs=[pl.BlockSpec(memory_space=pl.ANY), hd, hd,
                  pl.BlockSpec((1,) + knew.shape[1:], lambda b, pt, ix: (b, 0, 0)),
                  pl.BlockSpec((1,) + win_t.shape[1:], lambda b, pt, ix: (b, 0, 0)),
                  pl.BlockSpec(gates.shape, lambda b, pt, ix: (0, 0))],
        out_specs=hd,
        scratch_shapes=[pltpu.VMEM((NSA_G, NSA_DH, nsel * PAGE), F32), pltpu.VMEM((NSA_G, NSA_DH, nsel * PAGE), F32),
                        pltpu.SemaphoreType.DMA((1,))])
    return pl.pallas_call(
        functools.partial(_dec_attn_kernel, npages=npages, nsel=nsel, wlen=wlen),
        grid_spec=grid_spec,
        out_shape=jax.ShapeDtypeStruct((bsz, NSA_H, NSA_DH), F32),
        compiler_params=_cparams(("arbitrary",)),
        name="nsa_dec_attn",
    )(page_table.reshape(-1), idx.reshape(-1), cache2, q3, ocmp, knew, win_t, gates)


def _prep_weights(w_in_even, nsa_cmp_pos, nsa_cmp_w1, nsa_cmp_w2, w_out_even, w_in_odd, gla_wa2, w_out_odd,
                  ffn_w13, ffn_w2):
    d = D_MODEL
    wie = jnp.pad(w_in_even[0], ((0, 0), (0, PROJ_W - w_in_even.shape[2]))).astype(BF16)
    a0 = 2 * GLA_H * GLA_DK + GLA_H * GLA_DV
    wo = w_in_odd[0]
    wio = jnp.concatenate([wo[:, :a0], wo[:, a0 + GLA_RANK:], wo[:, a0:a0 + GLA_RANK]], axis=1)
    wio = jnp.pad(wio, ((0, 0), (0, ODD_W - wio.shape[1]))).astype(BF16)
    wa2p = jnp.pad(gla_wa2[0], ((0, LANES - GLA_RANK), (0, 0))).astype(BF16)
    hv = HG_H * HG_DV
    woe_h, woe_n = w_out_even[0, :hv].astype(BF16), w_out_even[0, hv:].astype(BF16)
    woo = w_out_odd[0].astype(BF16)
    nj = D_FF // FF_TILE
    w13t, w2b = [], []
    for l in range(ffn_w13.shape[0]):
        gate = ffn_w13[l, :, :D_FF].reshape(d, nj, FF_TILE)
        up = ffn_w13[l, :, D_FF:].reshape(d, nj, FF_TILE)
        w13t.append(jnp.concatenate([gate, up], axis=2).reshape(d, 2 * D_FF).astype(BF16))
        w2b.append(ffn_w2[l].astype(BF16))
    w1 = nsa_cmp_w1[0].reshape(2, NSA_BLK, NSA_DH, NSA_DH)
    w2 = nsa_cmp_w2[0]
    cw = 4 * NSA_DH
    eye = jnp.eye(2 * NSA_G, dtype=F32).reshape(2, NSA_G, 2, NSA_G)
    w1bd = jnp.einsum("cldj,cgCG->lcgdCGj", w1, eye).reshape(NSA_BLK, cw, cw)
    w2bd = jnp.einsum("cjd,cgCG->cgjCGd", w2, eye).reshape(cw, cw)
    pe = nsa_cmp_pos[0]
    pe4 = jnp.concatenate([pe[0], pe[0], pe[1], pe[1]], axis=1)
    we = w_in_even[0]
    hgw = 2 * HG_H * HG_DK + 2 * HG_H * HG_DV
    nq0 = hgw + NSA_H * NSA_DH
    w_main = jnp.concatenate([we[:, :hgw], we[:, nq0:nq0 + cw]], axis=1).astype(BF16)
    w_tr = jnp.pad(we[:, hgw:].T, ((0, T_END - (we.shape[1] - hgw)), (0, 0))).astype(BF16)
    eye2 = jnp.eye(2, dtype=F32)
    w1dec = jnp.einsum("cldo,jJ->djlcJo", w1, eye2).reshape(NSA_DH, 2 * NSA_BLK, 4 * NSA_DH)
    w2dec = jnp.einsum("cod,jJ->jocJd", w2, eye2).reshape(2 * NSA_DH, 4 * NSA_DH)
    pe_dec = jnp.tile(pe.transpose(2, 0, 1), (1, 1, 2))
    return dict(wie=wie, wio=wio, wa2p=wa2p, woe_h=woe_h, woe_n=woe_n, woo=woo, w13t=w13t, w2b=w2b,
                w1bd=w1bd.astype(BF16), w2bd=w2bd.astype(BF16), pe4=pe4, w_main=w_main, w_tr=w_tr,
                w1dec=w1dec.astype(BF16), w2dec=w2dec.astype(BF16), pe_dec=pe_dec)


def _forward_prompt(x, pw, hg_lb_logits, hg_norm, gla_ba, gla_norm, norm_mix, norm_ffn, norm_final):
    bsz, seq, d = x.shape
    n = bsz * seq
    x2d = x.reshape(n, d)
    tm = 512 if seq % 512 == 0 else 256
    tmf = next(t for t in (1024, 512, 256) if n % t == 0)
    tb = tm

    p, qt, rows_t, win_t, gate_t = _norm_proj_t(x2d, norm_mix[0], pw["w_main"], pw["w_tr"], bsz, seq, tm)
    o_h, hg_state = _hgrn_prompt(p, hg_lb_logits, hg_norm[0], bsz, seq, tb)
    cmp = _compress_prompt(p, pw["pe4"], pw["w1bd"], pw["w2bd"])
    nblk = seq // NSA_BLK
    cmp4 = cmp.reshape(bsz, nblk, 2 * NSA_G, NSA_DH).transpose(0, 2, 1, 3)
    kc = cmp4[:, :NSA_G].astype(BF16)
    vct = cmp4[:, NSA_G:].transpose(0, 1, 3, 2).astype(BF16)
    o_n = _nsa_attn_prompt(qt, rows_t, win_t, gate_t, kc, vct, bsz, seq)
    x1 = _out_ffn([o_h, o_n], x2d, [pw["woe_h"], pw["woe_n"]], norm_ffn[0], pw["w13t"][0], pw["w2b"][0], None, tmf)

    rows = rows_t.reshape(bsz, 4, NSA_G, NSA_DH, seq).transpose(0, 4, 1, 2, 3)[None]
    wn = min(NSA_WINDOW, seq)
    win = win_t[:, :, seq - wn:].reshape(bsz, 2, NSA_G, NSA_DH, wn).transpose(0, 4, 1, 2, 3)[None]

    p2 = _norm_proj(x1, norm_mix[1], pw["wio"], tm, ODD_W)
    o_g, gla_state = _gla_prompt(p2, pw["wa2p"], gla_ba[0], gla_norm[0], bsz, seq, tb)
    y = _out_ffn([o_g], x1, [pw["woo"]], norm_ffn[1], pw["w13t"][1], pw["w2b"][1], norm_final, tmf)
    return y.reshape(bsz, seq, d), rows, win, hg_state[None], gla_state[None]


def _forward_sample(x, pw, cache_nsa_kv, cache_win_kv, state_hgrn, state_gla, page_table, hg_lb_logits, hg_norm,
                    gla_ba, gla_norm, norm_mix, norm_ffn, norm_final):
    bsz, seq, d = x.shape
    x2d = x.reshape(bsz, d)
    npages = page_table.shape[1]
    nblk_total = npages * (PAGE // NSA_BLK) + 1
    nsel = min(NSA_TOPK, nblk_total) - 1

    p = _norm_proj(x2d, norm_mix[0], pw["wie"], bsz, PROJ_W // 2)
    o_h, hg_state = _hgrn_decode(p, hg_lb_logits, hg_norm[0], state_hgrn[0])

    cache2 = cache_nsa_kv[0].transpose(0, 2, 3, 4, 1).reshape(cache_nsa_kv.shape[1] * PAGE_ROWS, PAGE)
    q3 = p[:, EV_NQ:EV_CMP].reshape(bsz, NSA_H, NSA_DH)
    ocmp, sel = _dec_cmp(page_table, cache2, pw["pe_dec"], pw["w1dec"], pw["w2dec"], q3, nsel)
    sel = sel.reshape(bsz, NSA_G, 2, npages).transpose(0, 1, 3, 2).reshape(bsz, NSA_G, 2 * npages)
    idx = jnp.argsort(-sel, axis=-1, stable=True)[..., :NSA_TOPK].astype(I32)
    knew = p[:, EV_SLC:EV_GATE].reshape(bsz, 4, NSA_G * NSA_DH)
    wlen = cache_win_kv.shape[2]
    win_t = cache_win_kv[0].transpose(0, 2, 3, 4, 1).reshape(bsz, 2 * NSA_G * NSA_DH, wlen)
    o_n = _dec_attn(page_table, idx, cache2, q3, ocmp, knew, win_t, p[:, EV_GATE:EV_GATE + LANES], nsel)
    o_n = o_n.reshape(bsz, NSA_H * NSA_DH)
    x1 = _out_ffn([o_h, o_n], x2d, [pw["woe_h"], pw["woe_n"]], norm_ffn[0], pw["w13t"][0], pw["w2b"][0], None, bsz)

    rows = p[:, EV_CMP:EV_WIN].reshape(1, bsz, 1, 4, NSA_G, NSA_DH)
    win_new = p[:, EV_WIN:EV_GATE].reshape(1, bsz, 1, 2, NSA_G, NSA_DH)
    wk = jnp.concatenate([cache_win_kv[:1], win_new], axis=2)
    wn = min(NSA_WINDOW, npages * PAGE + 1)
    win = wk[:, :, wk.shape[2] - wn:]

    p2 = _norm_proj(x1, norm_mix[1], pw["wio"], bsz, ODD_W)
    o_g, gla_state = _gla_decode(p2, pw["wa2p"], gla_ba[0], gla_norm[0], state_gla[0])
    y = _out_ffn([o_g], x1, [pw["woo"]], norm_ffn[1], pw["w13t"][1], pw["w2b"][1], norm_final, bsz)
    return y.reshape(bsz, 1, d), rows, win, hg_state[None], gla_state[None]


def kernel(x_prompt, x_sample, cache_nsa_kv, cache_win_kv, state_hgrn, state_gla, page_table, w_in_even,
           hg_lb_logits, hg_norm, nsa_cmp_pos, nsa_cmp_w1, nsa_cmp_w2, w_out_even, w_in_odd, gla_wa2, gla_ba,
           gla_norm, w_out_odd, norm_mix, norm_ffn, norm_final, ffn_w13, ffn_w2):
    pw = _prep_weights(w_in_even, nsa_cmp_pos, nsa_cmp_w1, nsa_cmp_w2, w_out_even, w_in_odd, gla_wa2, w_out_odd,
                       ffn_w13, ffn_w2)
    y_p, kv_p, win_p, hg_p, gla_p = _forward_prompt(x_prompt, pw, hg_lb_logits, hg_norm, gla_ba, gla_norm,
                                                    norm_mix, norm_ffn, norm_final)
    y_s, kv_s, win_s, hg_s, gla_s = _forward_sample(x_sample, pw, cache_nsa_kv, cache_win_kv, state_hgrn,
                                                    state_gla, page_table, hg_lb_logits, hg_norm, gla_ba,
                                                    gla_norm, norm_mix, norm_ffn, norm_final)
    return (y_p, y_s, kv_p, kv_s, win_p, win_s, hg_p, hg_s, gla_p, gla_s)
```

```python
import functools

import jax
import jax.numpy as jnp
from jax import lax
from jax.experimental import pallas as pl
from jax.experimental.pallas import tpu as pltpu

F32 = jnp.float32
BF16 = jnp.bfloat16
I32 = jnp.int32

D_MODEL = 1024
HG_H, HG_DK, HG_DV = 4, 128, 128
NSA_H, NSA_DH, NSA_G, NSA_R = 8, 64, 2, 4
NSA_BLK = 64
NSA_TOPK = 16
NSA_WINDOW = 512
NSA_QB = 128
GLA_H, GLA_DK, GLA_DV = 4, 128, 256
GLA_RANK = 16
GLA_TAU = 16.0
D_FF = 2816
EPS = 1e-6
NEG = -1e30
TINY = 1e-30
PAGE = 128

PROJ_W = 3584
ODD_W = 3200
EV_Q, EV_F, EV_I, EV_OG = 0, 512, 1024, 1536
EV_NQ, EV_CMP, EV_SLC, EV_WIN, EV_GATE = 2048, 2560, 2816, 3072, 3328
OD_Q, OD_K, OD_V, OD_R, OD_A = 0, 512, 1024, 2048, 3072

LANES = 128
V7X_VMEM_LIMIT = 56 * 1024 * 1024
FF_TILE = 256
CHUNK = 128
SUB = 16

NT_DIMS = (((1,), (1,)), ((), ()))
TN_DIMS = (((0,), (0,)), ((), ()))


def _cparams(sem):
    return pltpu.CompilerParams(dimension_semantics=sem, vmem_limit_bytes=V7X_VMEM_LIMIT)


def _rms(x, g):
    return x * lax.rsqrt(jnp.mean(x * x, axis=-1, keepdims=True) + EPS) * g


def _sigmoid(x):
    return 0.5 * jnp.tanh(0.5 * x) + 0.5


def _silu(x):
    return x * _sigmoid(x)


def _iota(shape, dim):
    return lax.broadcasted_iota(I32, shape, dim)


def _norm_proj_kernel(x_ref, g_ref, w_ref, o_ref, xn_ref):
    @pl.when(pl.program_id(1) == 0)
    def _():
        xn_ref[...] = _rms(x_ref[...], g_ref[...]).astype(BF16)

    o_ref[...] = jnp.dot(xn_ref[...], w_ref[...], preferred_element_type=F32)


def _norm_proj(x2d, gain, w, tm, tn):
    m, k = x2d.shape
    n = w.shape[1]
    return pl.pallas_call(
        _norm_proj_kernel,
        grid=(m // tm, n // tn),
        in_specs=[pl.BlockSpec((tm, k), lambda i, j: (i, 0)),
                  pl.BlockSpec((1, k), lambda i, j: (0, 0)),
                  pl.BlockSpec((k, tn), lambda i, j: (0, j))],
        out_specs=pl.BlockSpec((tm, tn), lambda i, j: (i, j)),
        out_shape=jax.ShapeDtypeStruct((m, n), F32),
        scratch_shapes=[pltpu.VMEM((tm, k), BF16)],
        compiler_params=_cparams(("parallel", "arbitrary")),
        name="norm_proj",
    )(x2d, gain.reshape(1, k), w)


PM_W = 2304
PM_CMP = 2048
T_Q, T_ROWS, T_WIN, T_GATE, T_END = 0, 512, 1024, 1280, 1312


def _norm_proj_t_kernel(x_ref, g_ref, w_ref, wt_ref, o_ref, q_ref, rows_ref, win_ref, gate_ref):
    xn = _rms(x_ref[...], g_ref[...]).astype(BF16)
    o_ref[...] = jnp.dot(xn, w_ref[...], preferred_element_type=F32)
    t = lax.dot_general(wt_ref[...], xn, NT_DIMS, preferred_element_type=F32)
    q_ref[0] = t[T_Q:T_ROWS]
    rows_ref[0] = t[T_ROWS:T_WIN]
    win_ref[0] = t[T_WIN:T_GATE]
    gate_ref[0] = t[T_GATE:T_END]


def _norm_proj_t(x2d, gain, w, wt, bsz, seq, tm):
    k = x2d.shape[1]
    nt = seq // tm
    tr = lambda rows: pl.BlockSpec((1, rows, tm), lambda b, t: (b, 0, t))
    sizes = (T_ROWS - T_Q, T_WIN - T_ROWS, T_GATE - T_WIN, T_END - T_GATE)
    return pl.pallas_call(
        _norm_proj_t_kernel,
        grid=(bsz, nt),
        in_specs=[pl.BlockSpec((tm, k), lambda b, t: (b * nt + t, 0)),
                  pl.BlockSpec((1, k), lambda b, t: (0, 0)),
                  pl.BlockSpec(w.shape, lambda b, t: (0, 0)),
                  pl.BlockSpec(wt.shape, lambda b, t: (0, 0))],
        out_specs=[pl.BlockSpec((tm, PM_W), lambda b, t: (b * nt + t, 0))] + [tr(r) for r in sizes],
        out_shape=[jax.ShapeDtypeStruct((bsz * seq, PM_W), F32)]
        + [jax.ShapeDtypeStruct((bsz, r, seq), F32) for r in sizes],
        compiler_params=_cparams(("parallel", "arbitrary")),
        name="norm_proj_t",
    )(x2d, gain.reshape(1, k), w, wt)


def _out_ffn_kernel(*refs, n_mix, final_norm):
    mix_refs = refs[:n_mix]
    res_ref = refs[n_mix]
    wo_refs = refs[n_mix + 1:2 * n_mix + 1]
    g_ref, w13_ref, w2_ref = refs[2 * n_mix + 1:2 * n_mix + 4]
    pos = 2 * n_mix + 4
    gf_ref = refs[pos] if final_norm else None
    pos += 1 if final_norm else 0
    o_ref, x1_ref, h_ref, acc_ref = refs[pos:pos + 4]
    j = pl.program_id(1)

    @pl.when(j == 0)
    def _():
        x1 = res_ref[...]
        for m_ref, w_ref in zip(mix_refs, wo_refs):
            x1 = x1 + jnp.dot(m_ref[...].astype(BF16), w_ref[...], preferred_element_type=F32)
        x1_ref[...] = x1
        h_ref[...] = _rms(x1, g_ref[...]).astype(BF16)
        acc_ref[...] = jnp.zeros_like(acc_ref)

    gu = jnp.dot(h_ref[...], w13_ref[...], preferred_element_type=F32)
    act = _silu(gu[:, :FF_TILE]) * gu[:, FF_TILE:]
    acc_ref[...] += jnp.dot(act.astype(BF16), w2_ref[...], preferred_element_type=F32)

    @pl.when(j == pl.num_programs(1) - 1)
    def _():
        y = x1_ref[...] + acc_ref[...]
        if final_norm:
            y = _rms(y, gf_ref[...])
        o_ref[...] = y


def _out_ffn(mixes, res, wos, g_ffn, w13t, w2, g_final, tm):
    m, d = res.shape
    n_mix = len(mixes)
    nj = D_FF // FF_TILE
    final_norm = g_final is not None
    in_specs = [pl.BlockSpec((tm, mx.shape[1]), lambda i, j: (i, 0)) for mx in mixes]
    in_specs.append(pl.BlockSpec((tm, d), lambda i, j: (i, 0)))
    in_specs += [pl.BlockSpec(w.shape, lambda i, j: (0, 0)) for w in wos]
    in_specs += [pl.BlockSpec((1, d), lambda i, j: (0, 0)),
                 pl.BlockSpec((d, 2 * FF_TILE), lambda i, j: (0, j)),
                 pl.BlockSpec((FF_TILE, d), lambda i, j: (j, 0))]
    args = list(mixes) + [res] + list(wos) + [g_ffn.reshape(1, d), w13t, w2]
    if final_norm:
        in_specs.append(pl.BlockSpec((1, d), lambda i, j: (0, 0)))
        args.append(g_final.reshape(1, d))
    return pl.pallas_call(
        functools.partial(_out_ffn_kernel, n_mix=n_mix, final_norm=final_norm),
        grid=(m // tm, nj),
        in_specs=in_specs,
        out_specs=pl.BlockSpec((tm, d), lambda i, j: (i, 0)),
        out_shape=jax.ShapeDtypeStruct((m, d), F32),
        scratch_shapes=[pltpu.VMEM((tm, d), F32), pltpu.VMEM((tm, d), BF16), pltpu.VMEM((tm, d), F32)],
        compiler_params=_cparams(("parallel", "arbitrary")),
        name="out_ffn",
    )(*args)


def _gla_chunk(q, k, v, g, st):
    c = q.shape[0]
    tri = (_iota((c, c), 0) >= _iota((c, c), 1)).astype(BF16)
    g_hi = g.astype(BF16)
    g_r1 = g - g_hi.astype(F32)
    g_mid = g_r1.astype(BF16)
    g_lo = (g_r1 - g_mid.astype(F32)).astype(BF16)
    b = (jnp.dot(tri, g_hi, preferred_element_type=F32) + jnp.dot(tri, g_mid, preferred_element_type=F32)
         + jnp.dot(tri, g_lo, preferred_element_type=F32))
    o = lax.dot_general((q * jnp.exp(b)).astype(BF16), st.astype(BF16), NT_DIMS, preferred_element_type=F32)
    lane = _iota((SUB, c), 1)
    row = _iota((SUB, c), 0)
    hs = SUB // 2
    rows = []
    for blk in range(c // SUB):
        lo = blk * SUB
        b_i, q_i, k_i = b[lo:lo + SUB], q[lo:lo + SUB], k[lo:lo + SUB]
        a_top = jnp.zeros((hs, c), F32)
        a_bot = jnp.zeros((hs, c), F32)
        for s in range(SUB):
            if s < hs:
                e = jnp.exp(b_i - b_i[s:s + 1])
                a = jnp.sum(q_i * e * k_i[s:s + 1], axis=1, keepdims=True)
                a_top = jnp.where(lane[:hs] == lo + s, a[:hs], a_top)
                a_bot = jnp.where(lane[:hs] == lo + s, a[hs:], a_bot)
            else:
                e = jnp.exp(b_i[hs:] - b_i[s:s + 1])
                a = jnp.sum(q_i[hs:] * e * k_i[s:s + 1], axis=1, keepdims=True)
                a_bot = jnp.where(lane[:hs] == lo + s, a, a_bot)
        a_blk = jnp.where(lane <= lo + row, jnp.concatenate([a_top, a_bot], axis=0), 0.0)
        if blk > 0:
            r = b[lo - 1:lo]
            qt = q_i * jnp.exp(b_i - r)
            kt = k * jnp.exp(jnp.minimum(r - b, 0.0))
            a_off = lax.dot_general(qt.astype(BF16), kt.astype(BF16), NT_DIMS, preferred_element_type=F32)
            a_blk = jnp.where(lane < lo, a_off, a_blk)
        rows.append(a_blk)
    a_full = jnp.concatenate(rows, axis=0)
    o = o + jnp.dot(a_full.astype(BF16), v.astype(BF16), preferred_element_type=F32)
    b_last = b[c - 1:c]
    kd = k * jnp.exp(b_last - b)
    st_new = st * jnp.exp(b_last) + lax.dot_general(v.astype(BF16), kd.astype(BF16), TN_DIMS,
                                                    preferred_element_type=F32)
    return o, st_new


def _lower_bound(lbl_ref, col):
    l = lbl_ref[:, col:col + HG_DK]
    e = jnp.exp(l - jnp.max(l, axis=0, keepdims=True))
    return e[0:1] / jnp.sum(e, axis=0, keepdims=True)


def _hgrn_kernel(q_ref, f_ref, i_ref, og_ref, lbl_ref, gain_ref, o_ref, s_ref, st_ref, *, nch):
    t = pl.program_id(1)

    @pl.when(t == 0)
    def _():
        st_ref[...] = jnp.zeros_like(st_ref)

    def body(ci, carry):
        r0 = pl.multiple_of(ci * CHUNK, CHUNK)
        rs = pl.ds(r0, CHUNK)
        for h in range(HG_H):
            cs = slice(h * HG_DK, (h + 1) * HG_DK)
            lb = _lower_bound(lbl_ref, h * HG_DK)
            q = _silu(q_ref[rs, cs])
            f = lb + (1.0 - lb) * _sigmoid(f_ref[rs, cs])
            o, st_new = _gla_chunk(q, 1.0 - f, i_ref[rs, cs], jnp.log(f), st_ref[h])
            st_ref[h] = st_new
            o = _rms(o, gain_ref[...]) * _sigmoid(og_ref[rs, cs])
            o_ref[rs, cs] = o.astype(o_ref.dtype)
        return carry

    lax.fori_loop(0, nch, body, 0)

    @pl.when(t == pl.num_programs(1) - 1)
    def _():
        for h in range(HG_H):
            s_ref[0, h] = st_ref[h].T


def _hgrn_prompt(p2d, lb_logits, gain, bsz, seq, tb):
    nt = seq // tb
    hk = HG_H * HG_DK
    row = lambda b, t: b * nt + t
    col_spec = lambda c: pl.BlockSpec((tb, hk), lambda b, t, c=c: (row(b, t), c))
    return pl.pallas_call(
        functools.partial(_hgrn_kernel, nch=tb // CHUNK),
        grid=(bsz, nt),
        in_specs=[col_spec(EV_Q // hk), col_spec(EV_F // hk), col_spec(EV_I // hk), col_spec(EV_OG // hk),
                  pl.BlockSpec(lb_logits.shape, lambda b, t: (0, 0)),
                  pl.BlockSpec((1, HG_DV), lambda b, t: (0, 0))],
        out_specs=[pl.BlockSpec((tb, hk), lambda b, t: (row(b, t), 0)),
                   pl.BlockSpec((1, HG_H, HG_DK, HG_DV), lambda b, t: (b, 0, 0, 0))],
        out_shape=[jax.ShapeDtypeStruct((bsz * seq, hk), BF16),
                   jax.ShapeDtypeStruct((bsz, HG_H, HG_DK, HG_DV), F32)],
        scratch_shapes=[pltpu.VMEM((HG_H, HG_DV, HG_DK), F32)],
        compiler_params=_cparams(("parallel", "arbitrary")),
        name="hgrn_chunk",
    )(p2d, p2d, p2d, p2d, lb_logits, gain.reshape(1, HG_DV))


def _log_sigmoid(x):
    return jnp.minimum(x, 0.0) - jnp.log(1.0 + jnp.exp(-jnp.abs(x)))


def _gla_kernel(q_ref, k_ref, v_ref, r_ref, a_ref, wa_ref, ba_ref, gain_ref, o_ref, s_ref, st_ref, *, nch):
    t = pl.program_id(1)

    @pl.when(t == 0)
    def _():
        st_ref[...] = jnp.zeros_like(st_ref)

    def body(ci, carry):
        r0 = pl.multiple_of(ci * CHUNK, CHUNK)
        rs = pl.ds(r0, CHUNK)
        gate = jnp.dot(a_ref[rs, :].astype(BF16), wa_ref[...], preferred_element_type=F32) + ba_ref[...]
        logf = _log_sigmoid(gate) * (1.0 / GLA_TAU)
        for h in range(GLA_H):
            ks = slice(h * GLA_DK, (h + 1) * GLA_DK)
            vs = slice(h * GLA_DV, (h + 1) * GLA_DV)
            q = q_ref[rs, ks] * (GLA_DK ** -0.5)
            o, st_new = _gla_chunk(q, k_ref[rs, ks], v_ref[rs, vs], logf[:, ks], st_ref[h])
            st_ref[h] = st_new
            o = _rms(o, gain_ref[...]) * _silu(r_ref[rs, vs])
            o_ref[rs, vs] = o.astype(o_ref.dtype)
        return carry

    lax.fori_loop(0, nch, body, 0)

    @pl.when(t == pl.num_programs(1) - 1)
    def _():
        for h in range(GLA_H):
            s_ref[0, h] = st_ref[h].T


def _gla_prompt(p2d, wa2p, ba, gain, bsz, seq, tb):
    nt = seq // tb
    hk, hv = GLA_H * GLA_DK, GLA_H * GLA_DV
    row = lambda b, t: b * nt + t
    return pl.pallas_call(
        functools.partial(_gla_kernel, nch=tb // CHUNK),
        grid=(bsz, nt),
        in_specs=[pl.BlockSpec((tb, hk), lambda b, t: (row(b, t), OD_Q // hk)),
                  pl.BlockSpec((tb, hk), lambda b, t: (row(b, t), OD_K // hk)),
                  pl.BlockSpec((tb, hv), lambda b, t: (row(b, t), OD_V // hv)),
                  pl.BlockSpec((tb, hv), lambda b, t: (row(b, t), OD_R // hv)),
                  pl.BlockSpec((tb, LANES), lambda b, t: (row(b, t), OD_A // LANES)),
                  pl.BlockSpec(wa2p.shape, lambda b, t: (0, 0)),
                  pl.BlockSpec((1, hk), lambda b, t: (0, 0)),
                  pl.BlockSpec((1, GLA_DV), lambda b, t: (0, 0))],
        out_specs=[pl.BlockSpec((tb, hv), lambda b, t: (row(b, t), 0)),
                   pl.BlockSpec((1, GLA_H, GLA_DK, GLA_DV), lambda b, t: (b, 0, 0, 0))],
        out_shape=[jax.ShapeDtypeStruct((bsz * seq, hv), BF16),
                   jax.ShapeDtypeStruct((bsz, GLA_H, GLA_DK, GLA_DV), F32)],
        scratch_shapes=[pltpu.VMEM((GLA_H, GLA_DV, GLA_DK), F32)],
        compiler_params=_cparams(("parallel", "arbitrary")),
        name="gla_chunk",
    )(p2d, p2d, p2d, p2d, p2d, wa2p, ba.reshape(1, hk), gain.reshape(1, GLA_DV))


def _to_columns(x):
    bsz = x.shape[0]
    if bsz < LANES:
        x = jnp.concatenate([x, jnp.zeros((LANES - bsz, x.shape[1]), x.dtype)], axis=0)
    return x.T


def _decode_update(q, k, v, g, s_ref, so_ref, o_scr):
    bsz = q.shape[0]
    qt, kt, et = _to_columns(q), _to_columns(k), _to_columns(jnp.exp(g))
    for b in range(bsz):
        s_new = et[:, b:b + 1] * s_ref[b, 0] + kt[:, b:b + 1] * v[b:b + 1, :]
        so_ref[b, 0] = s_new
        o_scr[b:b + 1, :] = jnp.sum(qt[:, b:b + 1] * s_new, axis=0, keepdims=True)


def _hgrn_decode_kernel(q_ref, f_ref, i_ref, og_ref, lbl_ref, gain_ref, s_ref, o_ref, so_ref, o_scr):
    l = lbl_ref[0]
    e = jnp.exp(l - jnp.max(l, axis=0, keepdims=True))
    lb = e[0:1] / jnp.sum(e, axis=0, keepdims=True)
    f = lb + (1.0 - lb) * _sigmoid(f_ref[...])
    _decode_update(_silu(q_ref[...]), 1.0 - f, i_ref[...], jnp.log(f), s_ref, so_ref, o_scr)
    o_ref[...] = _rms(o_scr[...], gain_ref[...]) * _sigmoid(og_ref[...])


def _hgrn_decode(p2d, lb_logits, gain, state):
    bsz = p2d.shape[0]
    col = lambda c: pl.BlockSpec((bsz, HG_DK), lambda h, c=c: (0, c + h))
    lbl3 = lb_logits.reshape(lb_logits.shape[0], HG_H, HG_DK).transpose(1, 0, 2)
    st_spec = pl.BlockSpec((bsz, 1, HG_DK, HG_DV), lambda h: (0, h, 0, 0))
    return pl.pallas_call(
        _hgrn_decode_kernel,
        grid=(HG_H,),
        in_specs=[col(EV_Q // HG_DK), col(EV_F // HG_DK), col(EV_I // HG_DK), col(EV_OG // HG_DK),
                  pl.BlockSpec((1,) + lbl3.shape[1:], lambda h: (h, 0, 0)),
                  pl.BlockSpec((1, HG_DV), lambda h: (0, 0)),
                  st_spec],
        out_specs=[pl.BlockSpec((bsz, HG_DV), lambda h: (0, h)), st_spec],
        out_shape=[jax.ShapeDtypeStruct((bsz, HG_H * HG_DV), F32),
                   jax.ShapeDtypeStruct(state.shape, F32)],
        scratch_shapes=[pltpu.VMEM((bsz, HG_DV), F32)],
        compiler_params=_cparams(("arbitrary",)),
        name="hgrn_decode",
    )(p2d, p2d, p2d, p2d, lbl3, gain.reshape(1, HG_DV), state)


def _gla_decode_kernel(q_ref, k_ref, v_ref, r_ref, a_ref, wa_ref, ba_ref, gain_ref, s_ref, o_ref, so_ref, o_scr):
    gate = jnp.dot(a_ref[...].astype(BF16), wa_ref[...], preferred_element_type=F32) + ba_ref[...]
    logf = _log_sigmoid(gate) * (1.0 / GLA_TAU)
    _decode_update(q_ref[...] * (GLA_DK ** -0.5), k_ref[...], v_ref[...], logf, s_ref, so_ref, o_scr)
    o_ref[...] = _rms(o_scr[...], gain_ref[...]) * _silu(r_ref[...])


def _gla_decode(p2d, wa2p, ba, gain, state):
    bsz = p2d.shape[0]
    hk = GLA_H * GLA_DK
    st_spec = pl.BlockSpec((bsz, 1, GLA_DK, GLA_DV), lambda h: (0, h, 0, 0))
    return pl.pallas_call(
        _gla_decode_kernel,
        grid=(GLA_H,),
        in_specs=[pl.BlockSpec((bsz, GLA_DK), lambda h: (0, OD_Q // GLA_DK + h)),
                  pl.BlockSpec((bsz, GLA_DK), lambda h: (0, OD_K // GLA_DK + h)),
                  pl.BlockSpec((bsz, GLA_DV), lambda h: (0, OD_V // GLA_DV + h)),
                  pl.BlockSpec((bsz, GLA_DV), lambda h: (0, OD_R // GLA_DV + h)),
                  pl.BlockSpec((bsz, LANES), lambda h: (0, OD_A // LANES)),
                  pl.BlockSpec((LANES, GLA_DK), lambda h: (0, h)),
                  pl.BlockSpec((1, GLA_DK), lambda h: (0, h)),
                  pl.BlockSpec((1, GLA_DV), lambda h: (0, 0)),
                  st_spec],
        out_specs=[pl.BlockSpec((bsz, GLA_DV), lambda h: (0, h)), st_spec],
        out_shape=[jax.ShapeDtypeStruct((bsz, GLA_H * GLA_DV), F32),
                   jax.ShapeDtypeStruct(state.shape, F32)],
        scratch_shapes=[pltpu.VMEM((bsz, GLA_DV), F32)],
        compiler_params=_cparams(("arbitrary",)),
        name="gla_decode",
    )(p2d, p2d, p2d, p2d, p2d, wa2p, ba.reshape(1, hk), gain.reshape(1, GLA_DV), state)


def _compress_kernel(xa_ref, xb_ref, pe_ref, w1_ref, w2_ref, o_ref, acc_ref):
    nb = o_ref.shape[0]
    acc_ref[...] = jnp.zeros_like(acc_ref)
    for l in range(NSA_BLK):
        rows = pl.ds(l, nb, stride=NSA_BLK)
        x = jnp.concatenate([xa_ref[rows, :], xb_ref[rows, :]], axis=1) + pe_ref[l:l + 1, :]
        acc_ref[...] += jnp.dot(x.astype(BF16), w1_ref[l], preferred_element_type=F32)
    o_ref[...] = jnp.dot(_silu(acc_ref[...]).astype(BF16), w2_ref[...], preferred_element_type=F32)


def _compress_prompt(p2d, pe4, w1bd, w2bd):
    n = p2d.shape[0]
    rc = min(8192, n)
    cw = 4 * NSA_DH
    c0 = PM_CMP // LANES
    return pl.pallas_call(
        _compress_kernel,
        grid=(n // rc,),
        in_specs=[pl.BlockSpec((rc, LANES), lambda i: (i, c0)),
                  pl.BlockSpec((rc, LANES), lambda i: (i, c0 + 1)),
                  pl.BlockSpec(pe4.shape, lambda i: (0, 0)),
                  pl.BlockSpec(w1bd.shape, lambda i: (0, 0, 0)),
                  pl.BlockSpec((cw, cw), lambda i: (0, 0))],
        out_specs=pl.BlockSpec((rc // NSA_BLK, cw), lambda i: (i, 0)),
        out_shape=jax.ShapeDtypeStruct((n // NSA_BLK, cw), F32),
        scratch_shapes=[pltpu.VMEM((rc // NSA_BLK, cw), F32)],
        compiler_params=_cparams(("arbitrary",)),
        name="nsa_compress",
    )(p2d, p2d, pe4, w1bd, w2bd)


AQB = 256
AQB_SHIFT = 8
AUG = 128
ONES_ROWS = 16


def _slopes_lane(g, lanes):
    r = _iota((1, lanes), 1) >> AQB_SHIFT
    out = jnp.zeros((1, lanes), F32)
    for rr in range(NSA_R):
        out = jnp.where(r == rr, 2.0 ** (-(g * NSA_R + rr + 1)), out)
    return out


def _build_key_features(src_ref, row0, dst_ref, g, seq, with_onehot):
    r = _iota((NSA_DH, LANES), 0)
    for cb in range(seq // LANES):
        kpos = cb * LANES + _iota((NSA_DH, LANES), 1)
        feat = jnp.where(r == 32, (kpos >> 6).astype(F32),
                         jnp.where(r == 33, (kpos & 63).astype(F32),
                                   jnp.where((r == 34) | (r == 35), 1.0, 0.0)))
        if with_onehot:
            feat = jnp.where((kpos >> 6) == r, 1.0, feat)
        kt = src_ref[0, row0:row0 + NSA_DH, cb * LANES:(cb + 1) * LANES]
        dst_ref[g, cb * LANES:(cb + 1) * LANES, :] = jnp.concatenate([kt, feat], axis=0).T.astype(BF16)


def _attn_tile(k_ref, g, k0, v_ref, v_row0, qa_ref, mask, m_ref, acc_ref, stream, nkeys=AQB):
    s = jnp.dot(k_ref[g, pl.ds(k0, nkeys), :], qa_ref[g], preferred_element_type=F32)
    if mask is not None:
        s = jnp.where(mask, s, NEG)
    m_old = m_ref[stream]
    m_new = jnp.maximum(m_old, jnp.max(s, axis=0, keepdims=True))
    p = jnp.exp(s - m_new).astype(BF16)
    vt = v_ref[0, v_row0:v_row0 + NSA_DH, pl.ds(k0, nkeys)].astype(BF16)
    vt = jnp.concatenate([vt, jnp.ones((ONES_ROWS, nkeys), BF16)], axis=0)
    acc_ref[stream] = jnp.exp(m_old - m_new) * acc_ref[stream] + jnp.dot(vt, p, preferred_element_type=F32)
    m_ref[stream] = m_new


def _nsa_attn_kernel(qt_ref, rows_ref, win_ref, gt_ref, kc_ref, vct_ref, o_ref, ks_s, kw_s, qa_s, m_s, acc_s,
                     *, nblk):
    qb = pl.program_id(1)
    seq = rows_ref.shape[2]
    q0 = qb * AQB
    nq = NSA_R * AQB
    half = NSA_G * NSA_DH

    @pl.when(qb == 0)
    def _():
        for g in range(NSA_G):
            _build_key_features(rows_ref, 2 * half + g * NSA_DH, ks_s, g, seq, True)
            _build_key_features(win_ref, g * NSA_DH, kw_s, g, seq, False)

    pq = q0 + (_iota((1, nq), 1) & (AQB - 1))
    gates = _sigmoid(gt_ref[0])
    ksel = min(NSA_TOPK, nblk)
    scale = NSA_DH ** -0.5
    key_i = _iota((AQB, nq), 0)
    qry_t = _iota((AQB, nq), 1) & (AQB - 1)
    causal = key_i <= qry_t
    beyond = key_i > qry_t
    pq_f = q0 + _iota((32, AQB), 1)
    feat_row = _iota((32, AQB), 0)

    m_s[...] = jnp.full(m_s.shape, NEG, F32)
    acc_s[...] = jnp.zeros_like(acc_s)

    def result(i):
        return acc_s[i, 0:NSA_DH, :] / jnp.maximum(acc_s[i, NSA_DH:NSA_DH + 1, :], TINY)

    o_cmp = []
    for g in range(NSA_G):
        slope = _slopes_lane(g, nq)

        qt_g = [qt_ref[0, (g * NSA_R + r) * NSA_DH:(g * NSA_R + r + 1) * NSA_DH, :] * scale for r in range(NSA_R)]
        qs_t = jnp.concatenate(qt_g, axis=1).astype(BF16)
        sc = jnp.dot(kc_ref[0, g], qs_t, preferred_element_type=F32)
        dist_c = pq - (_iota((nblk, nq), 0) * NSA_BLK + NSA_BLK - 1)
        mask_c = dist_c >= 0
        sc = jnp.where(mask_c, sc - slope * dist_c.astype(F32), NEG)
        pc = jnp.where(mask_c, jnp.exp(sc - jnp.max(sc, axis=0, keepdims=True)), 0.0)
        pc = pc / jnp.maximum(jnp.sum(pc, axis=0, keepdims=True), TINY)
        o_cmp.append(jnp.dot(vct_ref[0, g], pc.astype(BF16), preferred_element_type=F32))
        imp = pc[:, 0:AQB]
        for r in range(1, NSA_R):
            imp = imp + pc[:, r * AQB:(r + 1) * AQB]

        n_io = _iota((nblk, AQB), 0)
        tpos = q0 + _iota((nblk, AQB), 1)
        forced = (n_io == (tpos >> 6)) | (n_io == 0)
        started = n_io * NSA_BLK <= tpos
        score = jnp.where(forced, jnp.inf, jnp.where(started, imp, -jnp.inf))
        rank = jnp.zeros((nblk, AQB), I32)
        for i in range(nblk):
            row = score[i:i + 1, :]
            rank = rank + ((row > score) | ((row == score) & (i < n_io))).astype(I32)
        sel_bias = jnp.where((rank < ksel) & started, 0.0, NEG)
        if nblk < 32:
            sel_bias = jnp.concatenate([sel_bias, jnp.zeros((32 - nblk, AQB), F32)], axis=0)

        cols = []
        for r in range(NSA_R):
            sl = 2.0 ** (-(g * NSA_R + r + 1))
            pos_feat = jnp.where(feat_row == 0, 64.0 * sl,
                                 jnp.where(feat_row == 1, sl,
                                           jnp.where(feat_row == 2, (-64.0 * sl) * (pq_f >> 6).astype(F32),
                                                     jnp.where(feat_row == 3, (-sl) * (pq_f & 63).astype(F32), 0.0))))
            cols.append(jnp.concatenate([qt_g[r], sel_bias, pos_feat], axis=0))
        qa_s[g] = jnp.concatenate(cols, axis=1).astype(BF16)

    def slc_tile(k0, mask, nkeys=AQB):
        for g in range(NSA_G):
            _attn_tile(ks_s, g, k0, rows_ref, 3 * half + g * NSA_DH, qa_s, mask, m_s, acc_s, g, nkeys)

    def win_tile(k0, mask):
        for g in range(NSA_G):
            _attn_tile(kw_s, g, k0, win_ref, half + g * NSA_DH, qa_s, mask, m_s, acc_s, NSA_G + g)

    def slc_body(kt, c):
        slc_tile(pl.multiple_of(kt * 2 * AQB, 2 * AQB), None, 2 * AQB)
        return c

    lax.fori_loop(0, qb >> 1, slc_body, 0)

    @pl.when((qb & 1) == 1)
    def _():
        slc_tile(pl.multiple_of(q0 - AQB, AQB), None)

    @pl.when(qb >= 2)
    def _():
        win_tile(pl.multiple_of(q0 - 2 * AQB, AQB), beyond)

    @pl.when(qb >= 1)
    def _():
        win_tile(pl.multiple_of(q0 - AQB, AQB), None)

    slc_tile(pl.multiple_of(q0, AQB), causal)
    win_tile(pl.multiple_of(q0, AQB), causal)

    for g in range(NSA_G):
        def gate_row(j, g=g):
            return jnp.concatenate([gates[g * 12 + r * 3 + j:g * 12 + r * 3 + j + 1, :] for r in range(NSA_R)],
                                   axis=1)

        o_t = gate_row(0) * o_cmp[g] + gate_row(1) * result(g) + gate_row(2) * result(NSA_G + g)
        o_st = jnp.concatenate([o_t[:, r * AQB:(r + 1) * AQB] for r in range(NSA_R)], axis=0)
        o_ref[:, g * NSA_R * NSA_DH:(g + 1) * NSA_R * NSA_DH] = o_st.T.astype(o_ref.dtype)


def _nsa_attn_prompt(qt, rows_t, win_t, gate_t, kc, vct, bsz, seq):
    assert NSA_WINDOW == 2 * AQB and seq % AQB == 0 and seq // NSA_BLK <= 32
    nqb = seq // AQB
    nblk = seq // NSA_BLK
    qw = NSA_H * NSA_DH
    nq = NSA_R * AQB
    per_b = lambda a: pl.BlockSpec((1,) + a.shape[1:], lambda b, i: (b,) + (0,) * (a.ndim - 1))
    per_q = lambda a: pl.BlockSpec((1, a.shape[1], AQB), lambda b, i: (b, 0, i))
    return pl.pallas_call(
        functools.partial(_nsa_attn_kernel, nblk=nblk),
        grid=(bsz, nqb),
        in_specs=[per_q(qt), per_b(rows_t), per_b(win_t), per_q(gate_t), per_b(kc), per_b(vct)],
        out_specs=pl.BlockSpec((AQB, qw), lambda b, i: (b * nqb + i, 0)),
        out_shape=jax.ShapeDtypeStruct((bsz * seq, qw), BF16),
        scratch_shapes=[pltpu.VMEM((NSA_G, seq, AUG), BF16), pltpu.VMEM((NSA_G, seq, AUG), BF16),
                        pltpu.VMEM((NSA_G, AUG, nq), BF16),
                        pltpu.VMEM((2 * NSA_G, 1, nq), F32), pltpu.VMEM((2 * NSA_G, NSA_DH + ONES_ROWS, nq), F32)],
        compiler_params=_cparams(("parallel", "arbitrary")),
        name="nsa_attn",
    )(qt, rows_t, win_t, gate_t, kc, vct)


QROWS = 16


def _head_slopes(rows):
    r = _iota((rows, 1), 0)
    out = jnp.zeros((rows, 1), F32)
    for h in range(NSA_H):
        out = jnp.where(r == h, 2.0 ** (-(h + 1)), out)
    return out


def _softmax_rows(s, valid):
    s = jnp.where(valid, s, NEG)
    p = jnp.where(valid, jnp.exp(s - jnp.max(s, axis=1, keepdims=True)), 0.0)
    return p / jnp.maximum(jnp.sum(p, axis=1, keepdims=True), TINY)


def _padded_queries(q_ref):
    q = q_ref[0] * (NSA_DH ** -0.5)
    return jnp.concatenate([q, jnp.zeros((QROWS - NSA_H, NSA_DH), F32)], axis=0).astype(BF16)


PLANES = 4 * NSA_G
PAGE_ROWS = PLANES * NSA_DH
CMP_PLANES = 2 * NSA_G


def _dec_cmp_kernel(pt_ref, cache_ref, pe_ref, w1_ref, w2_ref, q_ref, o_ref, sel_ref, buf0, buf1, acc_ref, cmp_ref, sem,
                    *, npages, nsel):
    b = pl.program_id(0)
    nb = npages * (PAGE // NSA_BLK)
    nrow = npages * CMP_PLANES
    crows = CMP_PLANES * NSA_DH
    bufs = (buf0, buf1)

    def page_copy(seq, pg, slot):
        src = pl.multiple_of(pt_ref[seq * npages + pg] * PAGE_ROWS, PAGE_ROWS)
        return pltpu.make_async_copy(cache_ref.at[pl.ds(src, crows), :],
                                     bufs[slot].at[pl.ds(pl.multiple_of(pg * crows, crows), crows), :],
                                     sem.at[slot])

    def start_all(seq, slot):
        def start(pg, c):
            page_copy(seq, pg, slot).start()
            return c
        lax.fori_loop(0, npages, start, 0)

    def wait_all(seq, slot):
        def wait(pg, c):
            page_copy(seq, pg, slot).wait()
            return c
        lax.fori_loop(0, npages, wait, 0)

    @pl.when(b == 0)
    def _():
        start_all(0, 0)

    is_k = ((_iota((nrow, 1), 0) >> 1) & 1) == 0
    for slot in range(2):
        @pl.when((b & 1) == slot)
        def _(slot=slot):
            @pl.when(b + 1 < pl.num_programs(0))
            def _():
                start_all(b + 1, 1 - slot)

            wait_all(b, slot)
            total = jnp.zeros(acc_ref.shape, F32)
            for d in range(NSA_DH):
                x = bufs[slot][pl.ds(d, nrow, stride=NSA_DH), :]
                x = x + jnp.where(is_k, pe_ref[d, 0:1, :], pe_ref[d, 1:2, :])
                total = total + jnp.dot(x.astype(BF16), w1_ref[d], preferred_element_type=F32)
            acc_ref[...] = total
    acc = acc_ref[...]
    h = _silu(jnp.where(is_k, acc[:, :LANES], acc[:, LANES:]))
    c2 = jnp.dot(h.astype(BF16), w2_ref[...], preferred_element_type=F32)
    cmp_ref[...] = jnp.where(is_k, c2[:, :LANES], c2[:, LANES:])

    pos = nb * NSA_BLK
    q16 = _padded_queries(q_ref)
    slope = _head_slopes(QROWS)
    row_grp = _iota((QROWS, 1), 0) >> 2
    lane = _iota((1, nb), 1)
    blk_of = jnp.where(lane < npages, 2 * lane, 2 * (lane - npages) + 1)
    dist = (pos - (blk_of * NSA_BLK + NSA_BLK - 1)).astype(F32)
    ri, ci = _iota((nb, nb), 0), _iota((nb, nb), 1)
    blk_r = jnp.where(ri < npages, 2 * ri, 2 * (ri - npages) + 1)
    blk_c = jnp.where(ci < npages, 2 * ci, 2 * (ci - npages) + 1)
    o_all = jnp.zeros((QROWS, NSA_DH), F32)
    for g in range(NSA_G):
        kc = cmp_ref[pl.ds(g, npages, stride=CMP_PLANES), :].astype(BF16)
        vc = cmp_ref[pl.ds(NSA_G + g, npages, stride=CMP_PLANES), :].astype(BF16)
        s = jnp.concatenate([lax.dot_general(q16, kc[:, j * NSA_DH:(j + 1) * NSA_DH], NT_DIMS,
                                             preferred_element_type=F32) for j in range(2)], axis=1)
        p = _softmax_rows(s - slope * dist, dist >= 0)
        pb = p.astype(BF16)
        o_g = sum(jnp.dot(pb[:, j * npages:(j + 1) * npages], vc[:, j * NSA_DH:(j + 1) * NSA_DH],
                          preferred_element_type=F32) for j in range(2))
        o_all = jnp.where(row_grp == g, o_g, o_all)
        imp = jnp.sum(jnp.where(row_grp == g, p, 0.0), axis=0, keepdims=True)
        score_row = jnp.where(blk_of == 0, jnp.inf, imp)
        score_col = jnp.sum(jnp.where(ri == ci, jnp.broadcast_to(score_row, (nb, nb)), 0.0),
                            axis=1, keepdims=True)
        beats = (score_col > score_row) | ((score_col == score_row) & (blk_r < blk_c))
        rank = jnp.sum(beats.astype(I32), axis=0, keepdims=True)
        sel_ref[0, g:g + 1, :] = (rank < nsel).astype(F32)
    o_ref[0] = o_all[:NSA_H]


def _dec_cmp(page_table, cache2, pe_dec, w1dec, w2dec, q3, nsel):
    bsz, npages = page_table.shape
    nb = npages * (PAGE // NSA_BLK)
    nrow = npages * CMP_PLANES
    grid_spec = pltpu.PrefetchScalarGridSpec(
        num_scalar_prefetch=1,
        grid=(bsz,),
        in_specs=[pl.BlockSpec(memory_space=pl.ANY),
                  pl.BlockSpec(pe_dec.shape, lambda b, pt: (0, 0, 0)),
                  pl.BlockSpec(w1dec.shape, lambda b, pt: (0, 0, 0)),
                  pl.BlockSpec(w2dec.shape, lambda b, pt: (0, 0)),
                  pl.BlockSpec((1, NSA_H, NSA_DH), lambda b, pt: (b, 0, 0))],
        out_specs=[pl.BlockSpec((1, NSA_H, NSA_DH), lambda b, pt: (b, 0, 0)),
                   pl.BlockSpec((1, NSA_G, nb), lambda b, pt: (b, 0, 0))],
        scratch_shapes=[pltpu.VMEM((nrow * NSA_DH, LANES), F32), pltpu.VMEM((nrow * NSA_DH, LANES), F32),
                        pltpu.VMEM((nrow, 2 * LANES), F32), pltpu.VMEM((nrow, LANES), F32),
                        pltpu.SemaphoreType.DMA((2,))])
    return pl.pallas_call(
        functools.partial(_dec_cmp_kernel, npages=npages, nsel=nsel),
        grid_spec=grid_spec,
        out_shape=[jax.ShapeDtypeStruct((bsz, NSA_H, NSA_DH), F32),
                   jax.ShapeDtypeStruct((bsz, NSA_G, nb), F32)],
        compiler_params=_cparams(("arbitrary",)),
        name="nsa_dec_cmp",
    )(page_table.reshape(-1), cache2, pe_dec, w1dec, w2dec, q3)


def _attend_with_self(q16, slope, kt, vt, dist, valid, k_self, v_self):
    s = jnp.dot(q16, kt.astype(BF16), preferred_element_type=F32) - slope * dist
    s_self = jnp.sum(q16.astype(F32) * k_self.astype(BF16).astype(F32), axis=1, keepdims=True)
    m = jnp.maximum(jnp.max(jnp.where(valid, s, NEG), axis=1, keepdims=True), s_self)
    p = jnp.where(valid, jnp.exp(s - m), 0.0)
    p_self = jnp.exp(s_self - m)
    num = lax.dot_general(p.astype(BF16), vt.astype(BF16), NT_DIMS, preferred_element_type=F32) + p_self * v_self
    return num / jnp.maximum(jnp.sum(p, axis=1, keepdims=True) + p_self, TINY)


def _dec_attn_kernel(pt_ref, idx_ref, cache_ref, q_ref, oc_ref, kn_ref, win_ref, gt_ref, o_ref, kbuf, vbuf, sem,
                     *, npages, nsel, wlen):
    b = pl.program_id(0)
    pos = npages * PAGE

    copies = []
    for g in range(NSA_G):
        for j in range(nsel):
            blk = idx_ref[(b * NSA_G + g) * NSA_TOPK + j]
            base = pt_ref[b * npages + (blk >> 1)] * PAGE_ROWS
            dst = pl.ds(j * PAGE, PAGE)
            k_rows = pl.ds(pl.multiple_of(base + (2 * NSA_G + g) * NSA_DH, NSA_DH), NSA_DH)
            v_rows = pl.ds(pl.multiple_of(base + (3 * NSA_G + g) * NSA_DH, NSA_DH), NSA_DH)
            copies.append(pltpu.make_async_copy(cache_ref.at[k_rows, :], kbuf.at[g, :, dst], sem.at[0]))
            copies.append(pltpu.make_async_copy(cache_ref.at[v_rows, :], vbuf.at[g, :, dst], sem.at[0]))
    for cp in copies:
        cp.start()
    for cp in copies:
        cp.wait()

    q16 = _padded_queries(q_ref)
    slope = _head_slopes(QROWS)
    row_grp = _iota((QROWS, 1), 0) >> 2
    kn = kn_ref[0]
    lane = _iota((1, nsel * PAGE), 1)
    within = lane & (PAGE - 1)
    wdist = wlen - _iota((1, wlen), 1)
    o_slc = jnp.zeros((QROWS, NSA_DH), F32)
    o_win = jnp.zeros((QROWS, NSA_DH), F32)
    for g in range(NSA_G):
        gs = slice(g * NSA_DH, (g + 1) * NSA_DH)
        blk_lane = jnp.zeros((1, nsel * PAGE), I32)
        for j in range(nsel):
            blk_lane = jnp.where((lane >> 7) == j, idx_ref[(b * NSA_G + g) * NSA_TOPK + j], blk_lane)
        valid = (within >> 6) == (blk_lane & 1)
        kpos = (blk_lane >> 1) * PAGE + within
        o_g = _attend_with_self(q16, slope, kbuf[g], vbuf[g], (pos - kpos).astype(F32), valid,
                                kn[0:1, gs], kn[1:2, gs])
        o_slc = jnp.where(row_grp == g, o_g, o_slc)
        o_g = _attend_with_self(q16, slope, win_ref[0, g * NSA_DH:(g + 1) * NSA_DH, :],
                                win_ref[0, (NSA_G + g) * NSA_DH:(NSA_G + g + 1) * NSA_DH, :],
                                wdist.astype(F32), wdist < NSA_WINDOW, kn[2:3, gs], kn[3:4, gs])
        o_win = jnp.where(row_grp == g, o_g, o_win)

    gates = _sigmoid(gt_ref[pl.ds(b, 1), :])
    r_io = _iota((QROWS, LANES), 0)
    l_io = _iota((QROWS, LANES), 1)

    def gate_col(j):
        return jnp.sum(jnp.where(l_io == 3 * r_io + j, gates, 0.0), axis=1, keepdims=True)

    o_cmp = jnp.concatenate([oc_ref[0], jnp.zeros((QROWS - NSA_H, NSA_DH), F32)], axis=0)
    o = gate_col(0) * o_cmp + gate_col(1) * o_slc + gate_col(2) * o_win
    o_ref[0] = o[:NSA_H]


def _dec_attn(page_table, idx, cache2, q3, ocmp, knew, win_t, gates, nsel):
    bsz, npages = page_table.shape
    wlen = win_t.shape[2]
    hd = pl.BlockSpec((1, NSA_H, NSA_DH), lambda b, pt, ix: (b, 0, 0))
    grid_spec = pltpu.PrefetchScalarGridSpec(
        num_scalar_prefetch=2,
        grid=(bsz,),
        in_specs=[pl.BlockSpec(memory_space=pl.ANY), hd, hd,
                  pl.BlockSpec((1,) + knew.shape[1:], lambda b, pt, ix: (b, 0, 0)),
                  pl.BlockSpec((1,) + win_t.shape[1:], lambda b, pt, ix: (b, 0, 0)),
                  pl.BlockSpec(gates.shape, lambda b, pt, ix: (0, 0))],
        out_specs=hd,
        scratch_shapes=[pltpu.VMEM((NSA_G, NSA_DH, nsel * PAGE), F32), pltpu.VMEM((NSA_G, NSA_DH, nsel * PAGE), F32),
                        pltpu.SemaphoreType.DMA((1,))])
    return pl.pallas_call(
        functools.partial(_dec_attn_kernel, npages=npages, nsel=nsel, wlen=wlen),
        grid_spec=grid_spec,
        out_shape=jax.ShapeDtypeStruct((bsz, NSA_H, NSA_DH), F32),
        compiler_params=_cparams(("arbitrary",)),
        name="nsa_dec_attn",
    )(page_table.reshape(-1), idx.reshape(-1), cache2, q3, ocmp, knew, win_t, gates)


def _prep_weights(w_in_even, nsa_cmp_pos, nsa_cmp_w1, nsa_cmp_w2, w_out_even, w_in_odd, gla_wa2, w_out_odd,
                  ffn_w13, ffn_w2):
    d = D_MODEL
    wie = jnp.pad(w_in_even[0], ((0, 0), (0, PROJ_W - w_in_even.shape[2]))).astype(BF16)
    a0 = 2 * GLA_H * GLA_DK + GLA_H * GLA_DV
    wo = w_in_odd[0]
    wio = jnp.concatenate([wo[:, :a0], wo[:, a0 + GLA_RANK:], wo[:, a0:a0 + GLA_RANK]], axis=1)
    wio = jnp.pad(wio, ((0, 0), (0, ODD_W - wio.shape[1]))).astype(BF16)
    wa2p = jnp.pad(gla_wa2[0], ((0, LANES - GLA_RANK), (0, 0))).astype(BF16)
    hv = HG_H * HG_DV
    woe_h, woe_n = w_out_even[0, :hv].astype(BF16), w_out_even[0, hv:].astype(BF16)
    woo = w_out_odd[0].astype(BF16)
    nj = D_FF // FF_TILE
    w13t, w2b = [], []
    for l in range(ffn_w13.shape[0]):
        gate = ffn_w13[l, :, :D_FF].reshape(d, nj, FF_TILE)
        up = ffn_w13[l, :, D_FF:].reshape(d, nj, FF_TILE)
        w13t.append(jnp.concatenate([gate, up], axis=2).reshape(d, 2 * D_FF).astype(BF16))
        w2b.append(ffn_w2[l].astype(BF16))
    w1 = nsa_cmp_w1[0].reshape(2, NSA_BLK, NSA_DH, NSA_DH)
    w2 = nsa_cmp_w2[0]
    cw = 4 * NSA_DH
    eye = jnp.eye(2 * NSA_G, dtype=F32).reshape(2, NSA_G, 2, NSA_G)
    w1bd = jnp.einsum("cldj,cgCG->lcgdCGj", w1, eye).reshape(NSA_BLK, cw, cw)
    w2bd = jnp.einsum("cjd,cgCG->cgjCGd", w2, eye).reshape(cw, cw)
    pe = nsa_cmp_pos[0]
    pe4 = jnp.concatenate([pe[0], pe[0], pe[1], pe[1]], axis=1)
    we = w_in_even[0]
    hgw = 2 * HG_H * HG_DK + 2 * HG_H * HG_DV
    nq0 = hgw + NSA_H * NSA_DH
    w_main = jnp.concatenate([we[:, :hgw], we[:, nq0:nq0 + cw]], axis=1).astype(BF16)
    w_tr = jnp.pad(we[:, hgw:].T, ((0, T_END - (we.shape[1] - hgw)), (0, 0))).astype(BF16)
    eye2 = jnp.eye(2, dtype=F32)
    w1dec = jnp.einsum("cldo,jJ->djlcJo", w1, eye2).reshape(NSA_DH, 2 * NSA_BLK, 4 * NSA_DH)
    w2dec = jnp.einsum("cod,jJ->jocJd", w2, eye2).reshape(2 * NSA_DH, 4 * NSA_DH)
    pe_dec = jnp.tile(pe.transpose(2, 0, 1), (1, 1, 2))
    return dict(wie=wie, wio=wio, wa2p=wa2p, woe_h=woe_h, woe_n=woe_n, woo=woo, w13t=w13t, w2b=w2b,
                w1bd=w1bd.astype(BF16), w2bd=w2bd.astype(BF16), pe4=pe4, w_main=w_main, w_tr=w_tr,
                w1dec=w1dec.astype(BF16), w2dec=w2dec.astype(BF16), pe_dec=pe_dec)


def _forward_prompt(x, pw, hg_lb_logits, hg_norm, gla_ba, gla_norm, norm_mix, norm_ffn, norm_final):
    bsz, seq, d = x.shape
    n = bsz * seq
    x2d = x.reshape(n, d)
    tm = 512 if seq % 512 == 0 else 256
    tmf = next(t for t in (1024, 512, 256) if n % t == 0)
    tb = tm

    p, qt, rows_t, win_t, gate_t = _norm_proj_t(x2d, norm_mix[0], pw["w_main"], pw["w_tr"], bsz, seq, tm)
    o_h, hg_state = _hgrn_prompt(p, hg_lb_logits, hg_norm[0], bsz, seq, tb)
    cmp = _compress_prompt(p, pw["pe4"], pw["w1bd"], pw["w2bd"])
    nblk = seq // NSA_BLK
    cmp4 = cmp.reshape(bsz, nblk, 2 * NSA_G, NSA_DH).transpose(0, 2, 1, 3)
    kc = cmp4[:, :NSA_G].astype(BF16)
    vct = cmp4[:, NSA_G:].transpose(0, 1, 3, 2).astype(BF16)
    o_n = _nsa_attn_prompt(qt, rows_t, win_t, gate_t, kc, vct, bsz, seq)
    x1 = _out_ffn([o_h, o_n], x2d, [pw["woe_h"], pw["woe_n"]], norm_ffn[0], pw["w13t"][0], pw["w2b"][0], None, tmf)

    rows = rows_t.reshape(bsz, 4, NSA_G, NSA_DH, seq).transpose(0, 4, 1, 2, 3)[None]
    wn = min(NSA_WINDOW, seq)
    win = win_t[:, :, seq - wn:].reshape(bsz, 2, NSA_G, NSA_DH, wn).transpose(0, 4, 1, 2, 3)[None]

    p2 = _norm_proj(x1, norm_mix[1], pw["wio"], tm, ODD_W)
    o_g, gla_state = _gla_prompt(p2, pw["wa2p"], gla_ba[0], gla_norm[0], bsz, seq, tb)
    y = _out_ffn([o_g], x1, [pw["woo"]], norm_ffn[1], pw["w13t"][1], pw["w2b"][1], norm_final, tmf)
    return y.reshape(bsz, seq, d), rows, win, hg_state[None], gla_state[None]


def _forward_sample(x, pw, cache_nsa_kv, cache_win_kv, state_hgrn, state_gla, page_table, hg_lb_logits, hg_norm,
                    gla_ba, gla_norm, norm_mix, norm_ffn, norm_final):
    bsz, seq, d = x.shape
    x2d = x.reshape(bsz, d)
    npages = page_table.shape[1]
    nblk_total = npages * (PAGE // NSA_BLK) + 1
    nsel = min(NSA_TOPK, nblk_total) - 1

    p = _norm_proj(x2d, norm_mix[0], pw["wie"], bsz, PROJ_W // 2)
    o_h, hg_state = _hgrn_decode(p, hg_lb_logits, hg_norm[0], state_hgrn[0])

    cache2 = cache_nsa_kv[0].transpose(0, 2, 3, 4, 1).reshape(cache_nsa_kv.shape[1] * PAGE_ROWS, PAGE)
    q3 = p[:, EV_NQ:EV_CMP].reshape(bsz, NSA_H, NSA_DH)
    ocmp, sel = _dec_cmp(page_table, cache2, pw["pe_dec"], pw["w1dec"], pw["w2dec"], q3, nsel)
    sel = sel.reshape(bsz, NSA_G, 2, npages).transpose(0, 1, 3, 2).reshape(bsz, NSA_G, 2 * npages)
    idx = jnp.argsort(-sel, axis=-1, stable=True)[..., :NSA_TOPK].astype(I32)
    knew = p[:, EV_SLC:EV_GATE].reshape(bsz, 4, NSA_G * NSA_DH)
    wlen = cache_win_kv.shape[2]
    win_t = cache_win_kv[0].transpose(0, 2, 3, 4, 1).reshape(bsz, 2 * NSA_G * NSA_DH, wlen)
    o_n = _dec_attn(page_table, idx, cache2, q3, ocmp, knew, win_t, p[:, EV_GATE:EV_GATE + LANES], nsel)
    o_n = o_n.reshape(bsz, NSA_H * NSA_DH)
    x1 = _out_ffn([o_h, o_n], x2d, [pw["woe_h"], pw["woe_n"]], norm_ffn[0], pw["w13t"][0], pw["w2b"][0], None, bsz)

    rows = p[:, EV_CMP:EV_WIN].reshape(1, bsz, 1, 4, NSA_G, NSA_DH)
    win_new = p[:, EV_WIN:EV_GATE].reshape(1, bsz, 1, 2, NSA_G, NSA_DH)
    wk = jnp.concatenate([cache_win_kv[:1], win_new], axis=2)
    wn = min(NSA_WINDOW, npages * PAGE + 1)
    win = wk[:, :, wk.shape[2] - wn:]

    p2 = _norm_proj(x1, norm_mix[1], pw["wio"], bsz, ODD_W)
    o_g, gla_state = _gla_decode(p2, pw["wa2p"], gla_ba[0], gla_norm[0], state_gla[0])
    y = _out_ffn([o_g], x1, [pw["woo"]], norm_ffn[1], pw["w13t"][1], pw["w2b"][1], norm_final, bsz)
    return y.reshape(bsz, 1, d), rows, win, hg_state[None], gla_state[None]


def kernel(x_prompt, x_sample, cache_nsa_kv, cache_win_kv, state_hgrn, state_gla, page_table, w_in_even,
           hg_lb_logits, hg_norm, nsa_cmp_pos, nsa_cmp_w1, nsa_cmp_w2, w_out_even, w_in_odd, gla_wa2, gla_ba,
           gla_norm, w_out_odd, norm_mix, norm_ffn, norm_final, ffn_w13, ffn_w2):
    pw = _prep_weights(w_in_even, nsa_cmp_pos, nsa_cmp_w1, nsa_cmp_w2, w_out_even, w_in_odd, gla_wa2, w_out_odd,
                       ffn_w13, ffn_w2)
    y_p, kv_p, win_p, hg_p, gla_p = _forward_prompt(x_prompt, pw, hg_lb_logits, hg_norm, gla_ba, gla_norm,
                                                    norm_mix, norm_ffn, norm_final)
    y_s, kv_s, win_s, hg_s, gla_s = _forward_sample(x_sample, pw, cache_nsa_kv, cache_win_kv, state_hgrn,
                                                    state_gla, page_table, hg_lb_logits, hg_norm, gla_ba,
                                                    gla_norm, norm_mix, norm_ffn, norm_final)
    return (y_p, y_s, kv_p, kv_s, win_p, win_s, hg_p, hg_s, gla_p, gla_s)
```

```python
import functools

import jax
import jax.numpy as jnp
from jax import lax
from jax.experimental import pallas as pl
from jax.experimental.pallas import tpu as pltpu

F32 = jnp.float32
BF16 = jnp.bfloat16
I32 = jnp.int32

D_MODEL = 1024
HG_H, HG_DK, HG_DV = 4, 128, 128
NSA_H, NSA_DH, NSA_G, NSA_R = 8, 64, 2, 4
NSA_BLK = 64
NSA_TOPK = 16
NSA_WINDOW = 512
NSA_QB = 128
GLA_H, GLA_DK, GLA_DV = 4, 128, 256
GLA_RANK = 16
GLA_TAU = 16.0
D_FF = 2816
EPS = 1e-6
NEG = -1e30
TINY = 1e-30
PAGE = 128

PROJ_W = 3584
ODD_W = 3200
EV_Q, EV_F, EV_I, EV_OG = 0, 512, 1024, 1536
EV_NQ, EV_CMP, EV_SLC, EV_WIN, EV_GATE = 2048, 2560, 2816, 3072, 3328
OD_Q, OD_K, OD_V, OD_R, OD_A = 0, 512, 1024, 2048, 3072

LANES = 128
V7X_VMEM_LIMIT = 56 * 1024 * 1024
FF_TILE = 1408
CHUNK = 128
SUB = 16

NT_DIMS = (((1,), (1,)), ((), ()))
TN_DIMS = (((0,), (0,)), ((), ()))


def _cparams(sem):
    return pltpu.CompilerParams(dimension_semantics=sem, vmem_limit_bytes=V7X_VMEM_LIMIT)


def _rms(x, g):
    return x * lax.rsqrt(jnp.mean(x * x, axis=-1, keepdims=True) + EPS) * g


def _sigmoid(x):
    return 0.5 * jnp.tanh(0.5 * x) + 0.5


def _silu(x):
    return x * _sigmoid(x)


def _iota(shape, dim):
    return lax.broadcasted_iota(I32, shape, dim)


def _norm_proj_kernel(x_ref, g_ref, w_ref, o_ref, xn_ref):
    @pl.when(pl.program_id(1) == 0)
    def _():
        xn_ref[...] = _rms(x_ref[...], g_ref[...]).astype(BF16)

    o_ref[...] = jnp.dot(xn_ref[...], w_ref[...], preferred_element_type=F32)


def _norm_proj(x2d, gain, w, tm, tn):
    m, k = x2d.shape
    n = w.shape[1]
    return pl.pallas_call(
        _norm_proj_kernel,
        grid=(m // tm, n // tn),
        in_specs=[pl.BlockSpec((tm, k), lambda i, j: (i, 0)),
                  pl.BlockSpec((1, k), lambda i, j: (0, 0)),
                  pl.BlockSpec((k, tn), lambda i, j: (0, j))],
        out_specs=pl.BlockSpec((tm, tn), lambda i, j: (i, j)),
        out_shape=jax.ShapeDtypeStruct((m, n), F32),
        scratch_shapes=[pltpu.VMEM((tm, k), BF16)],
        compiler_params=_cparams(("parallel", "arbitrary")),
        name="norm_proj",
    )(x2d, gain.reshape(1, k), w)


PM_W = 2304
PM_CMP = 2048
T_Q, T_ROWS, T_WIN, T_GATE, T_END = 0, 512, 1024, 1280, 1312


def _norm_proj_t_kernel(x_ref, g_ref, w_ref, wt_ref, o_ref, q_ref, rows_ref, win_ref, gate_ref):
    xn = _rms(x_ref[...], g_ref[...]).astype(BF16)
    o_ref[...] = jnp.dot(xn, w_ref[...], preferred_element_type=F32)
    t = lax.dot_general(wt_ref[...], xn, NT_DIMS, preferred_element_type=F32)
    q_ref[0] = t[T_Q:T_ROWS]
    rows_ref[0] = t[T_ROWS:T_WIN]
    win_ref[0] = t[T_WIN:T_GATE]
    gate_ref[0] = t[T_GATE:T_END]


def _norm_proj_t(x2d, gain, w, wt, bsz, seq, tm):
    k = x2d.shape[1]
    nt = seq // tm
    tr = lambda rows: pl.BlockSpec((1, rows, tm), lambda b, t: (b, 0, t))
    sizes = (T_ROWS - T_Q, T_WIN - T_ROWS, T_GATE - T_WIN, T_END - T_GATE)
    return pl.pallas_call(
        _norm_proj_t_kernel,
        grid=(bsz, nt),
        in_specs=[pl.BlockSpec((tm, k), lambda b, t: (b * nt + t, 0)),
                  pl.BlockSpec((1, k), lambda b, t: (0, 0)),
                  pl.BlockSpec(w.shape, lambda b, t: (0, 0)),
                  pl.BlockSpec(wt.shape, lambda b, t: (0, 0))],
        out_specs=[pl.BlockSpec((tm, PM_W), lambda b, t: (b * nt + t, 0))] + [tr(r) for r in sizes],
        out_shape=[jax.ShapeDtypeStruct((bsz * seq, PM_W), F32)]
        + [jax.ShapeDtypeStruct((bsz, r, seq), F32) for r in sizes],
        compiler_params=_cparams(("parallel", "arbitrary")),
        name="norm_proj_t",
    )(x2d, gain.reshape(1, k), w, wt)


def _out_ffn_kernel(*refs, n_mix, final_norm):
    mix_refs = refs[:n_mix]
    res_ref = refs[n_mix]
    wo_refs = refs[n_mix + 1:2 * n_mix + 1]
    g_ref, w13_ref, w2_ref = refs[2 * n_mix + 1:2 * n_mix + 4]
    pos = 2 * n_mix + 4
    gf_ref = refs[pos] if final_norm else None
    pos += 1 if final_norm else 0
    o_ref, x1_ref, h_ref, acc_ref = refs[pos:pos + 4]
    j = pl.program_id(1)

    @pl.when(j == 0)
    def _():
        x1 = res_ref[...]
        for m_ref, w_ref in zip(mix_refs, wo_refs):
            x1 = x1 + jnp.dot(m_ref[...].astype(BF16), w_ref[...], preferred_element_type=F32)
        x1_ref[...] = x1
        h_ref[...] = _rms(x1, g_ref[...]).astype(BF16)
        acc_ref[...] = jnp.zeros_like(acc_ref)

    gu = jnp.dot(h_ref[...], w13_ref[...], preferred_element_type=F32)
    act = _silu(gu[:, :FF_TILE]) * gu[:, FF_TILE:]
    acc_ref[...] += jnp.dot(act.astype(BF16), w2_ref[...], preferred_element_type=F32)

    @pl.when(j == pl.num_programs(1) - 1)
    def _():
        y = x1_ref[...] + acc_ref[...]
        if final_norm:
            y = _rms(y, gf_ref[...])
        o_ref[...] = y


def _out_ffn(mixes, res, wos, g_ffn, w13t, w2, g_final, tm):
    m, d = res.shape
    n_mix = len(mixes)
    nj = D_FF // FF_TILE
    final_norm = g_final is not None
    in_specs = [pl.BlockSpec((tm, mx.shape[1]), lambda i, j: (i, 0)) for mx in mixes]
    in_specs.append(pl.BlockSpec((tm, d), lambda i, j: (i, 0)))
    in_specs += [pl.BlockSpec(w.shape, lambda i, j: (0, 0)) for w in wos]
    in_specs += [pl.BlockSpec((1, d), lambda i, j: (0, 0)),
                 pl.BlockSpec((d, 2 * FF_TILE), lambda i, j: (0, j)),
                 pl.BlockSpec((FF_TILE, d), lambda i, j: (j, 0))]
    args = list(mixes) + [res] + list(wos) + [g_ffn.reshape(1, d), w13t, w2]
    if final_norm:
        in_specs.append(pl.BlockSpec((1, d), lambda i, j: (0, 0)))
        args.append(g_final.reshape(1, d))
    return pl.pallas_call(
        functools.partial(_out_ffn_kernel, n_mix=n_mix, final_norm=final_norm),
        grid=(m // tm, nj),
        in_specs=in_specs,
        out_specs=pl.BlockSpec((tm, d), lambda i, j: (i, 0)),
        out_shape=jax.ShapeDtypeStruct((m, d), F32),
        scratch_shapes=[pltpu.VMEM((tm, d), F32), pltpu.VMEM((tm, d), BF16), pltpu.VMEM((tm, d), F32)],
        compiler_params=_cparams(("parallel", "arbitrary")),
        name="out_ffn",
    )(*args)


def _gla_chunk(q, k, v, g, st):
    c = q.shape[0]
    tri = (_iota((c, c), 0) >= _iota((c, c), 1)).astype(BF16)
    g_hi = g.astype(BF16)
    g_r1 = g - g_hi.astype(F32)
    g_mid = g_r1.astype(BF16)
    g_lo = (g_r1 - g_mid.astype(F32)).astype(BF16)
    b = (jnp.dot(tri, g_hi, preferred_element_type=F32) + jnp.dot(tri, g_mid, preferred_element_type=F32)
         + jnp.dot(tri, g_lo, preferred_element_type=F32))
    o = lax.dot_general((q * jnp.exp(b)).astype(BF16), st.astype(BF16), NT_DIMS, preferred_element_type=F32)
    lane = _iota((SUB, c), 1)
    row = _iota((SUB, c), 0)
    hs = SUB // 2
    rows = []
    for blk in range(c // SUB):
        lo = blk * SUB
        b_i, q_i, k_i = b[lo:lo + SUB], q[lo:lo + SUB], k[lo:lo + SUB]
        a_top = jnp.zeros((hs, c), F32)
        a_bot = jnp.zeros((hs, c), F32)
        for s in range(SUB):
            if s < hs:
                e = jnp.exp(b_i - b_i[s:s + 1])
                a = jnp.sum(q_i * e * k_i[s:s + 1], axis=1, keepdims=True)
                a_top = jnp.where(lane[:hs] == lo + s, a[:hs], a_top)
                a_bot = jnp.where(lane[:hs] == lo + s, a[hs:], a_bot)
            else:
                e = jnp.exp(b_i[hs:] - b_i[s:s + 1])
                a = jnp.sum(q_i[hs:] * e * k_i[s:s + 1], axis=1, keepdims=True)
                a_bot = jnp.where(lane[:hs] == lo + s, a, a_bot)
        a_blk = jnp.where(lane <= lo + row, jnp.concatenate([a_top, a_bot], axis=0), 0.0)
        if blk > 0:
            r = b[lo - 1:lo]
            qt = q_i * jnp.exp(b_i - r)
            kt = k * jnp.exp(jnp.minimum(r - b, 0.0))
            a_off = lax.dot_general(qt.astype(BF16), kt.astype(BF16), NT_DIMS, preferred_element_type=F32)
            a_blk = jnp.where(lane < lo, a_off, a_blk)
        rows.append(a_blk)
    a_full = jnp.concatenate(rows, axis=0)
    o = o + jnp.dot(a_full.astype(BF16), v.astype(BF16), preferred_element_type=F32)
    b_last = b[c - 1:c]
    kd = k * jnp.exp(b_last - b)
    st_new = st * jnp.exp(b_last) + lax.dot_general(v.astype(BF16), kd.astype(BF16), TN_DIMS,
                                                    preferred_element_type=F32)
    return o, st_new


def _lower_bound(lbl_ref, col):
    l = lbl_ref[:, col:col + HG_DK]
    e = jnp.exp(l - jnp.max(l, axis=0, keepdims=True))
    return e[0:1] / jnp.sum(e, axis=0, keepdims=True)


def _hgrn_kernel(q_ref, f_ref, i_ref, og_ref, lbl_ref, gain_ref, o_ref, s_ref, st_ref, *, nch):
    t = pl.program_id(1)

    @pl.when(t == 0)
    def _():
        st_ref[...] = jnp.zeros_like(st_ref)

    def body(ci, carry):
        r0 = pl.multiple_of(ci * CHUNK, CHUNK)
        rs = pl.ds(r0, CHUNK)
        for h in range(HG_H):
            cs = slice(h * HG_DK, (h + 1) * HG_DK)
            lb = _lower_bound(lbl_ref, h * HG_DK)
            q = _silu(q_ref[rs, cs])
            f = lb + (1.0 - lb) * _sigmoid(f_ref[rs, cs])
            o, st_new = _gla_chunk(q, 1.0 - f, i_ref[rs, cs], jnp.log(f), st_ref[h])
            st_ref[h] = st_new
            o = _rms(o, gain_ref[...]) * _sigmoid(og_ref[rs, cs])
            o_ref[rs, cs] = o.astype(o_ref.dtype)
        return carry

    lax.fori_loop(0, nch, body, 0)

    @pl.when(t == pl.num_programs(1) - 1)
    def _():
        for h in range(HG_H):
            s_ref[0, h] = st_ref[h].T


def _hgrn_prompt(p2d, lb_logits, gain, bsz, seq, tb):
    nt = seq // tb
    hk = HG_H * HG_DK
    row = lambda b, t: b * nt + t
    col_spec = lambda c: pl.BlockSpec((tb, hk), lambda b, t, c=c: (row(b, t), c))
    return pl.pallas_call(
        functools.partial(_hgrn_kernel, nch=tb // CHUNK),
        grid=(bsz, nt),
        in_specs=[col_spec(EV_Q // hk), col_spec(EV_F // hk), col_spec(EV_I // hk), col_spec(EV_OG // hk),
                  pl.BlockSpec(lb_logits.shape, lambda b, t: (0, 0)),
                  pl.BlockSpec((1, HG_DV), lambda b, t: (0, 0))],
        out_specs=[pl.BlockSpec((tb, hk), lambda b, t: (row(b, t), 0)),
                   pl.BlockSpec((1, HG_H, HG_DK, HG_DV), lambda b, t: (b, 0, 0, 0))],
        out_shape=[jax.ShapeDtypeStruct((bsz * seq, hk), BF16),
                   jax.ShapeDtypeStruct((bsz, HG_H, HG_DK, HG_DV), F32)],
        scratch_shapes=[pltpu.VMEM((HG_H, HG_DV, HG_DK), F32)],
        compiler_params=_cparams(("parallel", "arbitrary")),
        name="hgrn_chunk",
    )(p2d, p2d, p2d, p2d, lb_logits, gain.reshape(1, HG_DV))


def _log_sigmoid(x):
    return jnp.minimum(x, 0.0) - jnp.log(1.0 + jnp.exp(-jnp.abs(x)))


def _gla_kernel(q_ref, k_ref, v_ref, r_ref, a_ref, wa_ref, ba_ref, gain_ref, o_ref, s_ref, st_ref, *, nch):
    t = pl.program_id(1)

    @pl.when(t == 0)
    def _():
        st_ref[...] = jnp.zeros_like(st_ref)

    def body(ci, carry):
        r0 = pl.multiple_of(ci * CHUNK, CHUNK)
        rs = pl.ds(r0, CHUNK)
        gate = jnp.dot(a_ref[rs, :].astype(BF16), wa_ref[...], preferred_element_type=F32) + ba_ref[...]
        logf = _log_sigmoid(gate) * (1.0 / GLA_TAU)
        for h in range(GLA_H):
            ks = slice(h * GLA_DK, (h + 1) * GLA_DK)
            vs = slice(h * GLA_DV, (h + 1) * GLA_DV)
            q = q_ref[rs, ks] * (GLA_DK ** -0.5)
            o, st_new = _gla_chunk(q, k_ref[rs, ks], v_ref[rs, vs], logf[:, ks], st_ref[h])
            st_ref[h] = st_new
            o = _rms(o, gain_ref[...]) * _silu(r_ref[rs, vs])
            o_ref[rs, vs] = o.astype(o_ref.dtype)
        return carry

    lax.fori_loop(0, nch, body, 0)

    @pl.when(t == pl.num_programs(1) - 1)
    def _():
        for h in range(GLA_H):
            s_ref[0, h] = st_ref[h].T


def _gla_prompt(p2d, wa2p, ba, gain, bsz, seq, tb):
    nt = seq // tb
    hk, hv = GLA_H * GLA_DK, GLA_H * GLA_DV
    row = lambda b, t: b * nt + t
    return pl.pallas_call(
        functools.partial(_gla_kernel, nch=tb // CHUNK),
        grid=(bsz, nt),
        in_specs=[pl.BlockSpec((tb, hk), lambda b, t: (row(b, t), OD_Q // hk)),
                  pl.BlockSpec((tb, hk), lambda b, t: (row(b, t), OD_K // hk)),
                  pl.BlockSpec((tb, hv), lambda b, t: (row(b, t), OD_V // hv)),
                  pl.BlockSpec((tb, hv), lambda b, t: (row(b, t), OD_R // hv)),
                  pl.BlockSpec((tb, LANES), lambda b, t: (row(b, t), OD_A // LANES)),
                  pl.BlockSpec(wa2p.shape, lambda b, t: (0, 0)),
                  pl.BlockSpec((1, hk), lambda b, t: (0, 0)),
                  pl.BlockSpec((1, GLA_DV), lambda b, t: (0, 0))],
        out_specs=[pl.BlockSpec((tb, hv), lambda b, t: (row(b, t), 0)),
                   pl.BlockSpec((1, GLA_H, GLA_DK, GLA_DV), lambda b, t: (b, 0, 0, 0))],
        out_shape=[jax.ShapeDtypeStruct((bsz * seq, hv), BF16),
                   jax.ShapeDtypeStruct((bsz, GLA_H, GLA_DK, GLA_DV), F32)],
        scratch_shapes=[pltpu.VMEM((GLA_H, GLA_DV, GLA_DK), F32)],
        compiler_params=_cparams(("parallel", "arbitrary")),
        name="gla_chunk",
    )(p2d, p2d, p2d, p2d, p2d, wa2p, ba.reshape(1, hk), gain.reshape(1, GLA_DV))


def _to_columns(x):
    bsz = x.shape[0]
    if bsz < LANES:
        x = jnp.concatenate([x, jnp.zeros((LANES - bsz, x.shape[1]), x.dtype)], axis=0)
    return x.T


def _decode_update(q, k, v, g, s_ref, so_ref, o_scr):
    bsz = q.shape[0]
    qt, kt, et = _to_columns(q), _to_columns(k), _to_columns(jnp.exp(g))
    for b in range(bsz):
        s_new = et[:, b:b + 1] * s_ref[b, 0] + kt[:, b:b + 1] * v[b:b + 1, :]
        so_ref[b, 0] = s_new
        o_scr[b:b + 1, :] = jnp.sum(qt[:, b:b + 1] * s_new, axis=0, keepdims=True)


def _hgrn_decode_kernel(q_ref, f_ref, i_ref, og_ref, lbl_ref, gain_ref, s_ref, o_ref, so_ref, o_scr):
    l = lbl_ref[0]
    e = jnp.exp(l - jnp.max(l, axis=0, keepdims=True))
    lb = e[0:1] / jnp.sum(e, axis=0, keepdims=True)
    f = lb + (1.0 - lb) * _sigmoid(f_ref[...])
    _decode_update(_silu(q_ref[...]), 1.0 - f, i_ref[...], jnp.log(f), s_ref, so_ref, o_scr)
    o_ref[...] = _rms(o_scr[...], gain_ref[...]) * _sigmoid(og_ref[...])


def _hgrn_decode(p2d, lb_logits, gain, state):
    bsz = p2d.shape[0]
    col = lambda c: pl.BlockSpec((bsz, HG_DK), lambda h, c=c: (0, c + h))
    lbl3 = lb_logits.reshape(lb_logits.shape[0], HG_H, HG_DK).transpose(1, 0, 2)
    st_spec = pl.BlockSpec((bsz, 1, HG_DK, HG_DV), lambda h: (0, h, 0, 0))
    return pl.pallas_call(
        _hgrn_decode_kernel,
        grid=(HG_H,),
        in_specs=[col(EV_Q // HG_DK), col(EV_F // HG_DK), col(EV_I // HG_DK), col(EV_OG // HG_DK),
                  pl.BlockSpec((1,) + lbl3.shape[1:], lambda h: (h, 0, 0)),
                  pl.BlockSpec((1, HG_DV), lambda h: (0, 0)),
                  st_spec],
        out_specs=[pl.BlockSpec((bsz, HG_DV), lambda h: (0, h)), st_spec],
        out_shape=[jax.ShapeDtypeStruct((bsz, HG_H * HG_DV), F32),
                   jax.ShapeDtypeStruct(state.shape, F32)],
        scratch_shapes=[pltpu.VMEM((bsz, HG_DV), F32)],
        compiler_params=_cparams(("arbitrary",)),
        name="hgrn_decode",
    )(p2d, p2d, p2d, p2d, lbl3, gain.reshape(1, HG_DV), state)


def _gla_decode_kernel(q_ref, k_ref, v_ref, r_ref, a_ref, wa_ref, ba_ref, gain_ref, s_ref, o_ref, so_ref, o_scr):
    gate = jnp.dot(a_ref[...].astype(BF16), wa_ref[...], preferred_element_type=F32) + ba_ref[...]
    logf = _log_sigmoid(gate) * (1.0 / GLA_TAU)
    _decode_update(q_ref[...] * (GLA_DK ** -0.5), k_ref[...], v_ref[...], logf, s_ref, so_ref, o_scr)
    o_ref[...] = _rms(o_scr[...], gain_ref[...]) * _silu(r_ref[...])


def _gla_decode(p2d, wa2p, ba, gain, state):
    bsz = p2d.shape[0]
    hk = GLA_H * GLA_DK
    st_spec = pl.BlockSpec((bsz, 1, GLA_DK, GLA_DV), lambda h: (0, h, 0, 0))
    return pl.pallas_call(
        _gla_decode_kernel,
        grid=(GLA_H,),
        in_specs=[pl.BlockSpec((bsz, GLA_DK), lambda h: (0, OD_Q // GLA_DK + h)),
                  pl.BlockSpec((bsz, GLA_DK), lambda h: (0, OD_K // GLA_DK + h)),
                  pl.BlockSpec((bsz, GLA_DV), lambda h: (0, OD_V // GLA_DV + h)),
                  pl.BlockSpec((bsz, GLA_DV), lambda h: (0, OD_R // GLA_DV + h)),
                  pl.BlockSpec((bsz, LANES), lambda h: (0, OD_A // LANES)),
                  pl.BlockSpec((LANES, GLA_DK), lambda h: (0, h)),
                  pl.BlockSpec((1, GLA_DK), lambda h: (0, h)),
                  pl.BlockSpec((1, GLA_DV), lambda h: (0, 0)),
                  st_spec],
        out_specs=[pl.BlockSpec((bsz, GLA_DV), lambda h: (0, h)), st_spec],
        out_shape=[jax.ShapeDtypeStruct((bsz, GLA_H * GLA_DV), F32),
                   jax.ShapeDtypeStruct(state.shape, F32)],
        scratch_shapes=[pltpu.VMEM((bsz, GLA_DV), F32)],
        compiler_params=_cparams(("arbitrary",)),
        name="gla_decode",
    )(p2d, p2d, p2d, p2d, p2d, wa2p, ba.reshape(1, hk), gain.reshape(1, GLA_DV), state)


def _compress_kernel(xa_ref, xb_ref, pe_ref, w1_ref, w2_ref, o_ref, acc_ref):
    nb = o_ref.shape[0]
    acc_ref[...] = jnp.zeros_like(acc_ref)
    for l in range(NSA_BLK):
        rows = pl.ds(l, nb, stride=NSA_BLK)
        x = jnp.concatenate([xa_ref[rows, :], xb_ref[rows, :]], axis=1) + pe_ref[l:l + 1, :]
        acc_ref[...] += jnp.dot(x.astype(BF16), w1_ref[l], preferred_element_type=F32)
    o_ref[...] = jnp.dot(_silu(acc_ref[...]).astype(BF16), w2_ref[...], preferred_element_type=F32)


def _compress_prompt(p2d, pe4, w1bd, w2bd):
    n = p2d.shape[0]
    rc = min(8192, n)
    cw = 4 * NSA_DH
    c0 = PM_CMP // LANES
    return pl.pallas_call(
        _compress_kernel,
        grid=(n // rc,),
        in_specs=[pl.BlockSpec((rc, LANES), lambda i: (i, c0)),
                  pl.BlockSpec((rc, LANES), lambda i: (i, c0 + 1)),
                  pl.BlockSpec(pe4.shape, lambda i: (0, 0)),
                  pl.BlockSpec(w1bd.shape, lambda i: (0, 0, 0)),
                  pl.BlockSpec((cw, cw), lambda i: (0, 0))],
        out_specs=pl.BlockSpec((rc // NSA_BLK, cw), lambda i: (i, 0)),
        out_shape=jax.ShapeDtypeStruct((n // NSA_BLK, cw), F32),
        scratch_shapes=[pltpu.VMEM((rc // NSA_BLK, cw), F32)],
        compiler_params=_cparams(("arbitrary",)),
        name="nsa_compress",
    )(p2d, p2d, pe4, w1bd, w2bd)


AQB = 256
AQB_SHIFT = 8
AUG = 128
ONES_ROWS = 16


def _slopes_lane(g, lanes):
    r = _iota((1, lanes), 1) >> AQB_SHIFT
    out = jnp.zeros((1, lanes), F32)
    for rr in range(NSA_R):
        out = jnp.where(r == rr, 2.0 ** (-(g * NSA_R + rr + 1)), out)
    return out


def _build_key_features(src_ref, row0, dst_ref, g, seq, with_onehot):
    r = _iota((NSA_DH, LANES), 0)
    for cb in range(seq // LANES):
        kpos = cb * LANES + _iota((NSA_DH, LANES), 1)
        feat = jnp.where(r == 32, (kpos >> 6).astype(F32),
                         jnp.where(r == 33, (kpos & 63).astype(F32),
                                   jnp.where((r == 34) | (r == 35), 1.0, 0.0)))
        if with_onehot:
            feat = jnp.where((kpos >> 6) == r, 1.0, feat)
        kt = src_ref[0, row0:row0 + NSA_DH, cb * LANES:(cb + 1) * LANES]
        dst_ref[g, cb * LANES:(cb + 1) * LANES, :] = jnp.concatenate([kt, feat], axis=0).T.astype(BF16)


def _attn_tile(k_ref, g, k0, v_ref, v_row0, qa_ref, mask, m_ref, acc_ref, stream, nkeys=AQB):
    s = jnp.dot(k_ref[g, pl.ds(k0, nkeys), :], qa_ref[g], preferred_element_type=F32)
    if mask is not None:
        s = jnp.where(mask, s, NEG)
    m_old = m_ref[stream]
    m_new = jnp.maximum(m_old, jnp.max(s, axis=0, keepdims=True))
    p = jnp.exp(s - m_new).astype(BF16)
    vt = v_ref[0, v_row0:v_row0 + NSA_DH, pl.ds(k0, nkeys)].astype(BF16)
    vt = jnp.concatenate([vt, jnp.ones((ONES_ROWS, nkeys), BF16)], axis=0)
    acc_ref[stream] = jnp.exp(m_old - m_new) * acc_ref[stream] + jnp.dot(vt, p, preferred_element_type=F32)
    m_ref[stream] = m_new


def _nsa_attn_kernel(qt_ref, rows_ref, win_ref, gt_ref, kc_ref, vct_ref, o_ref, ks_s, kw_s, qa_s, m_s, acc_s,
                     *, nblk):
    qb = pl.program_id(1)
    seq = rows_ref.shape[2]
    q0 = qb * AQB
    nq = NSA_R * AQB
    half = NSA_G * NSA_DH

    @pl.when(qb == 0)
    def _():
        for g in range(NSA_G):
            _build_key_features(rows_ref, 2 * half + g * NSA_DH, ks_s, g, seq, True)
            _build_key_features(win_ref, g * NSA_DH, kw_s, g, seq, False)

    pq = q0 + (_iota((1, nq), 1) & (AQB - 1))
    gates = _sigmoid(gt_ref[0])
    ksel = min(NSA_TOPK, nblk)
    scale = NSA_DH ** -0.5
    key_i = _iota((AQB, nq), 0)
    qry_t = _iota((AQB, nq), 1) & (AQB - 1)
    causal = key_i <= qry_t
    beyond = key_i > qry_t
    pq_f = q0 + _iota((32, AQB), 1)
    feat_row = _iota((32, AQB), 0)

    m_s[...] = jnp.full(m_s.shape, NEG, F32)
    acc_s[...] = jnp.zeros_like(acc_s)

    def result(i):
        return acc_s[i, 0:NSA_DH, :] / jnp.maximum(acc_s[i, NSA_DH:NSA_DH + 1, :], TINY)

    o_cmp = []
    for g in range(NSA_G):
        slope = _slopes_lane(g, nq)

        qt_g = [qt_ref[0, (g * NSA_R + r) * NSA_DH:(g * NSA_R + r + 1) * NSA_DH, :] * scale for r in range(NSA_R)]
        qs_t = jnp.concatenate(qt_g, axis=1).astype(BF16)
        sc = jnp.dot(kc_ref[0, g], qs_t, preferred_element_type=F32)
        dist_c = pq - (_iota((nblk, nq), 0) * NSA_BLK + NSA_BLK - 1)
        mask_c = dist_c >= 0
        sc = jnp.where(mask_c, sc - slope * dist_c.astype(F32), NEG)
        pc = jnp.where(mask_c, jnp.exp(sc - jnp.max(sc, axis=0, keepdims=True)), 0.0)
        pc = pc / jnp.maximum(jnp.sum(pc, axis=0, keepdims=True), TINY)
        o_cmp.append(jnp.dot(vct_ref[0, g], pc.astype(BF16), preferred_element_type=F32))
        imp = pc[:, 0:AQB]
        for r in range(1, NSA_R):
            imp = imp + pc[:, r * AQB:(r + 1) * AQB]

        n_io = _iota((nblk, AQB), 0)
        tpos = q0 + _iota((nblk, AQB), 1)
        forced = (n_io == (tpos >> 6)) | (n_io == 0)
        started = n_io * NSA_BLK <= tpos
        score = jnp.where(forced, jnp.inf, jnp.where(started, imp, -jnp.inf))
        rank = jnp.zeros((nblk, AQB), I32)
        for i in range(nblk):
            row = score[i:i + 1, :]
            rank = rank + ((row > score) | ((row == score) & (i < n_io))).astype(I32)
        sel_bias = jnp.where((rank < ksel) & started, 0.0, NEG)
        if nblk < 32:
            sel_bias = jnp.concatenate([sel_bias, jnp.zeros((32 - nblk, AQB), F32)], axis=0)

        cols = []
        for r in range(NSA_R):
            sl = 2.0 ** (-(g * NSA_R + r + 1))
            pos_feat = jnp.where(feat_row == 0, 64.0 * sl,
                                 jnp.where(feat_row == 1, sl,
                                           jnp.where(feat_row == 2, (-64.0 * sl) * (pq_f >> 6).astype(F32),
                                                     jnp.where(feat_row == 3, (-sl) * (pq_f & 63).astype(F32), 0.0))))
            cols.append(jnp.concatenate([qt_g[r], sel_bias, pos_feat], axis=0))
        qa_s[g] = jnp.concatenate(cols, axis=1).astype(BF16)

    def slc_tile(k0, mask, nkeys=AQB):
        for g in range(NSA_G):
            _attn_tile(ks_s, g, k0, rows_ref, 3 * half + g * NSA_DH, qa_s, mask, m_s, acc_s, g, nkeys)

    def win_tile(k0, mask):
        for g in range(NSA_G):
            _attn_tile(kw_s, g, k0, win_ref, half + g * NSA_DH, qa_s, mask, m_s, acc_s, NSA_G + g)

    def slc_body(kt, c):
        slc_tile(pl.multiple_of(kt * 2 * AQB, 2 * AQB), None, 2 * AQB)
        return c

    lax.fori_loop(0, qb >> 1, slc_body, 0)

    @pl.when((qb & 1) == 1)
    def _():
        slc_tile(pl.multiple_of(q0 - AQB, AQB), None)

    @pl.when(qb >= 2)
    def _():
        win_tile(pl.multiple_of(q0 - 2 * AQB, AQB), beyond)

    @pl.when(qb >= 1)
    def _():
        win_tile(pl.multiple_of(q0 - AQB, AQB), None)

    slc_tile(pl.multiple_of(q0, AQB), causal)
    win_tile(pl.multiple_of(q0, AQB), causal)

    for g in range(NSA_G):
        def gate_row(j, g=g):
            return jnp.concatenate([gates[g * 12 + r * 3 + j:g * 12 + r * 3 + j + 1, :] for r in range(NSA_R)],
                                   axis=1)

        o_t = gate_row(0) * o_cmp[g] + gate_row(1) * result(g) + gate_row(2) * result(NSA_G + g)
        o_st = jnp.concatenate([o_t[:, r * AQB:(r + 1) * AQB] for r in range(NSA_R)], axis=0)
        o_ref[:, g * NSA_R * NSA_DH:(g + 1) * NSA_R * NSA_DH] = o_st.T.astype(o_ref.dtype)


def _nsa_attn_prompt(qt, rows_t, win_t, gate_t, kc, vct, bsz, seq):
    assert NSA_WINDOW == 2 * AQB and seq % AQB == 0 and seq // NSA_BLK <= 32
    nqb = seq // AQB
    nblk = seq // NSA_BLK
    qw = NSA_H * NSA_DH
    nq = NSA_R * AQB
    per_b = lambda a: pl.BlockSpec((1,) + a.shape[1:], lambda b, i: (b,) + (0,) * (a.ndim - 1))
    per_q = lambda a: pl.BlockSpec((1, a.shape[1], AQB), lambda b, i: (b, 0, i))
    return pl.pallas_call(
        functools.partial(_nsa_attn_kernel, nblk=nblk),
        grid=(bsz, nqb),
        in_specs=[per_q(qt), per_b(rows_t), per_b(win_t), per_q(gate_t), per_b(kc), per_b(vct)],
        out_specs=pl.BlockSpec((AQB, qw), lambda b, i: (b * nqb + i, 0)),
        out_shape=jax.ShapeDtypeStruct((bsz * seq, qw), BF16),
        scratch_shapes=[pltpu.VMEM((NSA_G, seq, AUG), BF16), pltpu.VMEM((NSA_G, seq, AUG), BF16),
                        pltpu.VMEM((NSA_G, AUG, nq), BF16),
                        pltpu.VMEM((2 * NSA_G, 1, nq), F32), pltpu.VMEM((2 * NSA_G, NSA_DH + ONES_ROWS, nq), F32)],
        compiler_params=_cparams(("parallel", "arbitrary")),
        name="nsa_attn",
    )(qt, rows_t, win_t, gate_t, kc, vct)


QROWS = 16


def _head_slopes(rows):
    r = _iota((rows, 1), 0)
    out = jnp.zeros((rows, 1), F32)
    for h in range(NSA_H):
        out = jnp.where(r == h, 2.0 ** (-(h + 1)), out)
    return out


def _softmax_rows(s, valid):
    s = jnp.where(valid, s, NEG)
    p = jnp.where(valid, jnp.exp(s - jnp.max(s, axis=1, keepdims=True)), 0.0)
    return p / jnp.maximum(jnp.sum(p, axis=1, keepdims=True), TINY)


def _padded_queries(q_ref):
    q = q_ref[0] * (NSA_DH ** -0.5)
    return jnp.concatenate([q, jnp.zeros((QROWS - NSA_H, NSA_DH), F32)], axis=0).astype(BF16)


PLANES = 4 * NSA_G
PAGE_ROWS = PLANES * NSA_DH
CMP_PLANES = 2 * NSA_G


def _dec_cmp_kernel(pt_ref, cache_ref, pe_ref, w1_ref, w2_ref, q_ref, o_ref, sel_ref, buf0, buf1, acc_ref, cmp_ref, sem,
                    *, npages, nsel):
    b = pl.program_id(0)
    nb = npages * (PAGE // NSA_BLK)
    nrow = npages * CMP_PLANES
    crows = CMP_PLANES * NSA_DH
    bufs = (buf0, buf1)
    nseq = 2 * pl.num_programs(0)

    def page_copy(seq, pg, slot):
        src = pl.multiple_of(pt_ref[seq * npages + pg] * PAGE_ROWS, PAGE_ROWS)
        return pltpu.make_async_copy(cache_ref.at[pl.ds(src, crows), :],
                                     bufs[slot].at[pl.ds(pl.multiple_of(pg * crows, crows), crows), :],
                                     sem.at[slot])

    def start_all(seq, slot):
        def start(pg, c):
            page_copy(seq, pg, slot).start()
            return c
        lax.fori_loop(0, npages, start, 0)

    def wait_all(seq, slot):
        def wait(pg, c):
            page_copy(seq, pg, slot).wait()
            return c
        lax.fori_loop(0, npages, wait, 0)

    @pl.when(b == 0)
    def _():
        start_all(0, 0)
        start_all(1, 1)

    is_k = ((_iota((nrow, 1), 0) >> 1) & 1) == 0
    pos = nb * NSA_BLK
    slope = _head_slopes(QROWS)
    row_grp = _iota((QROWS, 1), 0) >> 2
    lane = _iota((1, nb), 1)
    blk_of = jnp.where(lane < npages, 2 * lane, 2 * (lane - npages) + 1)
    dist = (pos - (blk_of * NSA_BLK + NSA_BLK - 1)).astype(F32)
    ri, ci = _iota((nb, nb), 0), _iota((nb, nb), 1)
    blk_r = jnp.where(ri < npages, 2 * ri, 2 * (ri - npages) + 1)
    blk_c = jnp.where(ci < npages, 2 * ci, 2 * (ci - npages) + 1)

    for slot in range(2):
        seq = 2 * b + slot
        wait_all(seq, slot)
        acc_ref[...] = jnp.zeros_like(acc_ref)
        for d in range(NSA_DH):
            x = bufs[slot][pl.ds(d, nrow, stride=NSA_DH), :]
            x = x + jnp.where(is_k, pe_ref[d, 0:1, :], pe_ref[d, 1:2, :])
            acc_ref[...] += jnp.dot(x.astype(BF16), w1_ref[d], preferred_element_type=F32)

        @pl.when(seq + 2 < nseq)
        def _(seq=seq, slot=slot):
            start_all(seq + 2, slot)

        acc = acc_ref[...]
        h = _silu(jnp.where(is_k, acc[:, :LANES], acc[:, LANES:]))
        c2 = jnp.dot(h.astype(BF16), w2_ref[...], preferred_element_type=F32)
        cmp_ref[...] = jnp.where(is_k, c2[:, :LANES], c2[:, LANES:])

        q = q_ref[slot] * (NSA_DH ** -0.5)
        q16 = jnp.concatenate([q, jnp.zeros((QROWS - NSA_H, NSA_DH), F32)], axis=0).astype(BF16)
        o_all = jnp.zeros((QROWS, NSA_DH), F32)
        for g in range(NSA_G):
            kc = cmp_ref[pl.ds(g, npages, stride=CMP_PLANES), :].astype(BF16)
            vc = cmp_ref[pl.ds(NSA_G + g, npages, stride=CMP_PLANES), :].astype(BF16)
            s = jnp.concatenate([lax.dot_general(q16, kc[:, j * NSA_DH:(j + 1) * NSA_DH], NT_DIMS,
                                                 preferred_element_type=F32) for j in range(2)], axis=1)
            p = _softmax_rows(s - slope * dist, dist >= 0)
            pb = p.astype(BF16)
            o_g = sum(jnp.dot(pb[:, j * npages:(j + 1) * npages], vc[:, j * NSA_DH:(j + 1) * NSA_DH],
                              preferred_element_type=F32) for j in range(2))
            o_all = jnp.where(row_grp == g, o_g, o_all)
            imp = jnp.sum(jnp.where(row_grp == g, p, 0.0), axis=0, keepdims=True)
            score_row = jnp.where(blk_of == 0, jnp.inf, imp)
            score_col = jnp.sum(jnp.where(ri == ci, jnp.broadcast_to(score_row, (nb, nb)), 0.0),
                                axis=1, keepdims=True)
            beats = (score_col > score_row) | ((score_col == score_row) & (blk_r < blk_c))
            rank = jnp.sum(beats.astype(I32), axis=0, keepdims=True)
            sel_ref[slot, g:g + 1, :] = (rank < nsel).astype(F32)
        o_ref[slot] = o_all[:NSA_H]


def _dec_cmp(page_table, cache2, pe_dec, w1dec, w2dec, q3, nsel):
    bsz, npages = page_table.shape
    nb = npages * (PAGE // NSA_BLK)
    nrow = npages * CMP_PLANES
    assert bsz % 2 == 0
    grid_spec = pltpu.PrefetchScalarGridSpec(
        num_scalar_prefetch=1,
        grid=(bsz // 2,),
        in_specs=[pl.BlockSpec(memory_space=pl.ANY),
                  pl.BlockSpec(pe_dec.shape, lambda b, pt: (0, 0, 0)),
                  pl.BlockSpec(w1dec.shape, lambda b, pt: (0, 0, 0)),
                  pl.BlockSpec(w2dec.shape, lambda b, pt: (0, 0)),
                  pl.BlockSpec((2, NSA_H, NSA_DH), lambda b, pt: (b, 0, 0))],
        out_specs=[pl.BlockSpec((2, NSA_H, NSA_DH), lambda b, pt: (b, 0, 0)),
                   pl.BlockSpec((2, NSA_G, nb), lambda b, pt: (b, 0, 0))],
        scratch_shapes=[pltpu.VMEM((nrow * NSA_DH, LANES), F32), pltpu.VMEM((nrow * NSA_DH, LANES), F32),
                        pltpu.VMEM((nrow, 2 * LANES), F32), pltpu.VMEM((nrow, LANES), F32),
                        pltpu.SemaphoreType.DMA((2,))])
    return pl.pallas_call(
        functools.partial(_dec_cmp_kernel, npages=npages, nsel=nsel),
        grid_spec=grid_spec,
        out_shape=[jax.ShapeDtypeStruct((bsz, NSA_H, NSA_DH), F32),
                   jax.ShapeDtypeStruct((bsz, NSA_G, nb), F32)],
        compiler_params=_cparams(("arbitrary",)),
        name="nsa_dec_cmp",
    )(page_table.reshape(-1), cache2, pe_dec, w1dec, w2dec, q3)


def _attend_with_self(q16, slope, kt, vt, dist, valid, k_self, v_self):
    s = jnp.dot(q16, kt.astype(BF16), preferred_element_type=F32) - slope * dist
    s_self = jnp.sum(q16.astype(F32) * k_self.astype(BF16).astype(F32), axis=1, keepdims=True)
    m = jnp.maximum(jnp.max(jnp.where(valid, s, NEG), axis=1, keepdims=True), s_self)
    p = jnp.where(valid, jnp.exp(s - m), 0.0)
    p_self = jnp.exp(s_self - m)
    num = lax.dot_general(p.astype(BF16), vt.astype(BF16), NT_DIMS, preferred_element_type=F32) + p_self * v_self
    return num / jnp.maximum(jnp.sum(p, axis=1, keepdims=True) + p_self, TINY)


def _dec_attn_kernel(pt_ref, idx_ref, cache_ref, q_ref, oc_ref, kn_ref, win_ref, gt_ref, o_ref, kbuf, vbuf, sem,
                     *, npages, nsel, wlen):
    b = pl.program_id(0)
    pos = npages * PAGE

    copies = []
    for g in range(NSA_G):
        for j in range(nsel):
            blk = idx_ref[(b * NSA_G + g) * NSA_TOPK + j]
            base = pt_ref[b * npages + (blk >> 1)] * PAGE_ROWS
            dst = pl.ds(j * PAGE, PAGE)
            k_rows = pl.ds(pl.multiple_of(base + (2 * NSA_G + g) * NSA_DH, NSA_DH), NSA_DH)
            v_rows = pl.ds(pl.multiple_of(base + (3 * NSA_G + g) * NSA_DH, NSA_DH), NSA_DH)
            copies.append(pltpu.make_async_copy(cache_ref.at[k_rows, :], kbuf.at[g, :, dst], sem.at[0]))
            copies.append(pltpu.make_async_copy(cache_ref.at[v_rows, :], vbuf.at[g, :, dst], sem.at[0]))
    for cp in copies:
        cp.start()
    for cp in copies:
        cp.wait()

    q16 = _padded_queries(q_ref)
    slope = _head_slopes(QROWS)
    row_grp = _iota((QROWS, 1), 0) >> 2
    kn = kn_ref[0]
    lane = _iota((1, nsel * PAGE), 1)
    within = lane & (PAGE - 1)
    wdist = wlen - _iota((1, wlen), 1)
    o_slc = jnp.zeros((QROWS, NSA_DH), F32)
    o_win = jnp.zeros((QROWS, NSA_DH), F32)
    for g in range(NSA_G):
        gs = slice(g * NSA_DH, (g + 1) * NSA_DH)
        blk_lane = jnp.zeros((1, nsel * PAGE), I32)
        for j in range(nsel):
            blk_lane = jnp.where((lane >> 7) == j, idx_ref[(b * NSA_G + g) * NSA_TOPK + j], blk_lane)
        valid = (within >> 6) == (blk_lane & 1)
        kpos = (blk_lane >> 1) * PAGE + within
        o_g = _attend_with_self(q16, slope, kbuf[g], vbuf[g], (pos - kpos).astype(F32), valid,
                                kn[0:1, gs], kn[1:2, gs])
        o_slc = jnp.where(row_grp == g, o_g, o_slc)
        o_g = _attend_with_self(q16, slope, win_ref[0, g * NSA_DH:(g + 1) * NSA_DH, :],
                                win_ref[0, (NSA_G + g) * NSA_DH:(NSA_G + g + 1) * NSA_DH, :],
                                wdist.astype(F32), wdist < NSA_WINDOW, kn[2:3, gs], kn[3:4, gs])
        o_win = jnp.where(row_grp == g, o_g, o_win)

    gates = _sigmoid(gt_ref[pl.ds(b, 1), :])
    r_io = _iota((QROWS, LANES), 0)
    l_io = _iota((QROWS, LANES), 1)

    def gate_col(j):
        return jnp.sum(jnp.where(l_io == 3 * r_io + j, gates, 0.0), axis=1, keepdims=True)

    o_cmp = jnp.concatenate([oc_ref[0], jnp.zeros((QROWS - NSA_H, NSA_DH), F32)], axis=0)
    o = gate_col(0) * o_cmp + gate_col(1) * o_slc + gate_col(2) * o_win
    o_ref[0] = o[:NSA_H]


def _dec_attn(page_table, idx, cache2, q3, ocmp, knew, win_t, gates, nsel):
    bsz, npages = page_table.shape
    wlen = win_t.shape[2]
    hd = pl.BlockSpec((1, NSA_H, NSA_DH), lambda b, pt, ix: (b, 0, 0))
    grid_spec = pltpu.PrefetchScalarGridSpec(
        num_scalar_prefetch=2,
        grid=(bsz,),
        in_specs=[pl.BlockSpec(memory_space=pl.ANY), hd, hd,
                  pl.BlockSpec((1,) + knew.shape[1:], lambda b, pt, ix: (b, 0, 0)),
                  pl.BlockSpec((1,) + win_t.shape[1:], lambda b, pt, ix: (b, 0, 0)),
                  pl.BlockSpec(gates.shape, lambda b, pt, ix: (0, 0))],
        out_specs=hd,
        scratch_shapes=[pltpu.VMEM((NSA_G, NSA_DH, nsel * PAGE), F32), pltpu.VMEM((NSA_G, NSA_DH, nsel * PAGE), F32),
                        pltpu.SemaphoreType.DMA((1,))])
    return pl.pallas_call(
        functools.partial(_dec_attn_kernel, npages=npages, nsel=nsel, wlen=wlen),
        grid_spec=grid_spec,
        out_shape=jax.ShapeDtypeStruct((bsz, NSA_H, NSA_DH), F32),
        compiler_params=_cparams(("arbitrary",)),
        name="nsa_dec_attn",
    )(page_table.reshape(-1), idx.reshape(-1), cache2, q3, ocmp, knew, win_t, gates)


def _prep_weights(w_in_even, nsa_cmp_pos, nsa_cmp_w1, nsa_cmp_w2, w_out_even, w_in_odd, gla_wa2, w_out_odd,
                  ffn_w13, ffn_w2):
    d = D_MODEL
    wie = jnp.pad(w_in_even[0], ((0, 0), (0, PROJ_W - w_in_even.shape[2]))).astype(BF16)
    a0 = 2 * GLA_H * GLA_DK + GLA_H * GLA_DV
    wo = w_in_odd[0]
    wio = jnp.concatenate([wo[:, :a0], wo[:, a0 + GLA_RANK:], wo[:, a0:a0 + GLA_RANK]], axis=1)
    wio = jnp.pad(wio, ((0, 0), (0, ODD_W - wio.shape[1]))).astype(BF16)
    wa2p = jnp.pad(gla_wa2[0], ((0, LANES - GLA_RANK), (0, 0))).astype(BF16)
    hv = HG_H * HG_DV
    woe_h, woe_n = w_out_even[0, :hv].astype(BF16), w_out_even[0, hv:].astype(BF16)
    woo = w_out_odd[0].astype(BF16)
    nj = D_FF // FF_TILE
    w13t, w2b = [], []
    for l in range(ffn_w13.shape[0]):
        gate = ffn_w13[l, :, :D_FF].reshape(d, nj, FF_TILE)
        up = ffn_w13[l, :, D_FF:].reshape(d, nj, FF_TILE)
        w13t.append(jnp.concatenate([gate, up], axis=2).reshape(d, 2 * D_FF).astype(BF16))
        w2b.append(ffn_w2[l].astype(BF16))
    w1 = nsa_cmp_w1[0].reshape(2, NSA_BLK, NSA_DH, NSA_DH)
    w2 = nsa_cmp_w2[0]
    cw = 4 * NSA_DH
    eye = jnp.eye(2 * NSA_G, dtype=F32).reshape(2, NSA_G, 2, NSA_G)
    w1bd = jnp.einsum("cldj,cgCG->lcgdCGj", w1, eye).reshape(NSA_BLK, cw, cw)
    w2bd = jnp.einsum("cjd,cgCG->cgjCGd", w2, eye).reshape(cw, cw)
    pe = nsa_cmp_pos[0]
    pe4 = jnp.concatenate([pe[0], pe[0], pe[1], pe[1]], axis=1)
    we = w_in_even[0]
    hgw = 2 * HG_H * HG_DK + 2 * HG_H * HG_DV
    nq0 = hgw + NSA_H * NSA_DH
    w_main = jnp.concatenate([we[:, :hgw], we[:, nq0:nq0 + cw]], axis=1).astype(BF16)
    w_tr = jnp.pad(we[:, hgw:].T, ((0, T_END - (we.shape[1] - hgw)), (0, 0))).astype(BF16)
    eye2 = jnp.eye(2, dtype=F32)
    w1dec = jnp.einsum("cldo,jJ->djlcJo", w1, eye2).reshape(NSA_DH, 2 * NSA_BLK, 4 * NSA_DH)
    w2dec = jnp.einsum("cod,jJ->jocJd", w2, eye2).reshape(2 * NSA_DH, 4 * NSA_DH)
    pe_dec = jnp.tile(pe.transpose(2, 0, 1), (1, 1, 2))
    return dict(wie=wie, wio=wio, wa2p=wa2p, woe_h=woe_h, woe_n=woe_n, woo=woo, w13t=w13t, w2b=w2b,
                w1bd=w1bd.astype(BF16), w2bd=w2bd.astype(BF16), pe4=pe4, w_main=w_main, w_tr=w_tr,
                w1dec=w1dec.astype(BF16), w2dec=w2dec.astype(BF16), pe_dec=pe_dec)


def _forward_prompt(x, pw, hg_lb_logits, hg_norm, gla_ba, gla_norm, norm_mix, norm_ffn, norm_final):
    bsz, seq, d = x.shape
    n = bsz * seq
    x2d = x.reshape(n, d)
    tm = 512 if seq % 512 == 0 else 256
    tmf = next(t for t in (512, 256) if n % t == 0)
    tb = tm

    p, qt, rows_t, win_t, gate_t = _norm_proj_t(x2d, norm_mix[0], pw["w_main"], pw["w_tr"], bsz, seq, tm)
    o_h, hg_state = _hgrn_prompt(p, hg_lb_logits, hg_norm[0], bsz, seq, tb)
    cmp = _compress_prompt(p, pw["pe4"], pw["w1bd"], pw["w2bd"])
    nblk = seq // NSA_BLK
    cmp4 = cmp.reshape(bsz, nblk, 2 * NSA_G, NSA_DH).transpose(0, 2, 1, 3)
    kc = cmp4[:, :NSA_G].astype(BF16)
    vct = cmp4[:, NSA_G:].transpose(0, 1, 3, 2).astype(BF16)
    o_n = _nsa_attn_prompt(qt, rows_t, win_t, gate_t, kc, vct, bsz, seq)
    x1 = _out_ffn([o_h, o_n], x2d, [pw["woe_h"], pw["woe_n"]], norm_ffn[0], pw["w13t"][0], pw["w2b"][0], None, tmf)

    rows = rows_t.reshape(bsz, 4, NSA_G, NSA_DH, seq).transpose(0, 4, 1, 2, 3)[None]
    wn = min(NSA_WINDOW, seq)
    win = win_t[:, :, seq - wn:].reshape(bsz, 2, NSA_G, NSA_DH, wn).transpose(0, 4, 1, 2, 3)[None]

    p2 = _norm_proj(x1, norm_mix[1], pw["wio"], tm, ODD_W)
    o_g, gla_state = _gla_prompt(p2, pw["wa2p"], gla_ba[0], gla_norm[0], bsz, seq, tb)
    y = _out_ffn([o_g], x1, [pw["woo"]], norm_ffn[1], pw["w13t"][1], pw["w2b"][1], norm_final, tmf)
    return y.reshape(bsz, seq, d), rows, win, hg_state[None], gla_state[None]


def _forward_sample(x, pw, cache_nsa_kv, cache_win_kv, state_hgrn, state_gla, page_table, hg_lb_logits, hg_norm,
                    gla_ba, gla_norm, norm_mix, norm_ffn, norm_final):
    bsz, seq, d = x.shape
    x2d = x.reshape(bsz, d)
    npages = page_table.shape[1]
    nblk_total = npages * (PAGE // NSA_BLK) + 1
    nsel = min(NSA_TOPK, nblk_total) - 1

    p = _norm_proj(x2d, norm_mix[0], pw["wie"], bsz, PROJ_W // 2)
    o_h, hg_state = _hgrn_decode(p, hg_lb_logits, hg_norm[0], state_hgrn[0])

    cache2 = cache_nsa_kv[0].transpose(0, 2, 3, 4, 1).reshape(cache_nsa_kv.shape[1] * PAGE_ROWS, PAGE)
    q3 = p[:, EV_NQ:EV_CMP].reshape(bsz, NSA_H, NSA_DH)
    ocmp, sel = _dec_cmp(page_table, cache2, pw["pe_dec"], pw["w1dec"], pw["w2dec"], q3, nsel)
    sel = sel.reshape(bsz, NSA_G, 2, npages).transpose(0, 1, 3, 2).reshape(bsz, NSA_G, 2 * npages)
    idx = jnp.argsort(-sel, axis=-1, stable=True)[..., :NSA_TOPK].astype(I32)
    knew = p[:, EV_SLC:EV_GATE].reshape(bsz, 4, NSA_G * NSA_DH)
    wlen = cache_win_kv.shape[2]
    win_t = cache_win_kv[0].transpose(0, 2, 3, 4, 1).reshape(bsz, 2 * NSA_G * NSA_DH, wlen)
    o_n = _dec_attn(page_table, idx, cache2, q3, ocmp, knew, win_t, p[:, EV_GATE:EV_GATE + LANES], nsel)
    o_n = o_n.reshape(bsz, NSA_H * NSA_DH)
    x1 = _out_ffn([o_h, o_n], x2d, [pw["woe_h"], pw["woe_n"]], norm_ffn[0], pw["w13t"][0], pw["w2b"][0], None, bsz)

    rows = p[:, EV_CMP:EV_WIN].reshape(1, bsz, 1, 4, NSA_G, NSA_DH)
    win_new = p[:, EV_WIN:EV_GATE].reshape(1, bsz, 1, 2, NSA_G, NSA_DH)
    wk = jnp.concatenate([cache_win_kv[:1], win_new], axis=2)
    wn = min(NSA_WINDOW, npages * PAGE + 1)
    win = wk[:, :, wk.shape[2] - wn:]

    p2 = _norm_proj(x1, norm_mix[1], pw["wio"], bsz, ODD_W)
    o_g, gla_state = _gla_decode(p2, pw["wa2p"], gla_ba[0], gla_norm[0], state_gla[0])
    y = _out_ffn([o_g], x1, [pw["woo"]], norm_ffn[1], pw["w13t"][1], pw["w2b"][1], norm_final, bsz)
    return y.reshape(bsz, 1, d), rows, win, hg_state[None], gla_state[None]


def kernel(x_prompt, x_sample, cache_nsa_kv, cache_win_kv, state_hgrn, state_gla, page_table, w_in_even,
           hg_lb_logits, hg_norm, nsa_cmp_pos, nsa_cmp_w1, nsa_cmp_w2, w_out_even, w_in_odd, gla_wa2, gla_ba,
           gla_norm, w_out_odd, norm_mix, norm_ffn, norm_final, ffn_w13, ffn_w2):
    pw = _prep_weights(w_in_even, nsa_cmp_pos, nsa_cmp_w1, nsa_cmp_w2, w_out_even, w_in_odd, gla_wa2, w_out_odd,
                       ffn_w13, ffn_w2)
    y_p, kv_p, win_p, hg_p, gla_p = _forward_prompt(x_prompt, pw, hg_lb_logits, hg_norm, gla_ba, gla_norm,
                                                    norm_mix, norm_ffn, norm_final)
    y_s, kv_s, win_s, hg_s, gla_s = _forward_sample(x_sample, pw, cache_nsa_kv, cache_win_kv, state_hgrn,
                                                    state_gla, page_table, hg_lb_logits, hg_norm, gla_ba,
                                                    gla_norm, norm_mix, norm_ffn, norm_final)
    return (y_p, y_s, kv_p, kv_s, win_p, win_s, hg_p, hg_s, gla_p, gla_s)
```

```python
import functools

import jax
import jax.numpy as jnp
from jax import lax
from jax.experimental import pallas as pl
from jax.experimental.pallas import tpu as pltpu

F32 = jnp.float32
BF16 = jnp.bfloat16
I32 = jnp.int32

D_MODEL = 1024
HG_H, HG_DK, HG_DV = 4, 128, 128
NSA_H, NSA_DH, NSA_G, NSA_R = 8, 64, 2, 4
NSA_BLK = 64
NSA_TOPK = 16
NSA_WINDOW = 512
NSA_QB = 128
GLA_H, GLA_DK, GLA_DV = 4, 128, 256
GLA_RANK = 16
GLA_TAU = 16.0
D_FF = 2816
EPS = 1e-6
NEG = -1e30
TINY = 1e-30
PAGE = 128

PROJ_W = 3584
ODD_W = 3200
EV_Q, EV_F, EV_I, EV_OG = 0, 512, 1024, 1536
EV_NQ, EV_CMP, EV_SLC, EV_WIN, EV_GATE = 2048, 2560, 2816, 3072, 3328
OD_Q, OD_K, OD_V, OD_R, OD_A = 0, 512, 1024, 2048, 3072

LANES = 128
V7X_VMEM_LIMIT = 56 * 1024 * 1024
FF_TILE = 1408
CHUNK = 128
SUB = 16

NT_DIMS = (((1,), (1,)), ((), ()))
TN_DIMS = (((0,), (0,)), ((), ()))


def _cparams(sem):
    return pltpu.CompilerParams(dimension_semantics=sem, vmem_limit_bytes=V7X_VMEM_LIMIT)


def _rms(x, g):
    return x * lax.rsqrt(jnp.mean(x * x, axis=-1, keepdims=True) + EPS) * g


def _sigmoid(x):
    return 0.5 * jnp.tanh(0.5 * x) + 0.5


def _silu(x):
    return x * _sigmoid(x)


def _iota(shape, dim):
    return lax.broadcasted_iota(I32, shape, dim)


def _norm_proj_kernel(x_ref, g_ref, w_ref, o_ref, xn_ref):
    @pl.when(pl.program_id(1) == 0)
    def _():
        xn_ref[...] = _rms(x_ref[...], g_ref[...]).astype(BF16)

    o_ref[...] = jnp.dot(xn_ref[...], w_ref[...], preferred_element_type=F32)


def _norm_proj(x2d, gain, w, tm, tn):
    m, k = x2d.shape
    n = w.shape[1]
    return pl.pallas_call(
        _norm_proj_kernel,
        grid=(m // tm, n // tn),
        in_specs=[pl.BlockSpec((tm, k), lambda i, j: (i, 0)),
                  pl.BlockSpec((1, k), lambda i, j: (0, 0)),
                  pl.BlockSpec((k, tn), lambda i, j: (0, j))],
        out_specs=pl.BlockSpec((tm, tn), lambda i, j: (i, j)),
        out_shape=jax.ShapeDtypeStruct((m, n), F32),
        scratch_shapes=[pltpu.VMEM((tm, k), BF16)],
        compiler_params=_cparams(("parallel", "arbitrary")),
        name="norm_proj",
    )(x2d, gain.reshape(1, k), w)


PM_W = 2304
PM_CMP = 2048
T_Q, T_ROWS, T_WIN, T_GATE, T_END = 0, 512, 1024, 1280, 1312


def _norm_proj_t_kernel(x_ref, g_ref, w_ref, wt_ref, o_ref, q_ref, rows_ref, win_ref, gate_ref):
    xn = _rms(x_ref[...], g_ref[...]).astype(BF16)
    o_ref[...] = jnp.dot(xn, w_ref[...], preferred_element_type=F32)
    t = lax.dot_general(wt_ref[...], xn, NT_DIMS, preferred_element_type=F32)
    q_ref[0] = t[T_Q:T_ROWS]
    rows_ref[0] = t[T_ROWS:T_WIN]
    win_ref[0] = t[T_WIN:T_GATE]
    gate_ref[0] = t[T_GATE:T_END]


def _norm_proj_t(x2d, gain, w, wt, bsz, seq, tm):
    k = x2d.shape[1]
    nt = seq // tm
    tr = lambda rows: pl.BlockSpec((1, rows, tm), lambda b, t: (b, 0, t))
    sizes = (T_ROWS - T_Q, T_WIN - T_ROWS, T_GATE - T_WIN, T_END - T_GATE)
    return pl.pallas_call(
        _norm_proj_t_kernel,
        grid=(bsz, nt),
        in_specs=[pl.BlockSpec((tm, k), lambda b, t: (b * nt + t, 0)),
                  pl.BlockSpec((1, k), lambda b, t: (0, 0)),
                  pl.BlockSpec(w.shape, lambda b, t: (0, 0)),
                  pl.BlockSpec(wt.shape, lambda b, t: (0, 0))],
        out_specs=[pl.BlockSpec((tm, PM_W), lambda b, t: (b * nt + t, 0))] + [tr(r) for r in sizes],
        out_shape=[jax.ShapeDtypeStruct((bsz * seq, PM_W), F32)]
        + [jax.ShapeDtypeStruct((bsz, r, seq), F32) for r in sizes],
        compiler_params=_cparams(("parallel", "arbitrary")),
        name="norm_proj_t",
    )(x2d, gain.reshape(1, k), w, wt)


def _out_ffn_kernel(*refs, n_mix, final_norm):
    mix_refs = refs[:n_mix]
    res_ref = refs[n_mix]
    wo_refs = refs[n_mix + 1:2 * n_mix + 1]
    g_ref, w13_ref, w2_ref = refs[2 * n_mix + 1:2 * n_mix + 4]
    pos = 2 * n_mix + 4
    gf_ref = refs[pos] if final_norm else None
    pos += 1 if final_norm else 0
    o_ref, x1_ref, h_ref, acc_ref = refs[pos:pos + 4]
    j = pl.program_id(1)

    @pl.when(j == 0)
    def _():
        x1 = res_ref[...]
        for m_ref, w_ref in zip(mix_refs, wo_refs):
            x1 = x1 + jnp.dot(m_ref[...].astype(BF16), w_ref[...], preferred_element_type=F32)
        x1_ref[...] = x1
        h_ref[...] = _rms(x1, g_ref[...]).astype(BF16)
        acc_ref[...] = jnp.zeros_like(acc_ref)

    gu = jnp.dot(h_ref[...], w13_ref[...], preferred_element_type=F32)
    act = _silu(gu[:, :FF_TILE]) * gu[:, FF_TILE:]
    acc_ref[...] += jnp.dot(act.astype(BF16), w2_ref[...], preferred_element_type=F32)

    @pl.when(j == pl.num_programs(1) - 1)
    def _():
        y = x1_ref[...] + acc_ref[...]
        if final_norm:
            y = _rms(y, gf_ref[...])
        o_ref[...] = y


def _out_ffn(mixes, res, wos, g_ffn, w13t, w2, g_final, tm):
    m, d = res.shape
    n_mix = len(mixes)
    nj = D_FF // FF_TILE
    final_norm = g_final is not None
    in_specs = [pl.BlockSpec((tm, mx.shape[1]), lambda i, j: (i, 0)) for mx in mixes]
    in_specs.append(pl.BlockSpec((tm, d), lambda i, j: (i, 0)))
    in_specs += [pl.BlockSpec(w.shape, lambda i, j: (0, 0)) for w in wos]
    in_specs += [pl.BlockSpec((1, d), lambda i, j: (0, 0)),
                 pl.BlockSpec((d, 2 * FF_TILE), lambda i, j: (0, j)),
                 pl.BlockSpec((FF_TILE, d), lambda i, j: (j, 0))]
    args = list(mixes) + [res] + list(wos) + [g_ffn.reshape(1, d), w13t, w2]
    if final_norm:
        in_specs.append(pl.BlockSpec((1, d), lambda i, j: (0, 0)))
        args.append(g_final.reshape(1, d))
    return pl.pallas_call(
        functools.partial(_out_ffn_kernel, n_mix=n_mix, final_norm=final_norm),
        grid=(m // tm, nj),
        in_specs=in_specs,
        out_specs=pl.BlockSpec((tm, d), lambda i, j: (i, 0)),
        out_shape=jax.ShapeDtypeStruct((m, d), F32),
        scratch_shapes=[pltpu.VMEM((tm, d), F32), pltpu.VMEM((tm, d), BF16), pltpu.VMEM((tm, d), F32)],
        compiler_params=_cparams(("parallel", "arbitrary")),
        name="out_ffn",
    )(*args)


def _gla_chunk(q, k, v, g, st):
    c = q.shape[0]
    tri = (_iota((c, c), 0) >= _iota((c, c), 1)).astype(BF16)
    g_hi = g.astype(BF16)
    g_r1 = g - g_hi.astype(F32)
    g_mid = g_r1.astype(BF16)
    g_lo = (g_r1 - g_mid.astype(F32)).astype(BF16)
    b = (jnp.dot(tri, g_hi, preferred_element_type=F32) + jnp.dot(tri, g_mid, preferred_element_type=F32)
         + jnp.dot(tri, g_lo, preferred_element_type=F32))
    o = lax.dot_general((q * jnp.exp(b)).astype(BF16), st.astype(BF16), NT_DIMS, preferred_element_type=F32)
    lane = _iota((SUB, c), 1)
    row = _iota((SUB, c), 0)
    hs = SUB // 2
    rows = []
    for blk in range(c // SUB):
        lo = blk * SUB
        b_i, q_i, k_i = b[lo:lo + SUB], q[lo:lo + SUB], k[lo:lo + SUB]
        a_top = jnp.zeros((hs, c), F32)
        a_bot = jnp.zeros((hs, c), F32)
        for s in range(SUB):
            if s < hs:
                e = jnp.exp(b_i - b_i[s:s + 1])
                a = jnp.sum(q_i * e * k_i[s:s + 1], axis=1, keepdims=True)
                a_top = jnp.where(lane[:hs] == lo + s, a[:hs], a_top)
                a_bot = jnp.where(lane[:hs] == lo + s, a[hs:], a_bot)
            else:
                e = jnp.exp(b_i[hs:] - b_i[s:s + 1])
                a = jnp.sum(q_i[hs:] * e * k_i[s:s + 1], axis=1, keepdims=True)
                a_bot = jnp.where(lane[:hs] == lo + s, a, a_bot)
        a_blk = jnp.where(lane <= lo + row, jnp.concatenate([a_top, a_bot], axis=0), 0.0)
        if blk > 0:
            r = b[lo - 1:lo]
            qt = q_i * jnp.exp(b_i - r)
            kt = k * jnp.exp(jnp.minimum(r - b, 0.0))
            a_off = lax.dot_general(qt.astype(BF16), kt.astype(BF16), NT_DIMS, preferred_element_type=F32)
            a_blk = jnp.where(lane < lo, a_off, a_blk)
        rows.append(a_blk)
    a_full = jnp.concatenate(rows, axis=0)
    o = o + jnp.dot(a_full.astype(BF16), v.astype(BF16), preferred_element_type=F32)
    b_last = b[c - 1:c]
    kd = k * jnp.exp(b_last - b)
    st_new = st * jnp.exp(b_last) + lax.dot_general(v.astype(BF16), kd.astype(BF16), TN_DIMS,
                                                    preferred_element_type=F32)
    return o, st_new


def _lower_bound(lbl_ref, col):
    l = lbl_ref[:, col:col + HG_DK]
    e = jnp.exp(l - jnp.max(l, axis=0, keepdims=True))
    return e[0:1] / jnp.sum(e, axis=0, keepdims=True)


def _hgrn_kernel(q_ref, f_ref, i_ref, og_ref, lbl_ref, gain_ref, o_ref, s_ref, st_ref, *, nch):
    t = pl.program_id(1)

    @pl.when(t == 0)
    def _():
        st_ref[...] = jnp.zeros_like(st_ref)

    def body(ci, carry):
        r0 = pl.multiple_of(ci * CHUNK, CHUNK)
        rs = pl.ds(r0, CHUNK)
        for h in range(HG_H):
            cs = slice(h * HG_DK, (h + 1) * HG_DK)
            lb = _lower_bound(lbl_ref, h * HG_DK)
            q = _silu(q_ref[rs, cs])
            f = lb + (1.0 - lb) * _sigmoid(f_ref[rs, cs])
            o, st_new = _gla_chunk(q, 1.0 - f, i_ref[rs, cs], jnp.log(f), st_ref[h])
            st_ref[h] = st_new
            o = _rms(o, gain_ref[...]) * _sigmoid(og_ref[rs, cs])
            o_ref[rs, cs] = o.astype(o_ref.dtype)
        return carry

    lax.fori_loop(0, nch, body, 0)

    @pl.when(t == pl.num_programs(1) - 1)
    def _():
        for h in range(HG_H):
            s_ref[0, h] = st_ref[h].T


def _hgrn_prompt(p2d, lb_logits, gain, bsz, seq, tb):
    nt = seq // tb
    hk = HG_H * HG_DK
    row = lambda b, t: b * nt + t
    col_spec = lambda c: pl.BlockSpec((tb, hk), lambda b, t, c=c: (row(b, t), c))
    return pl.pallas_call(
        functools.partial(_hgrn_kernel, nch=tb // CHUNK),
        grid=(bsz, nt),
        in_specs=[col_spec(EV_Q // hk), col_spec(EV_F // hk), col_spec(EV_I // hk), col_spec(EV_OG // hk),
                  pl.BlockSpec(lb_logits.shape, lambda b, t: (0, 0)),
                  pl.BlockSpec((1, HG_DV), lambda b, t: (0, 0))],
        out_specs=[pl.BlockSpec((tb, hk), lambda b, t: (row(b, t), 0)),
                   pl.BlockSpec((1, HG_H, HG_DK, HG_DV), lambda b, t: (b, 0, 0, 0))],
        out_shape=[jax.ShapeDtypeStruct((bsz * seq, hk), BF16),
                   jax.ShapeDtypeStruct((bsz, HG_H, HG_DK, HG_DV), F32)],
        scratch_shapes=[pltpu.VMEM((HG_H, HG_DV, HG_DK), F32)],
        compiler_params=_cparams(("parallel", "arbitrary")),
        name="hgrn_chunk",
    )(p2d, p2d, p2d, p2d, lb_logits, gain.reshape(1, HG_DV))


def _log_sigmoid(x):
    return jnp.minimum(x, 0.0) - jnp.log(1.0 + jnp.exp(-jnp.abs(x)))


def _gla_kernel(q_ref, k_ref, v_ref, r_ref, a_ref, wa_ref, ba_ref, gain_ref, o_ref, s_ref, st_ref, *, nch):
    t = pl.program_id(1)

    @pl.when(t == 0)
    def _():
        st_ref[...] = jnp.zeros_like(st_ref)

    def body(ci, carry):
        r0 = pl.multiple_of(ci * CHUNK, CHUNK)
        rs = pl.ds(r0, CHUNK)
        gate = jnp.dot(a_ref[rs, :].astype(BF16), wa_ref[...], preferred_element_type=F32) + ba_ref[...]
        logf = _log_sigmoid(gate) * (1.0 / GLA_TAU)
        for h in range(GLA_H):
            ks = slice(h * GLA_DK, (h + 1) * GLA_DK)
            vs = slice(h * GLA_DV, (h + 1) * GLA_DV)
            q = q_ref[rs, ks] * (GLA_DK ** -0.5)
            o, st_new = _gla_chunk(q, k_ref[rs, ks], v_ref[rs, vs], logf[:, ks], st_ref[h])
            st_ref[h] = st_new
            o = _rms(o, gain_ref[...]) * _silu(r_ref[rs, vs])
            o_ref[rs, vs] = o.astype(o_ref.dtype)
        return carry

    lax.fori_loop(0, nch, body, 0)

    @pl.when(t == pl.num_programs(1) - 1)
    def _():
        for h in range(GLA_H):
            s_ref[0, h] = st_ref[h].T


def _gla_prompt(p2d, wa2p, ba, gain, bsz, seq, tb):
    nt = seq // tb
    hk, hv = GLA_H * GLA_DK, GLA_H * GLA_DV
    row = lambda b, t: b * nt + t
    return pl.pallas_call(
        functools.partial(_gla_kernel, nch=tb // CHUNK),
        grid=(bsz, nt),
        in_specs=[pl.BlockSpec((tb, hk), lambda b, t: (row(b, t), OD_Q // hk)),
                  pl.BlockSpec((tb, hk), lambda b, t: (row(b, t), OD_K // hk)),
                  pl.BlockSpec((tb, hv), lambda b, t: (row(b, t), OD_V // hv)),
                  pl.BlockSpec((tb, hv), lambda b, t: (row(b, t), OD_R // hv)),
                  pl.BlockSpec((tb, LANES), lambda b, t: (row(b, t), OD_A // LANES)),
                  pl.BlockSpec(wa2p.shape, lambda b, t: (0, 0)),
                  pl.BlockSpec((1, hk), lambda b, t: (0, 0)),
                  pl.BlockSpec((1, GLA_DV), lambda b, t: (0, 0))],
        out_specs=[pl.BlockSpec((tb, hv), lambda b, t: (row(b, t), 0)),
                   pl.BlockSpec((1, GLA_H, GLA_DK, GLA_DV), lambda b, t: (b, 0, 0, 0))],
        out_shape=[jax.ShapeDtypeStruct((bsz * seq, hv), BF16),
                   jax.ShapeDtypeStruct((bsz, GLA_H, GLA_DK, GLA_DV), F32)],
        scratch_shapes=[pltpu.VMEM((GLA_H, GLA_DV, GLA_DK), F32)],
        compiler_params=_cparams(("parallel", "arbitrary")),
        name="gla_chunk",
    )(p2d, p2d, p2d, p2d, p2d, wa2p, ba.reshape(1, hk), gain.reshape(1, GLA_DV))


def _to_columns(x):
    bsz = x.shape[0]
    if bsz < LANES:
        x = jnp.concatenate([x, jnp.zeros((LANES - bsz, x.shape[1]), x.dtype)], axis=0)
    return x.T


def _decode_update(q, k, v, g, s_ref, so_ref, o_scr):
    bsz = q.shape[0]
    qt, kt, et = _to_columns(q), _to_columns(k), _to_columns(jnp.exp(g))
    for b in range(bsz):
        s_new = et[:, b:b + 1] * s_ref[b, 0] + kt[:, b:b + 1] * v[b:b + 1, :]
        so_ref[b, 0] = s_new
        o_scr[b:b + 1, :] = jnp.sum(qt[:, b:b + 1] * s_new, axis=0, keepdims=True)


def _hgrn_decode_kernel(q_ref, f_ref, i_ref, og_ref, lbl_ref, gain_ref, s_ref, o_ref, so_ref, o_scr):
    l = lbl_ref[0]
    e = jnp.exp(l - jnp.max(l, axis=0, keepdims=True))
    lb = e[0:1] / jnp.sum(e, axis=0, keepdims=True)
    f = lb + (1.0 - lb) * _sigmoid(f_ref[...])
    _decode_update(_silu(q_ref[...]), 1.0 - f, i_ref[...], jnp.log(f), s_ref, so_ref, o_scr)
    o_ref[...] = _rms(o_scr[...], gain_ref[...]) * _sigmoid(og_ref[...])


def _hgrn_decode(p2d, lb_logits, gain, state):
    bsz = p2d.shape[0]
    col = lambda c: pl.BlockSpec((bsz, HG_DK), lambda h, c=c: (0, c + h))
    lbl3 = lb_logits.reshape(lb_logits.shape[0], HG_H, HG_DK).transpose(1, 0, 2)
    st_spec = pl.BlockSpec((bsz, 1, HG_DK, HG_DV), lambda h: (0, h, 0, 0))
    return pl.pallas_call(
        _hgrn_decode_kernel,
        grid=(HG_H,),
        in_specs=[col(EV_Q // HG_DK), col(EV_F // HG_DK), col(EV_I // HG_DK), col(EV_OG // HG_DK),
                  pl.BlockSpec((1,) + lbl3.shape[1:], lambda h: (h, 0, 0)),
                  pl.BlockSpec((1, HG_DV), lambda h: (0, 0)),
                  st_spec],
        out_specs=[pl.BlockSpec((bsz, HG_DV), lambda h: (0, h)), st_spec],
        out_shape=[jax.ShapeDtypeStruct((bsz, HG_H * HG_DV), F32),
                   jax.ShapeDtypeStruct(state.shape, F32)],
        scratch_shapes=[pltpu.VMEM((bsz, HG_DV), F32)],
        compiler_params=_cparams(("arbitrary",)),
        name="hgrn_decode",
    )(p2d, p2d, p2d, p2d, lbl3, gain.reshape(1, HG_DV), state)


def _gla_decode_kernel(q_ref, k_ref, v_ref, r_ref, a_ref, wa_ref, ba_ref, gain_ref, s_ref, o_ref, so_ref, o_scr):
    gate = jnp.dot(a_ref[...].astype(BF16), wa_ref[...], preferred_element_type=F32) + ba_ref[...]
    logf = _log_sigmoid(gate) * (1.0 / GLA_TAU)
    _decode_update(q_ref[...] * (GLA_DK ** -0.5), k_ref[...], v_ref[...], logf, s_ref, so_ref, o_scr)
    o_ref[...] = _rms(o_scr[...], gain_ref[...]) * _silu(r_ref[...])


def _gla_decode(p2d, wa2p, ba, gain, state):
    bsz = p2d.shape[0]
    hk = GLA_H * GLA_DK
    st_spec = pl.BlockSpec((bsz, 1, GLA_DK, GLA_DV), lambda h: (0, h, 0, 0))
    return pl.pallas_call(
        _gla_decode_kernel,
        grid=(GLA_H,),
        in_specs=[pl.BlockSpec((bsz, GLA_DK), lambda h: (0, OD_Q // GLA_DK + h)),
                  pl.BlockSpec((bsz, GLA_DK), lambda h: (0, OD_K // GLA_DK + h)),
                  pl.BlockSpec((bsz, GLA_DV), lambda h: (0, OD_V // GLA_DV + h)),
                  pl.BlockSpec((bsz, GLA_DV), lambda h: (0, OD_R // GLA_DV + h)),
                  pl.BlockSpec((bsz, LANES), lambda h: (0, OD_A // LANES)),
                  pl.BlockSpec((LANES, GLA_DK), lambda h: (0, h)),
                  pl.BlockSpec((1, GLA_DK), lambda h: (0, h)),
                  pl.BlockSpec((1, GLA_DV), lambda h: (0, 0)),
                  st_spec],
        out_specs=[pl.BlockSpec((bsz, GLA_DV), lambda h: (0, h)), st_spec],
        out_shape=[jax.ShapeDtypeStruct((bsz, GLA_H * GLA_DV), F32),
                   jax.ShapeDtypeStruct(state.shape, F32)],
        scratch_shapes=[pltpu.VMEM((bsz, GLA_DV), F32)],
        compiler_params=_cparams(("arbitrary",)),
        name="gla_decode",
    )(p2d, p2d, p2d, p2d, p2d, wa2p, ba.reshape(1, hk), gain.reshape(1, GLA_DV), state)


def _compress_kernel(xa_ref, xb_ref, pe_ref, w1_ref, w2_ref, o_ref, acc_ref):
    nb = o_ref.shape[0]
    acc_ref[...] = jnp.zeros_like(acc_ref)
    for l in range(NSA_BLK):
        rows = pl.ds(l, nb, stride=NSA_BLK)
        x = jnp.concatenate([xa_ref[rows, :], xb_ref[rows, :]], axis=1) + pe_ref[l:l + 1, :]
        acc_ref[...] += jnp.dot(x.astype(BF16), w1_ref[l], preferred_element_type=F32)
    o_ref[...] = jnp.dot(_silu(acc_ref[...]).astype(BF16), w2_ref[...], preferred_element_type=F32)


def _compress_prompt(p2d, pe4, w1bd, w2bd):
    n = p2d.shape[0]
    rc = min(8192, n)
    cw = 4 * NSA_DH
    c0 = PM_CMP // LANES
    return pl.pallas_call(
        _compress_kernel,
        grid=(n // rc,),
        in_specs=[pl.BlockSpec((rc, LANES), lambda i: (i, c0)),
                  pl.BlockSpec((rc, LANES), lambda i: (i, c0 + 1)),
                  pl.BlockSpec(pe4.shape, lambda i: (0, 0)),
                  pl.BlockSpec(w1bd.shape, lambda i: (0, 0, 0)),
                  pl.BlockSpec((cw, cw), lambda i: (0, 0))],
        out_specs=pl.BlockSpec((rc // NSA_BLK, cw), lambda i: (i, 0)),
        out_shape=jax.ShapeDtypeStruct((n // NSA_BLK, cw), F32),
        scratch_shapes=[pltpu.VMEM((rc // NSA_BLK, cw), F32)],
        compiler_params=_cparams(("arbitrary",)),
        name="nsa_compress",
    )(p2d, p2d, pe4, w1bd, w2bd)


AQB = 256
AQB_SHIFT = 8
AUG = 128
ONES_ROWS = 16


def _slopes_lane(g, lanes):
    r = _iota((1, lanes), 1) >> AQB_SHIFT
    out = jnp.zeros((1, lanes), F32)
    for rr in range(NSA_R):
        out = jnp.where(r == rr, 2.0 ** (-(g * NSA_R + rr + 1)), out)
    return out


def _build_key_features(src_ref, row0, dst_ref, g, seq, with_onehot):
    r = _iota((NSA_DH, LANES), 0)
    for cb in range(seq // LANES):
        kpos = cb * LANES + _iota((NSA_DH, LANES), 1)
        feat = jnp.where(r == 32, (kpos >> 6).astype(F32),
                         jnp.where(r == 33, (kpos & 63).astype(F32),
                                   jnp.where((r == 34) | (r == 35), 1.0, 0.0)))
        if with_onehot:
            feat = jnp.where((kpos >> 6) == r, 1.0, feat)
        kt = src_ref[0, row0:row0 + NSA_DH, cb * LANES:(cb + 1) * LANES]
        dst_ref[g, cb * LANES:(cb + 1) * LANES, :] = jnp.concatenate([kt, feat], axis=0).T.astype(BF16)


def _attn_tile(k_ref, g, k0, v_ref, v_row0, qa_ref, mask, m_ref, acc_ref, stream, nkeys=AQB):
    s = jnp.dot(k_ref[g, pl.ds(k0, nkeys), :], qa_ref[g], preferred_element_type=F32)
    if mask is not None:
        s = jnp.where(mask, s, NEG)
    m_old = m_ref[stream]
    m_new = jnp.maximum(m_old, jnp.max(s, axis=0, keepdims=True))
    p = jnp.exp(s - m_new).astype(BF16)
    vt = v_ref[0, v_row0:v_row0 + NSA_DH, pl.ds(k0, nkeys)].astype(BF16)
    vt = jnp.concatenate([vt, jnp.ones((ONES_ROWS, nkeys), BF16)], axis=0)
    acc_ref[stream] = jnp.exp(m_old - m_new) * acc_ref[stream] + jnp.dot(vt, p, preferred_element_type=F32)
    m_ref[stream] = m_new


def _nsa_attn_kernel(qt_ref, rows_ref, win_ref, gt_ref, kc_ref, vct_ref, o_ref, ks_s, kw_s, qa_s, m_s, acc_s,
                     *, nblk):
    qb = pl.program_id(1)
    seq = rows_ref.shape[2]
    q0 = qb * AQB
    nq = NSA_R * AQB
    half = NSA_G * NSA_DH

    @pl.when(qb == 0)
    def _():
        for g in range(NSA_G):
            _build_key_features(rows_ref, 2 * half + g * NSA_DH, ks_s, g, seq, True)
            _build_key_features(win_ref, g * NSA_DH, kw_s, g, seq, False)

    pq = q0 + (_iota((1, nq), 1) & (AQB - 1))
    gates = _sigmoid(gt_ref[0])
    ksel = min(NSA_TOPK, nblk)
    scale = NSA_DH ** -0.5
    key_i = _iota((AQB, nq), 0)
    qry_t = _iota((AQB, nq), 1) & (AQB - 1)
    causal = key_i <= qry_t
    beyond = key_i > qry_t
    pq_f = q0 + _iota((32, AQB), 1)
    feat_row = _iota((32, AQB), 0)

    m_s[...] = jnp.full(m_s.shape, NEG, F32)
    acc_s[...] = jnp.zeros_like(acc_s)

    def result(i):
        return acc_s[i, 0:NSA_DH, :] / jnp.maximum(acc_s[i, NSA_DH:NSA_DH + 1, :], TINY)

    o_cmp = []
    for g in range(NSA_G):
        slope = _slopes_lane(g, nq)

        qt_g = [qt_ref[0, (g * NSA_R + r) * NSA_DH:(g * NSA_R + r + 1) * NSA_DH, :] * scale for r in range(NSA_R)]
        qs_t = jnp.concatenate(qt_g, axis=1).astype(BF16)
        sc = jnp.dot(kc_ref[0, g], qs_t, preferred_element_type=F32)
        dist_c = pq - (_iota((nblk, nq), 0) * NSA_BLK + NSA_BLK - 1)
        mask_c = dist_c >= 0
        sc = jnp.where(mask_c, sc - slope * dist_c.astype(F32), NEG)
        pc = jnp.where(mask_c, jnp.exp(sc - jnp.max(sc, axis=0, keepdims=True)), 0.0)
        pc = pc / jnp.maximum(jnp.sum(pc, axis=0, keepdims=True), TINY)
        o_cmp.append(jnp.dot(vct_ref[0, g], pc.astype(BF16), preferred_element_type=F32))
        imp = pc[:, 0:AQB]
        for r in range(1, NSA_R):
            imp = imp + pc[:, r * AQB:(r + 1) * AQB]

        n_io = _iota((nblk, AQB), 0)
        tpos = q0 + _iota((nblk, AQB), 1)
        forced = (n_io == (tpos >> 6)) | (n_io == 0)
        started = n_io * NSA_BLK <= tpos
        score = jnp.where(forced, jnp.inf, jnp.where(started, imp, -jnp.inf))
        rank = jnp.zeros((nblk, AQB), I32)
        for i in range(nblk):
            row = score[i:i + 1, :]
            rank = rank + ((row > score) | ((row == score) & (i < n_io))).astype(I32)
        sel_bias = jnp.where((rank < ksel) & started, 0.0, NEG)
        if nblk < 32:
            sel_bias = jnp.concatenate([sel_bias, jnp.zeros((32 - nblk, AQB), F32)], axis=0)

        cols = []
        for r in range(NSA_R):
            sl = 2.0 ** (-(g * NSA_R + r + 1))
            pos_feat = jnp.where(feat_row == 0, 64.0 * sl,
                                 jnp.where(feat_row == 1, sl,
                                           jnp.where(feat_row == 2, (-64.0 * sl) * (pq_f >> 6).astype(F32),
                                                     jnp.where(feat_row == 3, (-sl) * (pq_f & 63).astype(F32), 0.0))))
            cols.append(jnp.concatenate([qt_g[r], sel_bias, pos_feat], axis=0))
        qa_s[g] = jnp.concatenate(cols, axis=1).astype(BF16)

    def slc_tile(k0, mask, nkeys=AQB):
        for g in range(NSA_G):
            _attn_tile(ks_s, g, k0, rows_ref, 3 * half + g * NSA_DH, qa_s, mask, m_s, acc_s, g, nkeys)

    def win_tile(k0, mask):
        for g in range(NSA_G):
            _attn_tile(kw_s, g, k0, win_ref, half + g * NSA_DH, qa_s, mask, m_s, acc_s, NSA_G + g)

    def slc_body(kt, c):
        slc_tile(pl.multiple_of(kt * 2 * AQB, 2 * AQB), None, 2 * AQB)
        return c

    lax.fori_loop(0, qb >> 1, slc_body, 0)

    @pl.when((qb & 1) == 1)
    def _():
        slc_tile(pl.multiple_of(q0 - AQB, AQB), None)

    @pl.when(qb >= 2)
    def _():
        win_tile(pl.multiple_of(q0 - 2 * AQB, AQB), beyond)

    @pl.when(qb >= 1)
    def _():
        win_tile(pl.multiple_of(q0 - AQB, AQB), None)

    slc_tile(pl.multiple_of(q0, AQB), causal)
    win_tile(pl.multiple_of(q0, AQB), causal)

    for g in range(NSA_G):
        def gate_row(j, g=g):
            return jnp.concatenate([gates[g * 12 + r * 3 + j:g * 12 + r * 3 + j + 1, :] for r in range(NSA_R)],
                                   axis=1)

        o_t = gate_row(0) * o_cmp[g] + gate_row(1) * result(g) + gate_row(2) * result(NSA_G + g)
        o_st = jnp.concatenate([o_t[:, r * AQB:(r + 1) * AQB] for r in range(NSA_R)], axis=0)
        o_ref[:, g * NSA_R * NSA_DH:(g + 1) * NSA_R * NSA_DH] = o_st.T.astype(o_ref.dtype)


def _nsa_attn_prompt(qt, rows_t, win_t, gate_t, kc, vct, bsz, seq):
    assert NSA_WINDOW == 2 * AQB and seq % AQB == 0 and seq // NSA_BLK <= 32
    nqb = seq // AQB
    nblk = seq // NSA_BLK
    qw = NSA_H * NSA_DH
    nq = NSA_R * AQB
    per_b = lambda a: pl.BlockSpec((1,) + a.shape[1:], lambda b, i: (b,) + (0,) * (a.ndim - 1))
    per_q = lambda a: pl.BlockSpec((1, a.shape[1], AQB), lambda b, i: (b, 0, i))
    return pl.pallas_call(
        functools.partial(_nsa_attn_kernel, nblk=nblk),
        grid=(bsz, nqb),
        in_specs=[per_q(qt), per_b(rows_t), per_b(win_t), per_q(gate_t), per_b(kc), per_b(vct)],
        out_specs=pl.BlockSpec((AQB, qw), lambda b, i: (b * nqb + i, 0)),
        out_shape=jax.ShapeDtypeStruct((bsz * seq, qw), BF16),
        scratch_shapes=[pltpu.VMEM((NSA_G, seq, AUG), BF16), pltpu.VMEM((NSA_G, seq, AUG), BF16),
                        pltpu.VMEM((NSA_G, AUG, nq), BF16),
                        pltpu.VMEM((2 * NSA_G, 1, nq), F32), pltpu.VMEM((2 * NSA_G, NSA_DH + ONES_ROWS, nq), F32)],
        compiler_params=_cparams(("parallel", "arbitrary")),
        name="nsa_attn",
    )(qt, rows_t, win_t, gate_t, kc, vct)


QROWS = 16


def _head_slopes(rows):
    r = _iota((rows, 1), 0)
    out = jnp.zeros((rows, 1), F32)
    for h in range(NSA_H):
        out = jnp.where(r == h, 2.0 ** (-(h + 1)), out)
    return out


def _softmax_rows(s, valid):
    s = jnp.where(valid, s, NEG)
    p = jnp.where(valid, jnp.exp(s - jnp.max(s, axis=1, keepdims=True)), 0.0)
    return p / jnp.maximum(jnp.sum(p, axis=1, keepdims=True), TINY)


def _padded_queries(q_ref):
    q = q_ref[0] * (NSA_DH ** -0.5)
    return jnp.concatenate([q, jnp.zeros((QROWS - NSA_H, NSA_DH), F32)], axis=0).astype(BF16)


PLANES = 4 * NSA_G
PAGE_ROWS = PLANES * NSA_DH
CMP_PLANES = 2 * NSA_G


def _dec_cmp_kernel(pt_ref, cache_ref, pe_ref, w1_ref, w2_ref, q_ref, o_ref, sel_ref, buf0, buf1, acc_ref, cmp_ref, sem,
                    *, npages, nsel):
    b = pl.program_id(0)
    nb = npages * (PAGE // NSA_BLK)
    nrow = npages * CMP_PLANES
    crows = CMP_PLANES * NSA_DH
    bufs = (buf0, buf1)
    nseq = 2 * pl.num_programs(0)

    def page_copy(seq, pg, slot):
        src = pl.multiple_of(pt_ref[seq * npages + pg] * PAGE_ROWS, PAGE_ROWS)
        return pltpu.make_async_copy(cache_ref.at[pl.ds(src, crows), :],
                                     bufs[slot].at[pl.ds(pl.multiple_of(pg * crows, crows), crows), :],
                                     sem.at[slot])

    def start_all(seq, slot):
        def start(pg, c):
            page_copy(seq, pg, slot).start()
            return c
        lax.fori_loop(0, npages, start, 0)

    def wait_all(seq, slot):
        def wait(pg, c):
            page_copy(seq, pg, slot).wait()
            return c
        lax.fori_loop(0, npages, wait, 0)

    is_k = ((_iota((nrow, 1), 0) >> 1) & 1) == 0
    pos = nb * NSA_BLK
    slope = _head_slopes(QROWS)
    row_grp = _iota((QROWS, 1), 0) >> 2
    lane = _iota((1, nb), 1)
    blk_of = jnp.where(lane < npages, 2 * lane, 2 * (lane - npages) + 1)
    dist = (pos - (blk_of * NSA_BLK + NSA_BLK - 1)).astype(F32)
    ri, ci = _iota((nb, nb), 0), _iota((nb, nb), 1)
    blk_r = jnp.where(ri < npages, 2 * ri, 2 * (ri - npages) + 1)
    blk_c = jnp.where(ci < npages, 2 * ci, 2 * (ci - npages) + 1)

    for slot in range(2):
        seq = 2 * b + slot
        if slot == 0:
            @pl.when(b == 0)
            def _():
                start_all(0, 0)

        wait_all(seq, slot)
        acc_ref[...] = jnp.zeros_like(acc_ref)
        for d in range(NSA_DH):
            x = bufs[slot][pl.ds(d, nrow, stride=NSA_DH), :]
            x = x + jnp.where(is_k, pe_ref[d, 0:1, :], pe_ref[d, 1:2, :])
            acc_ref[...] += jnp.dot(x.astype(BF16), w1_ref[d], preferred_element_type=F32)

        if slot == 0:
            start_all(seq + 1, 1)
        else:
            @pl.when(seq + 1 < nseq)
            def _(seq=seq):
                start_all(seq + 1, 0)

        acc = acc_ref[...]
        h = _silu(jnp.where(is_k, acc[:, :LANES], acc[:, LANES:]))
        c2 = jnp.dot(h.astype(BF16), w2_ref[...], preferred_element_type=F32)
        cmp_ref[...] = jnp.where(is_k, c2[:, :LANES], c2[:, LANES:])

        q = q_ref[slot] * (NSA_DH ** -0.5)
        q16 = jnp.concatenate([q, jnp.zeros((QROWS - NSA_H, NSA_DH), F32)], axis=0).astype(BF16)
        o_all = jnp.zeros((QROWS, NSA_DH), F32)
        for g in range(NSA_G):
            kc = cmp_ref[pl.ds(g, npages, stride=CMP_PLANES), :].astype(BF16)
            vc = cmp_ref[pl.ds(NSA_G + g, npages, stride=CMP_PLANES), :].astype(BF16)
            s = jnp.concatenate([lax.dot_general(q16, kc[:, j * NSA_DH:(j + 1) * NSA_DH], NT_DIMS,
                                                 preferred_element_type=F32) for j in range(2)], axis=1)
            p = _softmax_rows(s - slope * dist, dist >= 0)
            pb = p.astype(BF16)
            o_g = sum(jnp.dot(pb[:, j * npages:(j + 1) * npages], vc[:, j * NSA_DH:(j + 1) * NSA_DH],
                              preferred_element_type=F32) for j in range(2))
            o_all = jnp.where(row_grp == g, o_g, o_all)
            imp = jnp.sum(jnp.where(row_grp == g, p, 0.0), axis=0, keepdims=True)
            score_row = jnp.where(blk_of == 0, jnp.inf, imp)
            score_col = jnp.sum(jnp.where(ri == ci, jnp.broadcast_to(score_row, (nb, nb)), 0.0),
                                axis=1, keepdims=True)
            beats = (score_col > score_row) | ((score_col == score_row) & (blk_r < blk_c))
            rank = jnp.sum(beats.astype(I32), axis=0, keepdims=True)
            sel_ref[slot, g:g + 1, :] = (rank < nsel).astype(F32)
        o_ref[slot] = o_all[:NSA_H]


def _dec_cmp(page_table, cache2, pe_dec, w1dec, w2dec, q3, nsel):
    bsz, npages = page_table.shape
    nb = npages * (PAGE // NSA_BLK)
    nrow = npages * CMP_PLANES
    assert bsz % 2 == 0
    grid_spec = pltpu.PrefetchScalarGridSpec(
        num_scalar_prefetch=1,
        grid=(bsz // 2,),
        in_specs=[pl.BlockSpec(memory_space=pl.ANY),
                  pl.BlockSpec(pe_dec.shape, lambda b, pt: (0, 0, 0)),
                  pl.BlockSpec(w1dec.shape, lambda b, pt: (0, 0, 0)),
                  pl.BlockSpec(w2dec.shape, lambda b, pt: (0, 0)),
                  pl.BlockSpec((2, NSA_H, NSA_DH), lambda b, pt: (b, 0, 0))],
        out_specs=[pl.BlockSpec((2, NSA_H, NSA_DH), lambda b, pt: (b, 0, 0)),
                   pl.BlockSpec((2, NSA_G, nb), lambda b, pt: (b, 0, 0))],
        scratch_shapes=[pltpu.VMEM((nrow * NSA_DH, LANES), F32), pltpu.VMEM((nrow * NSA_DH, LANES), F32),
                        pltpu.VMEM((nrow, 2 * LANES), F32), pltpu.VMEM((nrow, LANES), F32),
                        pltpu.SemaphoreType.DMA((2,))])
    return pl.pallas_call(
        functools.partial(_dec_cmp_kernel, npages=npages, nsel=nsel),
        grid_spec=grid_spec,
        out_shape=[jax.ShapeDtypeStruct((bsz, NSA_H, NSA_DH), F32),
                   jax.ShapeDtypeStruct((bsz, NSA_G, nb), F32)],
        compiler_params=_cparams(("arbitrary",)),
        name="nsa_dec_cmp",
    )(page_table.reshape(-1), cache2, pe_dec, w1dec, w2dec, q3)


def _attend_with_self(q16, slope, kt, vt, dist, valid, k_self, v_self):
    s = jnp.dot(q16, kt.astype(BF16), preferred_element_type=F32) - slope * dist
    s_self = jnp.sum(q16.astype(F32) * k_self.astype(BF16).astype(F32), axis=1, keepdims=True)
    m = jnp.maximum(jnp.max(jnp.where(valid, s, NEG), axis=1, keepdims=True), s_self)
    p = jnp.where(valid, jnp.exp(s - m), 0.0)
    p_self = jnp.exp(s_self - m)
    num = lax.dot_general(p.astype(BF16), vt.astype(BF16), NT_DIMS, preferred_element_type=F32) + p_self * v_self
    return num / jnp.maximum(jnp.sum(p, axis=1, keepdims=True) + p_self, TINY)


def _dec_attn_kernel(pt_ref, idx_ref, cache_ref, q_ref, oc_ref, kn_ref, win_ref, gt_ref, o_ref, kbuf, vbuf, sem,
                     *, npages, nsel, wlen):
    b = pl.program_id(0)
    pos = npages * PAGE

    copies = []
    for g in range(NSA_G):
        for j in range(nsel):
            blk = idx_ref[(b * NSA_G + g) * NSA_TOPK + j]
            base = pt_ref[b * npages + (blk >> 1)] * PAGE_ROWS
            dst = pl.ds(j * PAGE, PAGE)
            k_rows = pl.ds(pl.multiple_of(base + (2 * NSA_G + g) * NSA_DH, NSA_DH), NSA_DH)
            v_rows = pl.ds(pl.multiple_of(base + (3 * NSA_G + g) * NSA_DH, NSA_DH), NSA_DH)
            copies.append(pltpu.make_async_copy(cache_ref.at[k_rows, :], kbuf.at[g, :, dst], sem.at[0]))
            copies.append(pltpu.make_async_copy(cache_ref.at[v_rows, :], vbuf.at[g, :, dst], sem.at[0]))
    for cp in copies:
        cp.start()
    for cp in copies:
        cp.wait()

    q16 = _padded_queries(q_ref)
    slope = _head_slopes(QROWS)
    row_grp = _iota((QROWS, 1), 0) >> 2
    kn = kn_ref[0]
    lane = _iota((1, nsel * PAGE), 1)
    within = lane & (PAGE - 1)
    wdist = wlen - _iota((1, wlen), 1)
    o_slc = jnp.zeros((QROWS, NSA_DH), F32)
    o_win = jnp.zeros((QROWS, NSA_DH), F32)
    for g in range(NSA_G):
        gs = slice(g * NSA_DH, (g + 1) * NSA_DH)
        blk_lane = jnp.zeros((1, nsel * PAGE), I32)
        for j in range(nsel):
            blk_lane = jnp.where((lane >> 7) == j, idx_ref[(b * NSA_G + g) * NSA_TOPK + j], blk_lane)
        valid = (within >> 6) == (blk_lane & 1)
        kpos = (blk_lane >> 1) * PAGE + within
        o_g = _attend_with_self(q16, slope, kbuf[g], vbuf[g], (pos - kpos).astype(F32), valid,
                                kn[0:1, gs], kn[1:2, gs])
        o_slc = jnp.where(row_grp == g, o_g, o_slc)
        o_g = _attend_with_self(q16, slope, win_ref[0, g * NSA_DH:(g + 1) * NSA_DH, :],
                                win_ref[0, (NSA_G + g) * NSA_DH:(NSA_G + g + 1) * NSA_DH, :],
                                wdist.astype(F32), wdist < NSA_WINDOW, kn[2:3, gs], kn[3:4, gs])
        o_win = jnp.where(row_grp == g, o_g, o_win)

    gates = _sigmoid(gt_ref[pl.ds(b, 1), :])
    r_io = _iota((QROWS, LANES), 0)
    l_io = _iota((QROWS, LANES), 1)

    def gate_col(j):
        return jnp.sum(jnp.where(l_io == 3 * r_io + j, gates, 0.0), axis=1, keepdims=True)

    o_cmp = jnp.concatenate([oc_ref[0], jnp.zeros((QROWS - NSA_H, NSA_DH), F32)], axis=0)
    o = gate_col(0) * o_cmp + gate_col(1) * o_slc + gate_col(2) * o_win
    o_ref[0] = o[:NSA_H]


def _dec_attn(page_table, idx, cache2, q3, ocmp, knew, win_t, gates, nsel):
    bsz, npages = page_table.shape
    wlen = win_t.shape[2]
    hd = pl.BlockSpec((1, NSA_H, NSA_DH), lambda b, pt, ix: (b, 0, 0))
    grid_spec = pltpu.PrefetchScalarGridSpec(
        num_scalar_prefetch=2,
        grid=(bsz,),
        in_specs=[pl.BlockSpec(memory_space=pl.ANY), hd, hd,
                  pl.BlockSpec((1,) + knew.shape[1:], lambda b, pt, ix: (b, 0, 0)),
                  pl.BlockSpec((1,) + win_t.shape[1:], lambda b, pt, ix: (b, 0, 0)),
                  pl.BlockSpec(gates.shape, lambda b, pt, ix: (0, 0))],
        out_specs=hd,
        scratch_shapes=[pltpu.VMEM((NSA_G, NSA_DH, nsel * PAGE), F32), pltpu.VMEM((NSA_G, NSA_DH, nsel * PAGE), F32),
                        pltpu.SemaphoreType.DMA((1,))])
    return pl.pallas_call(
        functools.partial(_dec_attn_kernel, npages=npages, nsel=nsel, wlen=wlen),
        grid_spec=grid_spec,
        out_shape=jax.ShapeDtypeStruct((bsz, NSA_H, NSA_DH), F32),
        compiler_params=_cparams(("arbitrary",)),
        name="nsa_dec_attn",
    )(page_table.reshape(-1), idx.reshape(-1), cache2, q3, ocmp, knew, win_t, gates)


def _prep_weights(w_in_even, nsa_cmp_pos, nsa_cmp_w1, nsa_cmp_w2, w_out_even, w_in_odd, gla_wa2, w_out_odd,
                  ffn_w13, ffn_w2):
    d = D_MODEL
    wie = jnp.pad(w_in_even[0], ((0, 0), (0, PROJ_W - w_in_even.shape[2]))).astype(BF16)
    a0 = 2 * GLA_H * GLA_DK + GLA_H * GLA_DV
    wo = w_in_odd[0]
    wio = jnp.concatenate([wo[:, :a0], wo[:, a0 + GLA_RANK:], wo[:, a0:a0 + GLA_RANK]], axis=1)
    wio = jnp.pad(wio, ((0, 0), (0, ODD_W - wio.shape[1]))).astype(BF16)
    wa2p = jnp.pad(gla_wa2[0], ((0, LANES - GLA_RANK), (0, 0))).astype(BF16)
    hv = HG_H * HG_DV
    woe_h, woe_n = w_out_even[0, :hv].astype(BF16), w_out_even[0, hv:].astype(BF16)
    woo = w_out_odd[0].astype(BF16)
    nj = D_FF // FF_TILE
    w13t, w2b = [], []
    for l in range(ffn_w13.shape[0]):
        gate = ffn_w13[l, :, :D_FF].reshape(d, nj, FF_TILE)
        up = ffn_w13[l, :, D_FF:].reshape(d, nj, FF_TILE)
        w13t.append(jnp.concatenate([gate, up], axis=2).reshape(d, 2 * D_FF).astype(BF16))
        w2b.append(ffn_w2[l].astype(BF16))
    w1 = nsa_cmp_w1[0].reshape(2, NSA_BLK, NSA_DH, NSA_DH)
    w2 = nsa_cmp_w2[0]
    cw = 4 * NSA_DH
    eye = jnp.eye(2 * NSA_G, dtype=F32).reshape(2, NSA_G, 2, NSA_G)
    w1bd = jnp.einsum("cldj,cgCG->lcgdCGj", w1, eye).reshape(NSA_BLK, cw, cw)
    w2bd = jnp.einsum("cjd,cgCG->cgjCGd", w2, eye).reshape(cw, cw)
    pe = nsa_cmp_pos[0]
    pe4 = jnp.concatenate([pe[0], pe[0], pe[1], pe[1]], axis=1)
    we = w_in_even[0]
    hgw = 2 * HG_H * HG_DK + 2 * HG_H * HG_DV
    nq0 = hgw + NSA_H * NSA_DH
    w_main = jnp.concatenate([we[:, :hgw], we[:, nq0:nq0 + cw]], axis=1).astype(BF16)
    w_tr = jnp.pad(we[:, hgw:].T, ((0, T_END - (we.shape[1] - hgw)), (0, 0))).astype(BF16)
    eye2 = jnp.eye(2, dtype=F32)
    w1dec = jnp.einsum("cldo,jJ->djlcJo", w1, eye2).reshape(NSA_DH, 2 * NSA_BLK, 4 * NSA_DH)
    w2dec = jnp.einsum("cod,jJ->jocJd", w2, eye2).reshape(2 * NSA_DH, 4 * NSA_DH)
    pe_dec = jnp.tile(pe.transpose(2, 0, 1), (1, 1, 2))
    return dict(wie=wie, wio=wio, wa2p=wa2p, woe_h=woe_h, woe_n=woe_n, woo=woo, w13t=w13t, w2b=w2b,
                w1bd=w1bd.astype(BF16), w2bd=w2bd.astype(BF16), pe4=pe4, w_main=w_main, w_tr=w_tr,
                w1dec=w1dec.astype(BF16), w2dec=w2dec.astype(BF16), pe_dec=pe_dec)


def _forward_prompt(x, pw, hg_lb_logits, hg_norm, gla_ba, gla_norm, norm_mix, norm_ffn, norm_final):
    bsz, seq, d = x.shape
    n = bsz * seq
    x2d = x.reshape(n, d)
    tm = 512 if seq % 512 == 0 else 256
    tmf = next(t for t in (512, 256) if n % t == 0)
    tb = tm

    p, qt, rows_t, win_t, gate_t = _norm_proj_t(x2d, norm_mix[0], pw["w_main"], pw["w_tr"], bsz, seq, tm)
    o_h, hg_state = _hgrn_prompt(p, hg_lb_logits, hg_norm[0], bsz, seq, tb)
    cmp = _compress_prompt(p, pw["pe4"], pw["w1bd"], pw["w2bd"])
    nblk = seq // NSA_BLK
    cmp4 = cmp.reshape(bsz, nblk, 2 * NSA_G, NSA_DH).transpose(0, 2, 1, 3)
    kc = cmp4[:, :NSA_G].astype(BF16)
    vct = cmp4[:, NSA_G:].transpose(0, 1, 3, 2).astype(BF16)
    o_n = _nsa_attn_prompt(qt, rows_t, win_t, gate_t, kc, vct, bsz, seq)
    x1 = _out_ffn([o_h, o_n], x2d, [pw["woe_h"], pw["woe_n"]], norm_ffn[0], pw["w13t"][0], pw["w2b"][0], None, tmf)

    rows = rows_t.reshape(bsz, 4, NSA_G, NSA_DH, seq).transpose(0, 4, 1, 2, 3)[None]
    wn = min(NSA_WINDOW, seq)
    win = win_t[:, :, seq - wn:].reshape(bsz, 2, NSA_G, NSA_DH, wn).transpose(0, 4, 1, 2, 3)[None]

    p2 = _norm_proj(x1, norm_mix[1], pw["wio"], tm, ODD_W)
    o_g, gla_state = _gla_prompt(p2, pw["wa2p"], gla_ba[0], gla_norm[0], bsz, seq, tb)
    y = _out_ffn([o_g], x1, [pw["woo"]], norm_ffn[1], pw["w13t"][1], pw["w2b"][1], norm_final, tmf)
    return y.reshape(bsz, seq, d), rows, win, hg_state[None], gla_state[None]


def _forward_sample(x, pw, cache_nsa_kv, cache_win_kv, state_hgrn, state_gla, page_table, hg_lb_logits, hg_norm,
                    gla_ba, gla_norm, norm_mix, norm_ffn, norm_final):
    bsz, seq, d = x.shape
    x2d = x.reshape(bsz, d)
    npages = page_table.shape[1]
    nblk_total = npages * (PAGE // NSA_BLK) + 1
    nsel = min(NSA_TOPK, nblk_total) - 1

    p = _norm_proj(x2d, norm_mix[0], pw["wie"], bsz, PROJ_W // 2)
    o_h, hg_state = _hgrn_decode(p, hg_lb_logits, hg_norm[0], state_hgrn[0])

    cache2 = cache_nsa_kv[0].transpose(0, 2, 3, 4, 1).reshape(cache_nsa_kv.shape[1] * PAGE_ROWS, PAGE)
    q3 = p[:, EV_NQ:EV_CMP].reshape(bsz, NSA_H, NSA_DH)
    ocmp, sel = _dec_cmp(page_table, cache2, pw["pe_dec"], pw["w1dec"], pw["w2dec"], q3, nsel)
    sel = sel.reshape(bsz, NSA_G, 2, npages).transpose(0, 1, 3, 2).reshape(bsz, NSA_G, 2 * npages)
    idx = jnp.argsort(-sel, axis=-1, stable=True)[..., :NSA_TOPK].astype(I32)
    knew = p[:, EV_SLC:EV_GATE].reshape(bsz, 4, NSA_G * NSA_DH)
    wlen = cache_win_kv.shape[2]
    win_t = cache_win_kv[0].transpose(0, 2, 3, 4, 1).reshape(bsz, 2 * NSA_G * NSA_DH, wlen)
    o_n = _dec_attn(page_table, idx, cache2, q3, ocmp, knew, win_t, p[:, EV_GATE:EV_GATE + LANES], nsel)
    o_n = o_n.reshape(bsz, NSA_H * NSA_DH)
    x1 = _out_ffn([o_h, o_n], x2d, [pw["woe_h"], pw["woe_n"]], norm_ffn[0], pw["w13t"][0], pw["w2b"][0], None, bsz)

    rows = p[:, EV_CMP:EV_WIN].reshape(1, bsz, 1, 4, NSA_G, NSA_DH)
    win_new = p[:, EV_WIN:EV_GATE].reshape(1, bsz, 1, 2, NSA_G, NSA_DH)
    wk = jnp.concatenate([cache_win_kv[:1], win_new], axis=2)
    wn = min(NSA_WINDOW, npages * PAGE + 1)
    win = wk[:, :, wk.shape[2] - wn:]

    p2 = _norm_proj(x1, norm_mix[1], pw["wio"], bsz, ODD_W)
    o_g, gla_state = _gla_decode(p2, pw["wa2p"], gla_ba[0], gla_norm[0], state_gla[0])
    y = _out_ffn([o_g], x1, [pw["woo"]], norm_ffn[1], pw["w13t"][1], pw["w2b"][1], norm_final, bsz)
    return y.reshape(bsz, 1, d), rows, win, hg_state[None], gla_state[None]


def kernel(x_prompt, x_sample, cache_nsa_kv, cache_win_kv, state_hgrn, state_gla, page_table, w_in_even,
           hg_lb_logits, hg_norm, nsa_cmp_pos, nsa_cmp_w1, nsa_cmp_w2, w_out_even, w_in_odd, gla_wa2, gla_ba,
           gla_norm, w_out_odd, norm_mix, norm_ffn, norm_final, ffn_w13, ffn_w2):
    pw = _prep_weights(w_in_even, nsa_cmp_pos, nsa_cmp_w1, nsa_cmp_w2, w_out_even, w_in_odd, gla_wa2, w_out_odd,
                       ffn_w13, ffn_w2)
    y_p, kv_p, win_p, hg_p, gla_p = _forward_prompt(x_prompt, pw, hg_lb_logits, hg_norm, gla_ba, gla_norm,
                                                    norm_mix, norm_ffn, norm_final)
    y_s, kv_s, win_s, hg_s, gla_s = _forward_sample(x_sample, pw, cache_nsa_kv, cache_win_kv, state_hgrn,
                                                    state_gla, page_table, hg_lb_logits, hg_norm, gla_ba,
                                                    gla_norm, norm_mix, norm_ffn, norm_final)
    return (y_p, y_s, kv_p, kv_s, win_p, win_s, hg_p, hg_s, gla_p, gla_s)
```

```python
import functools

import jax
import jax.numpy as jnp
from jax import lax
from jax.experimental import pallas as pl
from jax.experimental.pallas import tpu as pltpu

F32 = jnp.float32
BF16 = jnp.bfloat16
I32 = jnp.int32

D_MODEL = 1024
HG_H, HG_DK, HG_DV = 4, 128, 128
NSA_H, NSA_DH, NSA_G, NSA_R = 8, 64, 2, 4
NSA_BLK = 64
NSA_TOPK = 16
NSA_WINDOW = 512
NSA_QB = 128
GLA_H, GLA_DK, GLA_DV = 4, 128, 256
GLA_RANK = 16
GLA_TAU = 16.0
D_FF = 2816
EPS = 1e-6
NEG = -1e30
TINY = 1e-30
PAGE = 128

PROJ_W = 3584
ODD_W = 3200
EV_Q, EV_F, EV_I, EV_OG = 0, 512, 1024, 1536
EV_NQ, EV_CMP, EV_SLC, EV_WIN, EV_GATE = 2048, 2560, 2816, 3072, 3328
OD_Q, OD_K, OD_V, OD_R, OD_A = 0, 512, 1024, 2048, 3072

LANES = 128
V7X_VMEM_LIMIT = 56 * 1024 * 1024
FF_TILE = 1408
CHUNK = 128
SUB = 16

NT_DIMS = (((1,), (1,)), ((), ()))
TN_DIMS = (((0,), (0,)), ((), ()))


def _cparams(sem):
    return pltpu.CompilerParams(dimension_semantics=sem, vmem_limit_bytes=V7X_VMEM_LIMIT)


def _rms(x, g):
    return x * lax.rsqrt(jnp.mean(x * x, axis=-1, keepdims=True) + EPS) * g


def _sigmoid(x):
    return 0.5 * jnp.tanh(0.5 * x) + 0.5


def _silu(x):
    return x * _sigmoid(x)


def _iota(shape, dim):
    return lax.broadcasted_iota(I32, shape, dim)


def _norm_proj_kernel(x_ref, g_ref, w_ref, o_ref, xn_ref):
    @pl.when(pl.program_id(1) == 0)
    def _():
        xn_ref[...] = _rms(x_ref[...], g_ref[...]).astype(BF16)

    o_ref[...] = jnp.dot(xn_ref[...], w_ref[...], preferred_element_type=F32)


def _norm_proj(x2d, gain, w, tm, tn):
    m, k = x2d.shape
    n = w.shape[1]
    return pl.pallas_call(
        _norm_proj_kernel,
        grid=(m // tm, n // tn),
        in_specs=[pl.BlockSpec((tm, k), lambda i, j: (i, 0)),
                  pl.BlockSpec((1, k), lambda i, j: (0, 0)),
                  pl.BlockSpec((k, tn), lambda i, j: (0, j))],
        out_specs=pl.BlockSpec((tm, tn), lambda i, j: (i, j)),
        out_shape=jax.ShapeDtypeStruct((m, n), F32),
        scratch_shapes=[pltpu.VMEM((tm, k), BF16)],
        compiler_params=_cparams(("parallel", "arbitrary")),
        name="norm_proj",
    )(x2d, gain.reshape(1, k), w)


PM_W = 2304
PM_CMP = 2048
T_Q, T_ROWS, T_WIN, T_GATE, T_END = 0, 512, 1024, 1280, 1312


def _norm_proj_t_kernel(x_ref, g_ref, w_ref, wt_ref, o_ref, q_ref, rows_ref, win_ref, gate_ref):
    xn = _rms(x_ref[...], g_ref[...]).astype(BF16)
    o_ref[...] = jnp.dot(xn, w_ref[...], preferred_element_type=F32)
    t = lax.dot_general(wt_ref[...], xn, NT_DIMS, preferred_element_type=F32)
    q_ref[0] = t[T_Q:T_ROWS]
    rows_ref[0] = t[T_ROWS:T_WIN]
    win_ref[0] = t[T_WIN:T_GATE]
    gate_ref[0] = t[T_GATE:T_END]


def _norm_proj_t(x2d, gain, w, wt, bsz, seq, tm):
    k = x2d.shape[1]
    nt = seq // tm
    tr = lambda rows: pl.BlockSpec((1, rows, tm), lambda b, t: (b, 0, t))
    sizes = (T_ROWS - T_Q, T_WIN - T_ROWS, T_GATE - T_WIN, T_END - T_GATE)
    return pl.pallas_call(
        _norm_proj_t_kernel,
        grid=(bsz, nt),
        in_specs=[pl.BlockSpec((tm, k), lambda b, t: (b * nt + t, 0)),
                  pl.BlockSpec((1, k), lambda b, t: (0, 0)),
                  pl.BlockSpec(w.shape, lambda b, t: (0, 0)),
                  pl.BlockSpec(wt.shape, lambda b, t: (0, 0))],
        out_specs=[pl.BlockSpec((tm, PM_W), lambda b, t: (b * nt + t, 0))] + [tr(r) for r in sizes],
        out_shape=[jax.ShapeDtypeStruct((bsz * seq, PM_W), F32)]
        + [jax.ShapeDtypeStruct((bsz, r, seq), F32) for r in sizes],
        compiler_params=_cparams(("parallel", "arbitrary")),
        name="norm_proj_t",
    )(x2d, gain.reshape(1, k), w, wt)


def _out_ffn_kernel(*refs, n_mix, final_norm):
    mix_refs = refs[:n_mix]
    res_ref = refs[n_mix]
    wo_refs = refs[n_mix + 1:2 * n_mix + 1]
    g_ref, w13_ref, w2_ref = refs[2 * n_mix + 1:2 * n_mix + 4]
    pos = 2 * n_mix + 4
    gf_ref = refs[pos] if final_norm else None
    pos += 1 if final_norm else 0
    o_ref, x1_ref, h_ref, acc_ref = refs[pos:pos + 4]
    j = pl.program_id(1)

    @pl.when(j == 0)
    def _():
        x1 = res_ref[...]
        for m_ref, w_ref in zip(mix_refs, wo_refs):
            x1 = x1 + jnp.dot(m_ref[...].astype(BF16), w_ref[...], preferred_element_type=F32)
        x1_ref[...] = x1
        h_ref[...] = _rms(x1, g_ref[...]).astype(BF16)
        acc_ref[...] = jnp.zeros_like(acc_ref)

    gu = jnp.dot(h_ref[...], w13_ref[...], preferred_element_type=F32)
    act = _silu(gu[:, :FF_TILE]) * gu[:, FF_TILE:]
    acc_ref[...] += jnp.dot(act.astype(BF16), w2_ref[...], preferred_element_type=F32)

    @pl.when(j == pl.num_programs(1) - 1)
    def _():
        y = x1_ref[...] + acc_ref[...]
        if final_norm:
            y = _rms(y, gf_ref[...])
        o_ref[...] = y


def _out_ffn(mixes, res, wos, g_ffn, w13t, w2, g_final, tm):
    m, d = res.shape
    n_mix = len(mixes)
    nj = D_FF // FF_TILE
    final_norm = g_final is not None
    in_specs = [pl.BlockSpec((tm, mx.shape[1]), lambda i, j: (i, 0)) for mx in mixes]
    in_specs.append(pl.BlockSpec((tm, d), lambda i, j: (i, 0)))
    in_specs += [pl.BlockSpec(w.shape, lambda i, j: (0, 0)) for w in wos]
    in_specs += [pl.BlockSpec((1, d), lambda i, j: (0, 0)),
                 pl.BlockSpec((d, 2 * FF_TILE), lambda i, j: (0, j)),
                 pl.BlockSpec((FF_TILE, d), lambda i, j: (j, 0))]
    args = list(mixes) + [res] + list(wos) + [g_ffn.reshape(1, d), w13t, w2]
    if final_norm:
        in_specs.append(pl.BlockSpec((1, d), lambda i, j: (0, 0)))
        args.append(g_final.reshape(1, d))
    return pl.pallas_call(
        functools.partial(_out_ffn_kernel, n_mix=n_mix, final_norm=final_norm),
        grid=(m // tm, nj),
        in_specs=in_specs,
        out_specs=pl.BlockSpec((tm, d), lambda i, j: (i, 0)),
        out_shape=jax.ShapeDtypeStruct((m, d), F32),
        scratch_shapes=[pltpu.VMEM((tm, d), F32), pltpu.VMEM((tm, d), BF16), pltpu.VMEM((tm, d), F32)],
        compiler_params=_cparams(("parallel", "arbitrary")),
        name="out_ffn",
    )(*args)


def _gla_chunk(q, k, v, g, st):
    c = q.shape[0]
    tri = (_iota((c, c), 0) >= _iota((c, c), 1)).astype(BF16)
    g_hi = g.astype(BF16)
    g_r1 = g - g_hi.astype(F32)
    g_mid = g_r1.astype(BF16)
    g_lo = (g_r1 - g_mid.astype(F32)).astype(BF16)
    b = (jnp.dot(tri, g_hi, preferred_element_type=F32) + jnp.dot(tri, g_mid, preferred_element_type=F32)
         + jnp.dot(tri, g_lo, preferred_element_type=F32))
    o = lax.dot_general((q * jnp.exp(b)).astype(BF16), st.astype(BF16), NT_DIMS, preferred_element_type=F32)
    lane = _iota((SUB, c), 1)
    row = _iota((SUB, c), 0)
    hs = SUB // 2
    rows = []
    for blk in range(c // SUB):
        lo = blk * SUB
        b_i, q_i, k_i = b[lo:lo + SUB], q[lo:lo + SUB], k[lo:lo + SUB]
        a_top = jnp.zeros((hs, c), F32)
        a_bot = jnp.zeros((hs, c), F32)
        for s in range(SUB):
            if s < hs:
                e = jnp.exp(b_i - b_i[s:s + 1])
                a = jnp.sum(q_i * e * k_i[s:s + 1], axis=1, keepdims=True)
                a_top = jnp.where(lane[:hs] == lo + s, a[:hs], a_top)
                a_bot = jnp.where(lane[:hs] == lo + s, a[hs:], a_bot)
            else:
                e = jnp.exp(b_i[hs:] - b_i[s:s + 1])
                a = jnp.sum(q_i[hs:] * e * k_i[s:s + 1], axis=1, keepdims=True)
                a_bot = jnp.where(lane[:hs] == lo + s, a, a_bot)
        a_blk = jnp.where(lane <= lo + row, jnp.concatenate([a_top, a_bot], axis=0), 0.0)
        if blk > 0:
            r = b[lo - 1:lo]
            qt = q_i * jnp.exp(b_i - r)
            kt = k * jnp.exp(jnp.minimum(r - b, 0.0))
            a_off = lax.dot_general(qt.astype(BF16), kt.astype(BF16), NT_DIMS, preferred_element_type=F32)
            a_blk = jnp.where(lane < lo, a_off, a_blk)
        rows.append(a_blk)
    a_full = jnp.concatenate(rows, axis=0)
    o = o + jnp.dot(a_full.astype(BF16), v.astype(BF16), preferred_element_type=F32)
    b_last = b[c - 1:c]
    kd = k * jnp.exp(b_last - b)
    st_new = st * jnp.exp(b_last) + lax.dot_general(v.astype(BF16), kd.astype(BF16), TN_DIMS,
                                                    preferred_element_type=F32)
    return o, st_new


def _lower_bound(lbl_ref, col):
    l = lbl_ref[:, col:col + HG_DK]
    e = jnp.exp(l - jnp.max(l, axis=0, keepdims=True))
    return e[0:1] / jnp.sum(e, axis=0, keepdims=True)


def _hgrn_kernel(q_ref, f_ref, i_ref, og_ref, lbl_ref, gain_ref, o_ref, s_ref, st_ref, *, nch):
    t = pl.program_id(1)

    @pl.when(t == 0)
    def _():
        st_ref[...] = jnp.zeros_like(st_ref)

    def body(ci, carry):
        r0 = pl.multiple_of(ci * CHUNK, CHUNK)
        rs = pl.ds(r0, CHUNK)
        for h in range(HG_H):
            cs = slice(h * HG_DK, (h + 1) * HG_DK)
            lb = _lower_bound(lbl_ref, h * HG_DK)
            q = _silu(q_ref[rs, cs])
            f = lb + (1.0 - lb) * _sigmoid(f_ref[rs, cs])
            o, st_new = _gla_chunk(q, 1.0 - f, i_ref[rs, cs], jnp.log(f), st_ref[h])
            st_ref[h] = st_new
            o = _rms(o, gain_ref[...]) * _sigmoid(og_ref[rs, cs])
            o_ref[rs, cs] = o.astype(o_ref.dtype)
        return carry

    lax.fori_loop(0, nch, body, 0)

    @pl.when(t == pl.num_programs(1) - 1)
    def _():
        for h in range(HG_H):
            s_ref[0, h] = st_ref[h].T


def _hgrn_prompt(p2d, lb_logits, gain, bsz, seq, tb):
    nt = seq // tb
    hk = HG_H * HG_DK
    row = lambda b, t: b * nt + t
    col_spec = lambda c: pl.BlockSpec((tb, hk), lambda b, t, c=c: (row(b, t), c))
    return pl.pallas_call(
        functools.partial(_hgrn_kernel, nch=tb // CHUNK),
        grid=(bsz, nt),
        in_specs=[col_spec(EV_Q // hk), col_spec(EV_F // hk), col_spec(EV_I // hk), col_spec(EV_OG // hk),
                  pl.BlockSpec(lb_logits.shape, lambda b, t: (0, 0)),
                  pl.BlockSpec((1, HG_DV), lambda b, t: (0, 0))],
        out_specs=[pl.BlockSpec((tb, hk), lambda b, t: (row(b, t), 0)),
                   pl.BlockSpec((1, HG_H, HG_DK, HG_DV), lambda b, t: (b, 0, 0, 0))],
        out_shape=[jax.ShapeDtypeStruct((bsz * seq, hk), BF16),
                   jax.ShapeDtypeStruct((bsz, HG_H, HG_DK, HG_DV), F32)],
        scratch_shapes=[pltpu.VMEM((HG_H, HG_DV, HG_DK), F32)],
        compiler_params=_cparams(("parallel", "arbitrary")),
        name="hgrn_chunk",
    )(p2d, p2d, p2d, p2d, lb_logits, gain.reshape(1, HG_DV))


def _log_sigmoid(x):
    return jnp.minimum(x, 0.0) - jnp.log(1.0 + jnp.exp(-jnp.abs(x)))


def _gla_kernel(q_ref, k_ref, v_ref, r_ref, a_ref, wa_ref, ba_ref, gain_ref, o_ref, s_ref, st_ref, *, nch):
    t = pl.program_id(1)

    @pl.when(t == 0)
    def _():
        st_ref[...] = jnp.zeros_like(st_ref)

    def body(ci, carry):
        r0 = pl.multiple_of(ci * CHUNK, CHUNK)
        rs = pl.ds(r0, CHUNK)
        gate = jnp.dot(a_ref[rs, :].astype(BF16), wa_ref[...], preferred_element_type=F32) + ba_ref[...]
        logf = _log_sigmoid(gate) * (1.0 / GLA_TAU)
        for h in range(GLA_H):
            ks = slice(h * GLA_DK, (h + 1) * GLA_DK)
            vs = slice(h * GLA_DV, (h + 1) * GLA_DV)
            q = q_ref[rs, ks] * (GLA_DK ** -0.5)
            o, st_new = _gla_chunk(q, k_ref[rs, ks], v_ref[rs, vs], logf[:, ks], st_ref[h])
            st_ref[h] = st_new
            o = _rms(o, gain_ref[...]) * _silu(r_ref[rs, vs])
            o_ref[rs, vs] = o.astype(o_ref.dtype)
        return carry

    lax.fori_loop(0, nch, body, 0)

    @pl.when(t == pl.num_programs(1) - 1)
    def _():
        for h in range(GLA_H):
            s_ref[0, h] = st_ref[h].T


def _gla_prompt(p2d, wa2p, ba, gain, bsz, seq, tb):
    nt = seq // tb
    hk, hv = GLA_H * GLA_DK, GLA_H * GLA_DV
    row = lambda b, t: b * nt + t
    return pl.pallas_call(
        functools.partial(_gla_kernel, nch=tb // CHUNK),
        grid=(bsz, nt),
        in_specs=[pl.BlockSpec((tb, hk), lambda b, t: (row(b, t), OD_Q // hk)),
                  pl.BlockSpec((tb, hk), lambda b, t: (row(b, t), OD_K // hk)),
                  pl.BlockSpec((tb, hv), lambda b, t: (row(b, t), OD_V // hv)),
                  pl.BlockSpec((tb, hv), lambda b, t: (row(b, t), OD_R // hv)),
                  pl.BlockSpec((tb, LANES), lambda b, t: (row(b, t), OD_A // LANES)),
                  pl.BlockSpec(wa2p.shape, lambda b, t: (0, 0)),
                  pl.BlockSpec((1, hk), lambda b, t: (0, 0)),
                  pl.BlockSpec((1, GLA_DV), lambda b, t: (0, 0))],
        out_specs=[pl.BlockSpec((tb, hv), lambda b, t: (row(b, t), 0)),
                   pl.BlockSpec((1, GLA_H, GLA_DK, GLA_DV), lambda b, t: (b, 0, 0, 0))],
        out_shape=[jax.ShapeDtypeStruct((bsz * seq, hv), BF16),
                   jax.ShapeDtypeStruct((bsz, GLA_H, GLA_DK, GLA_DV), F32)],
        scratch_shapes=[pltpu.VMEM((GLA_H, GLA_DV, GLA_DK), F32)],
        compiler_params=_cparams(("parallel", "arbitrary")),
        name="gla_chunk",
    )(p2d, p2d, p2d, p2d, p2d, wa2p, ba.reshape(1, hk), gain.reshape(1, GLA_DV))


def _to_columns(x):
    bsz = x.shape[0]
    if bsz < LANES:
        x = jnp.concatenate([x, jnp.zeros((LANES - bsz, x.shape[1]), x.dtype)], axis=0)
    return x.T


def _decode_update(q, k, v, g, s_ref, so_ref, o_scr):
    bsz = q.shape[0]
    qt, kt, et = _to_columns(q), _to_columns(k), _to_columns(jnp.exp(g))
    for b in range(bsz):
        s_new = et[:, b:b + 1] * s_ref[b, 0] + kt[:, b:b + 1] * v[b:b + 1, :]
        so_ref[b, 0] = s_new
        o_scr[b:b + 1, :] = jnp.sum(qt[:, b:b + 1] * s_new, axis=0, keepdims=True)


def _hgrn_decode_kernel(q_ref, f_ref, i_ref, og_ref, lbl_ref, gain_ref, s_ref, o_ref, so_ref, o_scr):
    l = lbl_ref[0]
    e = jnp.exp(l - jnp.max(l, axis=0, keepdims=True))
    lb = e[0:1] / jnp.sum(e, axis=0, keepdims=True)
    f = lb + (1.0 - lb) * _sigmoid(f_ref[...])
    _decode_update(_silu(q_ref[...]), 1.0 - f, i_ref[...], jnp.log(f), s_ref, so_ref, o_scr)
    o_ref[...] = _rms(o_scr[...], gain_ref[...]) * _sigmoid(og_ref[...])


def _hgrn_decode(p2d, lb_logits, gain, state):
    bsz = p2d.shape[0]
    col = lambda c: pl.BlockSpec((bsz, HG_DK), lambda h, c=c: (0, c + h))
    lbl3 = lb_logits.reshape(lb_logits.shape[0], HG_H, HG_DK).transpose(1, 0, 2)
    st_spec = pl.BlockSpec((bsz, 1, HG_DK, HG_DV), lambda h: (0, h, 0, 0))
    return pl.pallas_call(
        _hgrn_decode_kernel,
        grid=(HG_H,),
        in_specs=[col(EV_Q // HG_DK), col(EV_F // HG_DK), col(EV_I // HG_DK), col(EV_OG // HG_DK),
                  pl.BlockSpec((1,) + lbl3.shape[1:], lambda h: (h, 0, 0)),
                  pl.BlockSpec((1, HG_DV), lambda h: (0, 0)),
                  st_spec],
        out_specs=[pl.BlockSpec((bsz, HG_DV), lambda h: (0, h)), st_spec],
        out_shape=[jax.ShapeDtypeStruct((bsz, HG_H * HG_DV), F32),
                   jax.ShapeDtypeStruct(state.shape, F32)],
        scratch_shapes=[pltpu.VMEM((bsz, HG_DV), F32)],
        compiler_params=_cparams(("arbitrary",)),
        name="hgrn_decode",
    )(p2d, p2d, p2d, p2d, lbl3, gain.reshape(1, HG_DV), state)


def _gla_decode_kernel(q_ref, k_ref, v_ref, r_ref, a_ref, wa_ref, ba_ref, gain_ref, s_ref, o_ref, so_ref, o_scr):
    gate = jnp.dot(a_ref[...].astype(BF16), wa_ref[...], preferred_element_type=F32) + ba_ref[...]
    logf = _log_sigmoid(gate) * (1.0 / GLA_TAU)
    _decode_update(q_ref[...] * (GLA_DK ** -0.5), k_ref[...], v_ref[...], logf, s_ref, so_ref, o_scr)
    o_ref[...] = _rms(o_scr[...], gain_ref[...]) * _silu(r_ref[...])


def _gla_decode(p2d, wa2p, ba, gain, state):
    bsz = p2d.shape[0]
    hk = GLA_H * GLA_DK
    st_spec = pl.BlockSpec((bsz, 1, GLA_DK, GLA_DV), lambda h: (0, h, 0, 0))
    return pl.pallas_call(
        _gla_decode_kernel,
        grid=(GLA_H,),
        in_specs=[pl.BlockSpec((bsz, GLA_DK), lambda h: (0, OD_Q // GLA_DK + h)),
                  pl.BlockSpec((bsz, GLA_DK), lambda h: (0, OD_K // GLA_DK + h)),
                  pl.BlockSpec((bsz, GLA_DV), lambda h: (0, OD_V // GLA_DV + h)),
                  pl.BlockSpec((bsz, GLA_DV), lambda h: (0, OD_R // GLA_DV + h)),
                  pl.BlockSpec((bsz, LANES), lambda h: (0, OD_A // LANES)),
                  pl.BlockSpec((LANES, GLA_DK), lambda h: (0, h)),
                  pl.BlockSpec((1, GLA_DK), lambda h: (0, h)),
                  pl.BlockSpec((1, GLA_DV), lambda h: (0, 0)),
                  st_spec],
        out_specs=[pl.BlockSpec((bsz, GLA_DV), lambda h: (0, h)), st_spec],
        out_shape=[jax.ShapeDtypeStruct((bsz, GLA_H * GLA_DV), F32),
                   jax.ShapeDtypeStruct(state.shape, F32)],
        scratch_shapes=[pltpu.VMEM((bsz, GLA_DV), F32)],
        compiler_params=_cparams(("arbitrary",)),
        name="gla_decode",
    )(p2d, p2d, p2d, p2d, p2d, wa2p, ba.reshape(1, hk), gain.reshape(1, GLA_DV), state)


def _compress_kernel(xa_ref, xb_ref, pe_ref, w1_ref, w2_ref, o_ref, acc_ref):
    nb = o_ref.shape[0]
    acc_ref[...] = jnp.zeros_like(acc_ref)
    for l in range(NSA_BLK):
        rows = pl.ds(l, nb, stride=NSA_BLK)
        x = jnp.concatenate([xa_ref[rows, :], xb_ref[rows, :]], axis=1) + pe_ref[l:l + 1, :]
        acc_ref[...] += jnp.dot(x.astype(BF16), w1_ref[l], preferred_element_type=F32)
    o_ref[...] = jnp.dot(_silu(acc_ref[...]).astype(BF16), w2_ref[...], preferred_element_type=F32)


def _compress_prompt(p2d, pe4, w1bd, w2bd):
    n = p2d.shape[0]
    rc = min(8192, n)
    cw = 4 * NSA_DH
    c0 = PM_CMP // LANES
    return pl.pallas_call(
        _compress_kernel,
        grid=(n // rc,),
        in_specs=[pl.BlockSpec((rc, LANES), lambda i: (i, c0)),
                  pl.BlockSpec((rc, LANES), lambda i: (i, c0 + 1)),
                  pl.BlockSpec(pe4.shape, lambda i: (0, 0)),
                  pl.BlockSpec(w1bd.shape, lambda i: (0, 0, 0)),
                  pl.BlockSpec((cw, cw), lambda i: (0, 0))],
        out_specs=pl.BlockSpec((rc // NSA_BLK, cw), lambda i: (i, 0)),
        out_shape=jax.ShapeDtypeStruct((n // NSA_BLK, cw), F32),
        scratch_shapes=[pltpu.VMEM((rc // NSA_BLK, cw), F32)],
        compiler_params=_cparams(("arbitrary",)),
        name="nsa_compress",
    )(p2d, p2d, pe4, w1bd, w2bd)


AQB = 256
AQB_SHIFT = 8
AUG = 128
ONES_ROWS = 16


def _slopes_lane(g, lanes):
    r = _iota((1, lanes), 1) >> AQB_SHIFT
    out = jnp.zeros((1, lanes), F32)
    for rr in range(NSA_R):
        out = jnp.where(r == rr, 2.0 ** (-(g * NSA_R + rr + 1)), out)
    return out


def _build_key_features(src_ref, row0, dst_ref, g, seq, with_onehot):
    r = _iota((NSA_DH, LANES), 0)
    for cb in range(seq // LANES):
        kpos = cb * LANES + _iota((NSA_DH, LANES), 1)
        feat = jnp.where(r == 32, (kpos >> 6).astype(F32),
                         jnp.where(r == 33, (kpos & 63).astype(F32),
                                   jnp.where((r == 34) | (r == 35), 1.0, 0.0)))
        if with_onehot:
            feat = jnp.where((kpos >> 6) == r, 1.0, feat)
        kt = src_ref[0, row0:row0 + NSA_DH, cb * LANES:(cb + 1) * LANES]
        dst_ref[g, cb * LANES:(cb + 1) * LANES, :] = jnp.concatenate([kt, feat], axis=0).T.astype(BF16)


def _attn_tile(k_ref, g, k0, v_ref, v_row0, qa_ref, mask, m_ref, acc_ref, stream, nkeys=AQB):
    s = jnp.dot(k_ref[g, pl.ds(k0, nkeys), :], qa_ref[g], preferred_element_type=F32)
    if mask is not None:
        s = jnp.where(mask, s, NEG)
    m_old = m_ref[stream]
    m_new = jnp.maximum(m_old, jnp.max(s, axis=0, keepdims=True))
    p = jnp.exp(s - m_new).astype(BF16)
    vt = v_ref[0, v_row0:v_row0 + NSA_DH, pl.ds(k0, nkeys)].astype(BF16)
    vt = jnp.concatenate([vt, jnp.ones((ONES_ROWS, nkeys), BF16)], axis=0)
    acc_ref[stream] = jnp.exp(m_old - m_new) * acc_ref[stream] + jnp.dot(vt, p, preferred_element_type=F32)
    m_ref[stream] = m_new


def _nsa_attn_kernel(qt_ref, rows_ref, win_ref, gt_ref, kc_ref, vct_ref, o_ref, ks_s, kw_s, qa_s, m_s, acc_s,
                     *, nblk):
    qb = pl.program_id(1)
    seq = rows_ref.shape[2]
    q0 = qb * AQB
    nq = NSA_R * AQB
    half = NSA_G * NSA_DH

    @pl.when(qb == 0)
    def _():
        for g in range(NSA_G):
            _build_key_features(rows_ref, 2 * half + g * NSA_DH, ks_s, g, seq, True)
            _build_key_features(win_ref, g * NSA_DH, kw_s, g, seq, False)

    pq = q0 + (_iota((1, nq), 1) & (AQB - 1))
    gates = _sigmoid(gt_ref[0])
    ksel = min(NSA_TOPK, nblk)
    scale = NSA_DH ** -0.5
    key_i = _iota((AQB, nq), 0)
    qry_t = _iota((AQB, nq), 1) & (AQB - 1)
    causal = key_i <= qry_t
    beyond = key_i > qry_t
    pq_f = q0 + _iota((32, AQB), 1)
    feat_row = _iota((32, AQB), 0)

    m_s[...] = jnp.full(m_s.shape, NEG, F32)
    acc_s[...] = jnp.zeros_like(acc_s)

    def result(i):
        return acc_s[i, 0:NSA_DH, :] / jnp.maximum(acc_s[i, NSA_DH:NSA_DH + 1, :], TINY)

    o_cmp = []
    for g in range(NSA_G):
        slope = _slopes_lane(g, nq)

        qt_g = [qt_ref[0, (g * NSA_R + r) * NSA_DH:(g * NSA_R + r + 1) * NSA_DH, :] * scale for r in range(NSA_R)]
        qs_t = jnp.concatenate(qt_g, axis=1).astype(BF16)
        sc = jnp.dot(kc_ref[0, g], qs_t, preferred_element_type=F32)
        dist_c = pq - (_iota((nblk, nq), 0) * NSA_BLK + NSA_BLK - 1)
        mask_c = dist_c >= 0
        sc = jnp.where(mask_c, sc - slope * dist_c.astype(F32), NEG)
        pc = jnp.where(mask_c, jnp.exp(sc - jnp.max(sc, axis=0, keepdims=True)), 0.0)
        pc = pc / jnp.maximum(jnp.sum(pc, axis=0, keepdims=True), TINY)
        o_cmp.append(jnp.dot(vct_ref[0, g], pc.astype(BF16), preferred_element_type=F32))
        imp = pc[:, 0:AQB]
        for r in range(1, NSA_R):
            imp = imp + pc[:, r * AQB:(r + 1) * AQB]

        n_io = _iota((nblk, AQB), 0)
        tpos = q0 + _iota((nblk, AQB), 1)
        forced = (n_io == (tpos >> 6)) | (n_io == 0)
        started = n_io * NSA_BLK <= tpos
        score = jnp.where(forced, jnp.inf, jnp.where(started, imp, -jnp.inf))
        rank = jnp.zeros((nblk, AQB), I32)
        for i in range(nblk):
            row = score[i:i + 1, :]
            rank = rank + ((row > score) | ((row == score) & (i < n_io))).astype(I32)
        sel_bias = jnp.where((rank < ksel) & started, 0.0, NEG)
        if nblk < 32:
            sel_bias = jnp.concatenate([sel_bias, jnp.zeros((32 - nblk, AQB), F32)], axis=0)

        cols = []
        for r in range(NSA_R):
            sl = 2.0 ** (-(g * NSA_R + r + 1))
            pos_feat = jnp.where(feat_row == 0, 64.0 * sl,
                                 jnp.where(feat_row == 1, sl,
                                           jnp.where(feat_row == 2, (-64.0 * sl) * (pq_f >> 6).astype(F32),
                                                     jnp.where(feat_row == 3, (-sl) * (pq_f & 63).astype(F32), 0.0))))
            cols.append(jnp.concatenate([qt_g[r], sel_bias, pos_feat], axis=0))
        qa_s[g] = jnp.concatenate(cols, axis=1).astype(BF16)

    def slc_tile(k0, mask, nkeys=AQB):
        for g in range(NSA_G):
            _attn_tile(ks_s, g, k0, rows_ref, 3 * half + g * NSA_DH, qa_s, mask, m_s, acc_s, g, nkeys)

    def win_tile(k0, mask):
        for g in range(NSA_G):
            _attn_tile(kw_s, g, k0, win_ref, half + g * NSA_DH, qa_s, mask, m_s, acc_s, NSA_G + g)

    def slc_body(kt, c):
        slc_tile(pl.multiple_of(kt * 2 * AQB, 2 * AQB), None, 2 * AQB)
        return c

    lax.fori_loop(0, qb >> 1, slc_body, 0)

    @pl.when((qb & 1) == 1)
    def _():
        slc_tile(pl.multiple_of(q0 - AQB, AQB), None)

    @pl.when(qb >= 2)
    def _():
        win_tile(pl.multiple_of(q0 - 2 * AQB, AQB), beyond)

    @pl.when(qb >= 1)
    def _():
        win_tile(pl.multiple_of(q0 - AQB, AQB), None)

    slc_tile(pl.multiple_of(q0, AQB), causal)
    win_tile(pl.multiple_of(q0, AQB), causal)

    for g in range(NSA_G):
        def gate_row(j, g=g):
            return jnp.concatenate([gates[g * 12 + r * 3 + j:g * 12 + r * 3 + j + 1, :] for r in range(NSA_R)],
                                   axis=1)

        o_t = gate_row(0) * o_cmp[g] + gate_row(1) * result(g) + gate_row(2) * result(NSA_G + g)
        o_st = jnp.concatenate([o_t[:, r * AQB:(r + 1) * AQB] for r in range(NSA_R)], axis=0)
        o_ref[:, g * NSA_R * NSA_DH:(g + 1) * NSA_R * NSA_DH] = o_st.T.astype(o_ref.dtype)


def _nsa_attn_prompt(qt, rows_t, win_t, gate_t, kc, vct, bsz, seq):
    assert NSA_WINDOW == 2 * AQB and seq % AQB == 0 and seq // NSA_BLK <= 32
    nqb = seq // AQB
    nblk = seq // NSA_BLK
    qw = NSA_H * NSA_DH
    nq = NSA_R * AQB
    per_b = lambda a: pl.BlockSpec((1,) + a.shape[1:], lambda b, i: (b,) + (0,) * (a.ndim - 1))
    per_q = lambda a: pl.BlockSpec((1, a.shape[1], AQB), lambda b, i: (b, 0, i))
    return pl.pallas_call(
        functools.partial(_nsa_attn_kernel, nblk=nblk),
        grid=(bsz, nqb),
        in_specs=[per_q(qt), per_b(rows_t), per_b(win_t), per_q(gate_t), per_b(kc), per_b(vct)],
        out_specs=pl.BlockSpec((AQB, qw), lambda b, i: (b * nqb + i, 0)),
        out_shape=jax.ShapeDtypeStruct((bsz * seq, qw), BF16),
        scratch_shapes=[pltpu.VMEM((NSA_G, seq, AUG), BF16), pltpu.VMEM((NSA_G, seq, AUG), BF16),
                        pltpu.VMEM((NSA_G, AUG, nq), BF16),
                        pltpu.VMEM((2 * NSA_G, 1, nq), F32), pltpu.VMEM((2 * NSA_G, NSA_DH + ONES_ROWS, nq), F32)],
        compiler_params=_cparams(("parallel", "arbitrary")),
        name="nsa_attn",
    )(qt, rows_t, win_t, gate_t, kc, vct)


QROWS = 16


def _head_slopes(rows):
    r = _iota((rows, 1), 0)
    out = jnp.zeros((rows, 1), F32)
    for h in range(NSA_H):
        out = jnp.where(r == h, 2.0 ** (-(h + 1)), out)
    return out


def _softmax_rows(s, valid):
    s = jnp.where(valid, s, NEG)
    p = jnp.where(valid, jnp.exp(s - jnp.max(s, axis=1, keepdims=True)), 0.0)
    return p / jnp.maximum(jnp.sum(p, axis=1, keepdims=True), TINY)


def _padded_queries(q_ref):
    q = q_ref[0] * (NSA_DH ** -0.5)
    return jnp.concatenate([q, jnp.zeros((QROWS - NSA_H, NSA_DH), F32)], axis=0).astype(BF16)


PLANES = 4 * NSA_G
PAGE_ROWS = PLANES * NSA_DH
CMP_PLANES = 2 * NSA_G


def _dec_cmp_kernel(pt_ref, cache_ref, pe_ref, w1_ref, w2_ref, q_ref, o_ref, sel_ref, buf, acc_ref, cmp_ref, sem,
                    *, npages, nsel):
    b = pl.program_id(0)
    nb = npages * (PAGE // NSA_BLK)
    nrow = npages * CMP_PLANES
    crows = CMP_PLANES * NSA_DH

    def page_copy(pg):
        src = pl.multiple_of(pt_ref[b * npages + pg] * PAGE_ROWS, PAGE_ROWS)
        return pltpu.make_async_copy(cache_ref.at[pl.ds(src, crows), :],
                                     buf.at[pl.ds(pl.multiple_of(pg * crows, crows), crows), :], sem.at[0])

    def start(pg, c):
        page_copy(pg).start()
        return c

    def wait(pg, c):
        page_copy(pg).wait()
        return c

    lax.fori_loop(0, npages, start, 0)
    lax.fori_loop(0, npages, wait, 0)

    is_k = ((_iota((nrow, 1), 0) >> 1) & 1) == 0
    acc_ref[...] = jnp.zeros_like(acc_ref)
    for d in range(NSA_DH):
        x = buf[pl.ds(d, nrow, stride=NSA_DH), :]
        x = x + jnp.where(is_k, pe_ref[d, 0:1, :], pe_ref[d, 1:2, :])
        acc_ref[...] += jnp.dot(x.astype(BF16), w1_ref[d], preferred_element_type=F32)
    acc = acc_ref[...]
    h = _silu(jnp.where(is_k, acc[:, :LANES], acc[:, LANES:]))
    c2 = jnp.dot(h.astype(BF16), w2_ref[...], preferred_element_type=F32)
    cmp_ref[...] = jnp.where(is_k, c2[:, :LANES], c2[:, LANES:])

    pos = nb * NSA_BLK
    q16 = _padded_queries(q_ref)
    slope = _head_slopes(QROWS)
    row_grp = _iota((QROWS, 1), 0) >> 2
    lane = _iota((1, nb), 1)
    blk_of = jnp.where(lane < npages, 2 * lane, 2 * (lane - npages) + 1)
    dist = (pos - (blk_of * NSA_BLK + NSA_BLK - 1)).astype(F32)
    ri, ci = _iota((nb, nb), 0), _iota((nb, nb), 1)
    blk_r = jnp.where(ri < npages, 2 * ri, 2 * (ri - npages) + 1)
    blk_c = jnp.where(ci < npages, 2 * ci, 2 * (ci - npages) + 1)
    o_all = jnp.zeros((QROWS, NSA_DH), F32)
    for g in range(NSA_G):
        kc = cmp_ref[pl.ds(g, npages, stride=CMP_PLANES), :].astype(BF16)
        vc = cmp_ref[pl.ds(NSA_G + g, npages, stride=CMP_PLANES), :].astype(BF16)
        s = jnp.concatenate([lax.dot_general(q16, kc[:, j * NSA_DH:(j + 1) * NSA_DH], NT_DIMS,
                                             preferred_element_type=F32) for j in range(2)], axis=1)
        p = _softmax_rows(s - slope * dist, dist >= 0)
        pb = p.astype(BF16)
        o_g = sum(jnp.dot(pb[:, j * npages:(j + 1) * npages], vc[:, j * NSA_DH:(j + 1) * NSA_DH],
                          preferred_element_type=F32) for j in range(2))
        o_all = jnp.where(row_grp == g, o_g, o_all)
        imp = jnp.sum(jnp.where(row_grp == g, p, 0.0), axis=0, keepdims=True)
        score_row = jnp.where(blk_of == 0, jnp.inf, imp)
        score_col = jnp.sum(jnp.where(ri == ci, jnp.broadcast_to(score_row, (nb, nb)), 0.0),
                            axis=1, keepdims=True)
        beats = (score_col > score_row) | ((score_col == score_row) & (blk_r < blk_c))
        rank = jnp.sum(beats.astype(I32), axis=0, keepdims=True)
        sel_ref[0, g:g + 1, :] = (rank < nsel).astype(F32)
    o_ref[0] = o_all[:NSA_H]


def _dec_cmp(page_table, cache2, pe_dec, w1dec, w2dec, q3, nsel):
    bsz, npages = page_table.shape
    nb = npages * (PAGE // NSA_BLK)
    nrow = npages * CMP_PLANES
    grid_spec = pltpu.PrefetchScalarGridSpec(
        num_scalar_prefetch=1,
        grid=(bsz,),
        in_specs=[pl.BlockSpec(memory_space=pl.ANY),
                  pl.BlockSpec(pe_dec.shape, lambda b, pt: (0, 0, 0)),
                  pl.BlockSpec(w1dec.shape, lambda b, pt: (0, 0, 0)),
                  pl.BlockSpec(w2dec.shape, lambda b, pt: (0, 0)),
                  pl.BlockSpec((1, NSA_H, NSA_DH), lambda b, pt: (b, 0, 0))],
        out_specs=[pl.BlockSpec((1, NSA_H, NSA_DH), lambda b, pt: (b, 0, 0)),
                   pl.BlockSpec((1, NSA_G, nb), lambda b, pt: (b, 0, 0))],
        scratch_shapes=[pltpu.VMEM((nrow * NSA_DH, LANES), F32), pltpu.VMEM((nrow, 2 * LANES), F32),
                        pltpu.VMEM((nrow, LANES), F32), pltpu.SemaphoreType.DMA((1,))])
    return pl.pallas_call(
        functools.partial(_dec_cmp_kernel, npages=npages, nsel=nsel),
        grid_spec=grid_spec,
        out_shape=[jax.ShapeDtypeStruct((bsz, NSA_H, NSA_DH), F32),
                   jax.ShapeDtypeStruct((bsz, NSA_G, nb), F32)],
        compiler_params=_cparams(("arbitrary",)),
        name="nsa_dec_cmp",
    )(page_table.reshape(-1), cache2, pe_dec, w1dec, w2dec, q3)


def _attend_with_self(q16, slope, kt, vt, dist, valid, k_self, v_self):
    s = jnp.dot(q16, kt.astype(BF16), preferred_element_type=F32) - slope * dist
    s_self = jnp.sum(q16.astype(F32) * k_self.astype(BF16).astype(F32), axis=1, keepdims=True)
    m = jnp.maximum(jnp.max(jnp.where(valid, s, NEG), axis=1, keepdims=True), s_self)
    p = jnp.where(valid, jnp.exp(s - m), 0.0)
    p_self = jnp.exp(s_self - m)
    num = lax.dot_general(p.astype(BF16), vt.astype(BF16), NT_DIMS, preferred_element_type=F32) + p_self * v_self
    return num / jnp.maximum(jnp.sum(p, axis=1, keepdims=True) + p_self, TINY)


def _dec_attn_kernel(pt_ref, idx_ref, cache_ref, q_ref, oc_ref, kn_ref, win_ref, gt_ref, o_ref, kbuf, vbuf, sem,
                     *, npages, nsel, wlen):
    b = pl.program_id(0)
    pos = npages * PAGE

    copies = []
    for g in range(NSA_G):
        for j in range(nsel):
            blk = idx_ref[(b * NSA_G + g) * NSA_TOPK + j]
            base = pt_ref[b * npages + (blk >> 1)] * PAGE_ROWS
            dst = pl.ds(j * PAGE, PAGE)
            k_rows = pl.ds(pl.multiple_of(base + (2 * NSA_G + g) * NSA_DH, NSA_DH), NSA_DH)
            v_rows = pl.ds(pl.multiple_of(base + (3 * NSA_G + g) * NSA_DH, NSA_DH), NSA_DH)
            copies.append(pltpu.make_async_copy(cache_ref.at[k_rows, :], kbuf.at[g, :, dst], sem.at[0]))
            copies.append(pltpu.make_async_copy(cache_ref.at[v_rows, :], vbuf.at[g, :, dst], sem.at[0]))
    for cp in copies:
        cp.start()
    for cp in copies:
        cp.wait()

    q16 = _padded_queries(q_ref)
    slope = _head_slopes(QROWS)
    row_grp = _iota((QROWS, 1), 0) >> 2
    kn = kn_ref[0]
    lane = _iota((1, nsel * PAGE), 1)
    within = lane & (PAGE - 1)
    wdist = wlen - _iota((1, wlen), 1)
    o_slc = jnp.zeros((QROWS, NSA_DH), F32)
    o_win = jnp.zeros((QROWS, NSA_DH), F32)
    for g in range(NSA_G):
        gs = slice(g * NSA_DH, (g + 1) * NSA_DH)
        blk_lane = jnp.zeros((1, nsel * PAGE), I32)
        for j in range(nsel):
            blk_lane = jnp.where((lane >> 7) == j, idx_ref[(b * NSA_G + g) * NSA_TOPK + j], blk_lane)
        valid = (within >> 6) == (blk_lane & 1)
        kpos = (blk_lane >> 1) * PAGE + within
        o_g = _attend_with_self(q16, slope, kbuf[g], vbuf[g], (pos - kpos).astype(F32), valid,
                                kn[0:1, gs], kn[1:2, gs])
        o_slc = jnp.where(row_grp == g, o_g, o_slc)
        o_g = _attend_with_self(q16, slope, win_ref[0, g * NSA_DH:(g + 1) * NSA_DH, :],
                                win_ref[0, (NSA_G + g) * NSA_DH:(NSA_G + g + 1) * NSA_DH, :],
                                wdist.astype(F32), wdist < NSA_WINDOW, kn[2:3, gs], kn[3:4, gs])
        o_win = jnp.where(row_grp == g, o_g, o_win)

    gates = _sigmoid(gt_ref[pl.ds(b, 1), :])
    r_io = _iota((QROWS, LANES), 0)
    l_io = _iota((QROWS, LANES), 1)

    def gate_col(j):
        return jnp.sum(jnp.where(l_io == 3 * r_io + j, gates, 0.0), axis=1, keepdims=True)

    o_cmp = jnp.concatenate([oc_ref[0], jnp.zeros((QROWS - NSA_H, NSA_DH), F32)], axis=0)
    o = gate_col(0) * o_cmp + gate_col(1) * o_slc + gate_col(2) * o_win
    o_ref[0] = o[:NSA_H]


def _dec_attn(page_table, idx, cache2, q3, ocmp, knew, win_t, gates, nsel):
    bsz, npages = page_table.shape
    wlen = win_t.shape[2]
    hd = pl.BlockSpec((1, NSA_H, NSA_DH), lambda b, pt, ix: (b, 0, 0))
    grid_spec = pltpu.PrefetchScalarGridSpec(
        num_scalar_prefetch=2,
        grid=(bsz,),
        in_specs=[pl.BlockSpec(memory_space=pl.ANY), hd, hd,
                  pl.BlockSpec((1,) + knew.shape[1:], lambda b, pt, ix: (b, 0, 0)),
                  pl.BlockSpec((1,) + win_t.shape[1:], lambda b, pt, ix: (b, 0, 0)),
                  pl.BlockSpec(gates.shape, lambda b, pt, ix: (0, 0))],
        out_specs=hd,
        scratch_shapes=[pltpu.VMEM((NSA_G, NSA_DH, nsel * PAGE), F32), pltpu.VMEM((NSA_G, NSA_DH, nsel * PAGE), F32),
                        pltpu.SemaphoreType.DMA((1,))])
    return pl.pallas_call(
        functools.partial(_dec_attn_kernel, npages=npages, nsel=nsel, wlen=wlen),
        grid_spec=grid_spec,
        out_shape=jax.ShapeDtypeStruct((bsz, NSA_H, NSA_DH), F32),
        compiler_params=_cparams(("arbitrary",)),
        name="nsa_dec_attn",
    )(page_table.reshape(-1), idx.reshape(-1), cache2, q3, ocmp, knew, win_t, gates)


def _prep_weights(w_in_even, nsa_cmp_pos, nsa_cmp_w1, nsa_cmp_w2, w_out_even, w_in_odd, gla_wa2, w_out_odd,
                  ffn_w13, ffn_w2):
    d = D_MODEL
    wie = jnp.pad(w_in_even[0], ((0, 0), (0, PROJ_W - w_in_even.shape[2]))).astype(BF16)
    a0 = 2 * GLA_H * GLA_DK + GLA_H * GLA_DV
    wo = w_in_odd[0]
    wio = jnp.concatenate([wo[:, :a0], wo[:, a0 + GLA_RANK:], wo[:, a0:a0 + GLA_RANK]], axis=1)
    wio = jnp.pad(wio, ((0, 0), (0, ODD_W - wio.shape[1]))).astype(BF16)
    wa2p = jnp.pad(gla_wa2[0], ((0, LANES - GLA_RANK), (0, 0))).astype(BF16)
    hv = HG_H * HG_DV
    woe_h, woe_n = w_out_even[0, :hv].astype(BF16), w_out_even[0, hv:].astype(BF16)
    woo = w_out_odd[0].astype(BF16)
    nj = D_FF // FF_TILE
    w13t, w2b = [], []
    for l in range(ffn_w13.shape[0]):
        gate = ffn_w13[l, :, :D_FF].reshape(d, nj, FF_TILE)
        up = ffn_w13[l, :, D_FF:].reshape(d, nj, FF_TILE)
        w13t.append(jnp.concatenate([gate, up], axis=2).reshape(d, 2 * D_FF).astype(BF16))
        w2b.append(ffn_w2[l].astype(BF16))
    w1 = nsa_cmp_w1[0].reshape(2, NSA_BLK, NSA_DH, NSA_DH)
    w2 = nsa_cmp_w2[0]
    cw = 4 * NSA_DH
    eye = jnp.eye(2 * NSA_G, dtype=F32).reshape(2, NSA_G, 2, NSA_G)
    w1bd = jnp.einsum("cldj,cgCG->lcgdCGj", w1, eye).reshape(NSA_BLK, cw, cw)
    w2bd = jnp.einsum("cjd,cgCG->cgjCGd", w2, eye).reshape(cw, cw)
    pe = nsa_cmp_pos[0]
    pe4 = jnp.concatenate([pe[0], pe[0], pe[1], pe[1]], axis=1)
    we = w_in_even[0]
    hgw = 2 * HG_H * HG_DK + 2 * HG_H * HG_DV
    nq0 = hgw + NSA_H * NSA_DH
    w_main = jnp.concatenate([we[:, :hgw], we[:, nq0:nq0 + cw]], axis=1).astype(BF16)
    w_tr = jnp.pad(we[:, hgw:].T, ((0, T_END - (we.shape[1] - hgw)), (0, 0))).astype(BF16)
    eye2 = jnp.eye(2, dtype=F32)
    w1dec = jnp.einsum("cldo,jJ->djlcJo", w1, eye2).reshape(NSA_DH, 2 * NSA_BLK, 4 * NSA_DH)
    w2dec = jnp.einsum("cod,jJ->jocJd", w2, eye2).reshape(2 * NSA_DH, 4 * NSA_DH)
    pe_dec = jnp.tile(pe.transpose(2, 0, 1), (1, 1, 2))
    return dict(wie=wie, wio=wio, wa2p=wa2p, woe_h=woe_h, woe_n=woe_n, woo=woo, w13t=w13t, w2b=w2b,
                w1bd=w1bd.astype(BF16), w2bd=w2bd.astype(BF16), pe4=pe4, w_main=w_main, w_tr=w_tr,
                w1dec=w1dec.astype(BF16), w2dec=w2dec.astype(BF16), pe_dec=pe_dec)


def _forward_prompt(x, pw, hg_lb_logits, hg_norm, gla_ba, gla_norm, norm_mix, norm_ffn, norm_final):
    bsz, seq, d = x.shape
    n = bsz * seq
    x2d = x.reshape(n, d)
    tm = 512 if seq % 512 == 0 else 256
    tmf = next(t for t in (512, 256) if n % t == 0)
    tb = tm

    p, qt, rows_t, win_t, gate_t = _norm_proj_t(x2d, norm_mix[0], pw["w_main"], pw["w_tr"], bsz, seq, tm)
    o_h, hg_state = _hgrn_prompt(p, hg_lb_logits, hg_norm[0], bsz, seq, tb)
    cmp = _compress_prompt(p, pw["pe4"], pw["w1bd"], pw["w2bd"])
    nblk = seq // NSA_BLK
    cmp4 = cmp.reshape(bsz, nblk, 2 * NSA_G, NSA_DH).transpose(0, 2, 1, 3)
    kc = cmp4[:, :NSA_G].astype(BF16)
    vct = cmp4[:, NSA_G:].transpose(0, 1, 3, 2).astype(BF16)
    o_n = _nsa_attn_prompt(qt, rows_t, win_t, gate_t, kc, vct, bsz, seq)
    x1 = _out_ffn([o_h, o_n], x2d, [pw["woe_h"], pw["woe_n"]], norm_ffn[0], pw["w13t"][0], pw["w2b"][0], None, tmf)

    rows = rows_t.reshape(bsz, 4, NSA_G, NSA_DH, seq).transpose(0, 4, 1, 2, 3)[None]
    wn = min(NSA_WINDOW, seq)
    win = win_t[:, :, seq - wn:].reshape(bsz, 2, NSA_G, NSA_DH, wn).transpose(0, 4, 1, 2, 3)[None]

    p2 = _norm_proj(x1, norm_mix[1], pw["wio"], tm, ODD_W)
    o_g, gla_state = _gla_prompt(p2, pw["wa2p"], gla_ba[0], gla_norm[0], bsz, seq, tb)
    y = _out_ffn([o_g], x1, [pw["woo"]], norm_ffn[1], pw["w13t"][1], pw["w2b"][1], norm_final, tmf)
    return y.reshape(bsz, seq, d), rows, win, hg_state[None], gla_state[None]


def _forward_sample(x, pw, cache_nsa_kv, cache_win_kv, state_hgrn, state_gla, page_table, hg_lb_logits, hg_norm,
                    gla_ba, gla_norm, norm_mix, norm_ffn, norm_final):
    bsz, seq, d = x.shape
    x2d = x.reshape(bsz, d)
    npages = page_table.shape[1]
    nblk_total = npages * (PAGE // NSA_BLK) + 1
    nsel = min(NSA_TOPK, nblk_total) - 1

    p = _norm_proj(x2d, norm_mix[0], pw["wie"], bsz, PROJ_W // 2)
    o_h, hg_state = _hgrn_decode(p, hg_lb_logits, hg_norm[0], state_hgrn[0])

    cache2 = cache_nsa_kv[0].transpose(0, 2, 3, 4, 1).reshape(cache_nsa_kv.shape[1] * PAGE_ROWS, PAGE)
    q3 = p[:, EV_NQ:EV_CMP].reshape(bsz, NSA_H, NSA_DH)
    ocmp, sel = _dec_cmp(page_table, cache2, pw["pe_dec"], pw["w1dec"], pw["w2dec"], q3, nsel)
    sel = sel.reshape(bsz, NSA_G, 2, npages).transpose(0, 1, 3, 2).reshape(bsz, NSA_G, 2 * npages)
    idx = jnp.argsort(-sel, axis=-1, stable=True)[..., :NSA_TOPK].astype(I32)
    knew = p[:, EV_SLC:EV_GATE].reshape(bsz, 4, NSA_G * NSA_DH)
    wlen = cache_win_kv.shape[2]
    win_t = cache_win_kv[0].transpose(0, 2, 3, 4, 1).reshape(bsz, 2 * NSA_G * NSA_DH, wlen)
    o_n = _dec_attn(page_table, idx, cache2, q3, ocmp, knew, win_t, p[:, EV_GATE:EV_GATE + LANES], nsel)
    o_n = o_n.reshape(bsz, NSA_H * NSA_DH)
    x1 = _out_ffn([o_h, o_n], x2d, [pw["woe_h"], pw["woe_n"]], norm_ffn[0], pw["w13t"][0], pw["w2b"][0], None, bsz)

    rows = p[:, EV_CMP:EV_WIN].reshape(1, bsz, 1, 4, NSA_G, NSA_DH)
    win_new = p[:, EV_WIN:EV_GATE].reshape(1, bsz, 1, 2, NSA_G, NSA_DH)
    wk = jnp.concatenate([cache_win_kv[:1], win_new], axis=2)
    wn = min(NSA_WINDOW, npages * PAGE + 1)
    win = wk[:, :, wk.shape[2] - wn:]

    p2 = _norm_proj(x1, norm_mix[1], pw["wio"], bsz, ODD_W)
    o_g, gla_state = _gla_decode(p2, pw["wa2p"], gla_ba[0], gla_norm[0], state_gla[0])
    y = _out_ffn([o_g], x1, [pw["woo"]], norm_ffn[1], pw["w13t"][1], pw["w2b"][1], norm_final, bsz)
    return y.reshape(bsz, 1, d), rows, win, hg_state[None], gla_state[None]


def kernel(x_prompt, x_sample, cache_nsa_kv, cache_win_kv, state_hgrn, state_gla, page_table, w_in_even,
           hg_lb_logits, hg_norm, nsa_cmp_pos, nsa_cmp_w1, nsa_cmp_w2, w_out_even, w_in_odd, gla_wa2, gla_ba,
           gla_norm, w_out_odd, norm_mix, norm_ffn, norm_final, ffn_w13, ffn_w2):
    pw = _prep_weights(w_in_even, nsa_cmp_pos, nsa_cmp_w1, nsa_cmp_w2, w_out_even, w_in_odd, gla_wa2, w_out_odd,
                       ffn_w13, ffn_w2)
    y_p, kv_p, win_p, hg_p, gla_p = _forward_prompt(x_prompt, pw, hg_lb_logits, hg_norm, gla_ba, gla_norm,
                                                    norm_mix, norm_ffn, norm_final)
    y_s, kv_s, win_s, hg_s, gla_s = _forward_sample(x_sample, pw, cache_nsa_kv, cache_win_kv, state_hgrn,
                                                    state_gla, page_table, hg_lb_logits, hg_norm, gla_ba,
                                                    gla_norm, norm_mix, norm_ffn, norm_final)
    return (y_p, y_s, kv_p, kv_s, win_p, win_s, hg_p, hg_s, gla_p, gla_s)
```

```python
import functools

import jax
import jax.numpy as jnp
from jax import lax
from jax.experimental import pallas as pl
from jax.experimental.pallas import tpu as pltpu

F32 = jnp.float32
BF16 = jnp.bfloat16
I32 = jnp.int32

D_MODEL = 1024
HG_H, HG_DK, HG_DV = 4, 128, 128
NSA_H, NSA_DH, NSA_G, NSA_R = 8, 64, 2, 4
NSA_BLK = 64
NSA_TOPK = 16
NSA_WINDOW = 512
NSA_QB = 128
GLA_H, GLA_DK, GLA_DV = 4, 128, 256
GLA_RANK = 16
GLA_TAU = 16.0
D_FF = 2816
EPS = 1e-6
NEG = -1e30
TINY = 1e-30
PAGE = 128

PROJ_W = 3584
ODD_W = 3200
EV_Q, EV_F, EV_I, EV_OG = 0, 512, 1024, 1536
EV_NQ, EV_CMP, EV_SLC, EV_WIN, EV_GATE = 2048, 2560, 2816, 3072, 3328
OD_Q, OD_K, OD_V, OD_R, OD_A = 0, 512, 1024, 2048, 3072

LANES = 128
V7X_VMEM_LIMIT = 56 * 1024 * 1024
FF_TILE = 1408
CHUNK = 128
SUB = 16

NT_DIMS = (((1,), (1,)), ((), ()))
TN_DIMS = (((0,), (0,)), ((), ()))


def _cparams(sem):
    return pltpu.CompilerParams(dimension_semantics=sem, vmem_limit_bytes=V7X_VMEM_LIMIT)


def _rms(x, g):
    return x * lax.rsqrt(jnp.mean(x * x, axis=-1, keepdims=True) + EPS) * g


def _sigmoid(x):
    return 0.5 * jnp.tanh(0.5 * x) + 0.5


def _silu(x):
    return x * _sigmoid(x)


def _iota(shape, dim):
    return lax.broadcasted_iota(I32, shape, dim)


def _norm_proj_kernel(x_ref, g_ref, w_ref, o_ref, xn_ref):
    @pl.when(pl.program_id(1) == 0)
    def _():
        xn_ref[...] = _rms(x_ref[...], g_ref[...]).astype(BF16)

    o_ref[...] = jnp.dot(xn_ref[...], w_ref[...], preferred_element_type=F32)


def _norm_proj(x2d, gain, w, tm, tn):
    m, k = x2d.shape
    n = w.shape[1]
    return pl.pallas_call(
        _norm_proj_kernel,
        grid=(m // tm, n // tn),
        in_specs=[pl.BlockSpec((tm, k), lambda i, j: (i, 0)),
                  pl.BlockSpec((1, k), lambda i, j: (0, 0)),
                  pl.BlockSpec((k, tn), lambda i, j: (0, j))],
        out_specs=pl.BlockSpec((tm, tn), lambda i, j: (i, j)),
        out_shape=jax.ShapeDtypeStruct((m, n), F32),
        scratch_shapes=[pltpu.VMEM((tm, k), BF16)],
        compiler_params=_cparams(("parallel", "arbitrary")),
        name="norm_proj",
    )(x2d, gain.reshape(1, k), w)


PM_W = 2304
PM_CMP = 2048
T_Q, T_ROWS, T_WIN, T_GATE, T_END = 0, 512, 1024, 1280, 1312


def _norm_proj_t_kernel(x_ref, g_ref, w_ref, wt_ref, o_ref, q_ref, rows_ref, win_ref, gate_ref):
    xn = _rms(x_ref[...], g_ref[...]).astype(BF16)
    o_ref[...] = jnp.dot(xn, w_ref[...], preferred_element_type=F32)
    t = lax.dot_general(wt_ref[...], xn, NT_DIMS, preferred_element_type=F32)
    q_ref[0] = t[T_Q:T_ROWS]
    rows_ref[0] = t[T_ROWS:T_WIN]
    win_ref[0] = t[T_WIN:T_GATE]
    gate_ref[0] = t[T_GATE:T_END]


def _norm_proj_t(x2d, gain, w, wt, bsz, seq, tm):
    k = x2d.shape[1]
    nt = seq // tm
    tr = lambda rows: pl.BlockSpec((1, rows, tm), lambda b, t: (b, 0, t))
    sizes = (T_ROWS - T_Q, T_WIN - T_ROWS, T_GATE - T_WIN, T_END - T_GATE)
    return pl.pallas_call(
        _norm_proj_t_kernel,
        grid=(bsz, nt),
        in_specs=[pl.BlockSpec((tm, k), lambda b, t: (b * nt + t, 0)),
                  pl.BlockSpec((1, k), lambda b, t: (0, 0)),
                  pl.BlockSpec(w.shape, lambda b, t: (0, 0)),
                  pl.BlockSpec(wt.shape, lambda b, t: (0, 0))],
        out_specs=[pl.BlockSpec((tm, PM_W), lambda b, t: (b * nt + t, 0))] + [tr(r) for r in sizes],
        out_shape=[jax.ShapeDtypeStruct((bsz * seq, PM_W), F32)]
        + [jax.ShapeDtypeStruct((bsz, r, seq), F32) for r in sizes],
        compiler_params=_cparams(("parallel", "arbitrary")),
        name="norm_proj_t",
    )(x2d, gain.reshape(1, k), w, wt)


def _out_ffn_kernel(*refs, n_mix, final_norm):
    mix_refs = refs[:n_mix]
    res_ref = refs[n_mix]
    wo_refs = refs[n_mix + 1:2 * n_mix + 1]
    g_ref, w13_ref, w2_ref = refs[2 * n_mix + 1:2 * n_mix + 4]
    pos = 2 * n_mix + 4
    gf_ref = refs[pos] if final_norm else None
    pos += 1 if final_norm else 0
    o_ref, x1_ref, h_ref, acc_ref = refs[pos:pos + 4]
    j = pl.program_id(1)

    @pl.when(j == 0)
    def _():
        x1 = res_ref[...]
        for m_ref, w_ref in zip(mix_refs, wo_refs):
            x1 = x1 + jnp.dot(m_ref[...].astype(BF16), w_ref[...], preferred_element_type=F32)
        x1_ref[...] = x1
        h_ref[...] = _rms(x1, g_ref[...]).astype(BF16)
        acc_ref[...] = jnp.zeros_like(acc_ref)

    gu = jnp.dot(h_ref[...], w13_ref[...], preferred_element_type=F32)
    act = _silu(gu[:, :FF_TILE]) * gu[:, FF_TILE:]
    acc_ref[...] += jnp.dot(act.astype(BF16), w2_ref[...], preferred_element_type=F32)

    @pl.when(j == pl.num_programs(1) - 1)
    def _():
        y = x1_ref[...] + acc_ref[...]
        if final_norm:
            y = _rms(y, gf_ref[...])
        o_ref[...] = y


def _out_ffn(mixes, res, wos, g_ffn, w13t, w2, g_final, tm):
    m, d = res.shape
    n_mix = len(mixes)
    nj = D_FF // FF_TILE
    final_norm = g_final is not None
    in_specs = [pl.BlockSpec((tm, mx.shape[1]), lambda i, j: (i, 0)) for mx in mixes]
    in_specs.append(pl.BlockSpec((tm, d), lambda i, j: (i, 0)))
    in_specs += [pl.BlockSpec(w.shape, lambda i, j: (0, 0)) for w in wos]
    in_specs += [pl.BlockSpec((1, d), lambda i, j: (0, 0)),
                 pl.BlockSpec((d, 2 * FF_TILE), lambda i, j: (0, j)),
                 pl.BlockSpec((FF_TILE, d), lambda i, j: (j, 0))]
    args = list(mixes) + [res] + list(wos) + [g_ffn.reshape(1, d), w13t, w2]
    if final_norm:
        in_specs.append(pl.BlockSpec((1, d), lambda i, j: (0, 0)))
        args.append(g_final.reshape(1, d))
    return pl.pallas_call(
        functools.partial(_out_ffn_kernel, n_mix=n_mix, final_norm=final_norm),
        grid=(m // tm, nj),
        in_specs=in_specs,
        out_specs=pl.BlockSpec((tm, d), lambda i, j: (i, 0)),
        out_shape=jax.ShapeDtypeStruct((m, d), F32),
        scratch_shapes=[pltpu.VMEM((tm, d), F32), pltpu.VMEM((tm, d), BF16), pltpu.VMEM((tm, d), F32)],
        compiler_params=_cparams(("parallel", "arbitrary")),
        name="out_ffn",
    )(*args)


def _gla_chunk(q, k, v, g, st):
    c = q.shape[0]
    tri = (_iota((c, c), 0) >= _iota((c, c), 1)).astype(BF16)
    g_hi = g.astype(BF16)
    g_r1 = g - g_hi.astype(F32)
    g_mid = g_r1.astype(BF16)
    g_lo = (g_r1 - g_mid.astype(F32)).astype(BF16)
    b = (jnp.dot(tri, g_hi, preferred_element_type=F32) + jnp.dot(tri, g_mid, preferred_element_type=F32)
         + jnp.dot(tri, g_lo, preferred_element_type=F32))
    o = lax.dot_general((q * jnp.exp(b)).astype(BF16), st.astype(BF16), NT_DIMS, preferred_element_type=F32)
    lane = _iota((SUB, c), 1)
    row = _iota((SUB, c), 0)
    hs = SUB // 2
    rows = []
    for blk in range(c // SUB):
        lo = blk * SUB
        b_i, q_i, k_i = b[lo:lo + SUB], q[lo:lo + SUB], k[lo:lo + SUB]
        a_top = jnp.zeros((hs, c), F32)
        a_bot = jnp.zeros((hs, c), F32)
        for s in range(SUB):
            if s < hs:
                e = jnp.exp(b_i - b_i[s:s + 1])
                a = jnp.sum(q_i * e * k_i[s:s + 1], axis=1, keepdims=True)
                a_top = jnp.where(lane[:hs] == lo + s, a[:hs], a_top)
                a_bot = jnp.where(lane[:hs] == lo + s, a[hs:], a_bot)
            else:
                e = jnp.exp(b_i[hs:] - b_i[s:s + 1])
                a = jnp.sum(q_i[hs:] * e * k_i[s:s + 1], axis=1, keepdims=True)
                a_bot = jnp.where(lane[:hs] == lo + s, a, a_bot)
        a_blk = jnp.where(lane <= lo + row, jnp.concatenate([a_top, a_bot], axis=0), 0.0)
        if blk > 0:
            r = b[lo - 1:lo]
            qt = q_i * jnp.exp(b_i - r)
            kt = k * jnp.exp(jnp.minimum(r - b, 0.0))
            a_off = lax.dot_general(qt.astype(BF16), kt.astype(BF16), NT_DIMS, preferred_element_type=F32)
            a_blk = jnp.where(lane < lo, a_off, a_blk)
        rows.append(a_blk)
    a_full = jnp.concatenate(rows, axis=0)
    o = o + jnp.dot(a_full.astype(BF16), v.astype(BF16), preferred_element_type=F32)
    b_last = b[c - 1:c]
    kd = k * jnp.exp(b_last - b)
    st_new = st * jnp.exp(b_last) + lax.dot_general(v.astype(BF16), kd.astype(BF16), TN_DIMS,
                                                    preferred_element_type=F32)
    return o, st_new


def _lower_bound(lbl_ref, col):
    l = lbl_ref[:, col:col + HG_DK]
    e = jnp.exp(l - jnp.max(l, axis=0, keepdims=True))
    return e[0:1] / jnp.sum(e, axis=0, keepdims=True)


def _hgrn_kernel(q_ref, f_ref, i_ref, og_ref, lbl_ref, gain_ref, o_ref, s_ref, st_ref, *, nch):
    t = pl.program_id(1)

    @pl.when(t == 0)
    def _():
        st_ref[...] = jnp.zeros_like(st_ref)

    def body(ci, carry):
        r0 = pl.multiple_of(ci * CHUNK, CHUNK)
        rs = pl.ds(r0, CHUNK)
        for h in range(HG_H):
            cs = slice(h * HG_DK, (h + 1) * HG_DK)
            lb = _lower_bound(lbl_ref, h * HG_DK)
            q = _silu(q_ref[rs, cs])
            f = lb + (1.0 - lb) * _sigmoid(f_ref[rs, cs])
            o, st_new = _gla_chunk(q, 1.0 - f, i_ref[rs, cs], jnp.log(f), st_ref[h])
            st_ref[h] = st_new
            o = _rms(o, gain_ref[...]) * _sigmoid(og_ref[rs, cs])
            o_ref[rs, cs] = o.astype(o_ref.dtype)
        return carry

    lax.fori_loop(0, nch, body, 0)

    @pl.when(t == pl.num_programs(1) - 1)
    def _():
        for h in range(HG_H):
            s_ref[0, h] = st_ref[h].T


def _hgrn_prompt(p2d, lb_logits, gain, bsz, seq, tb):
    nt = seq // tb
    hk = HG_H * HG_DK
    row = lambda b, t: b * nt + t
    col_spec = lambda c: pl.BlockSpec((tb, hk), lambda b, t, c=c: (row(b, t), c))
    return pl.pallas_call(
        functools.partial(_hgrn_kernel, nch=tb // CHUNK),
        grid=(bsz, nt),
        in_specs=[col_spec(EV_Q // hk), col_spec(EV_F // hk), col_spec(EV_I // hk), col_spec(EV_OG // hk),
                  pl.BlockSpec(lb_logits.shape, lambda b, t: (0, 0)),
                  pl.BlockSpec((1, HG_DV), lambda b, t: (0, 0))],
        out_specs=[pl.BlockSpec((tb, hk), lambda b, t: (row(b, t), 0)),
                   pl.BlockSpec((1, HG_H, HG_DK, HG_DV), lambda b, t: (b, 0, 0, 0))],
        out_shape=[jax.ShapeDtypeStruct((bsz * seq, hk), BF16),
                   jax.ShapeDtypeStruct((bsz, HG_H, HG_DK, HG_DV), F32)],
        scratch_shapes=[pltpu.VMEM((HG_H, HG_DV, HG_DK), F32)],
        compiler_params=_cparams(("parallel", "arbitrary")),
        name="hgrn_chunk",
    )(p2d, p2d, p2d, p2d, lb_logits, gain.reshape(1, HG_DV))


def _log_sigmoid(x):
    return jnp.minimum(x, 0.0) - jnp.log(1.0 + jnp.exp(-jnp.abs(x)))


def _gla_kernel(q_ref, k_ref, v_ref, r_ref, a_ref, wa_ref, ba_ref, gain_ref, o_ref, s_ref, st_ref, *, nch):
    t = pl.program_id(1)

    @pl.when(t == 0)
    def _():
        st_ref[...] = jnp.zeros_like(st_ref)

    def body(ci, carry):
        r0 = pl.multiple_of(ci * CHUNK, CHUNK)
        rs = pl.ds(r0, CHUNK)
        gate = jnp.dot(a_ref[rs, :].astype(BF16), wa_ref[...], preferred_element_type=F32) + ba_ref[...]
        logf = _log_sigmoid(gate) * (1.0 / GLA_TAU)
        for h in range(GLA_H):
            ks = slice(h * GLA_DK, (h + 1) * GLA_DK)
            vs = slice(h * GLA_DV, (h + 1) * GLA_DV)
            q = q_ref[rs, ks] * (GLA_DK ** -0.5)
            o, st_new = _gla_chunk(q, k_ref[rs, ks], v_ref[rs, vs], logf[:, ks], st_ref[h])
            st_ref[h] = st_new
            o = _rms(o, gain_ref[...]) * _silu(r_ref[rs, vs])
            o_ref[rs, vs] = o.astype(o_ref.dtype)
        return carry

    lax.fori_loop(0, nch, body, 0)

    @pl.when(t == pl.num_programs(1) - 1)
    def _():
        for h in range(GLA_H):
            s_ref[0, h] = st_ref[h].T


def _gla_prompt(p2d, wa2p, ba, gain, bsz, seq, tb):
    nt = seq // tb
    hk, hv = GLA_H * GLA_DK, GLA_H * GLA_DV
    row = lambda b, t: b * nt + t
    return pl.pallas_call(
        functools.partial(_gla_kernel, nch=tb // CHUNK),
        grid=(bsz, nt),
        in_specs=[pl.BlockSpec((tb, hk), lambda b, t: (row(b, t), OD_Q // hk)),
                  pl.BlockSpec((tb, hk), lambda b, t: (row(b, t), OD_K // hk)),
                  pl.BlockSpec((tb, hv), lambda b, t: (row(b, t), OD_V // hv)),
                  pl.BlockSpec((tb, hv), lambda b, t: (row(b, t), OD_R // hv)),
                  pl.BlockSpec((tb, LANES), lambda b, t: (row(b, t), OD_A // LANES)),
                  pl.BlockSpec(wa2p.shape, lambda b, t: (0, 0)),
                  pl.BlockSpec((1, hk), lambda b, t: (0, 0)),
                  pl.BlockSpec((1, GLA_DV), lambda b, t: (0, 0))],
        out_specs=[pl.BlockSpec((tb, hv), lambda b, t: (row(b, t), 0)),
                   pl.BlockSpec((1, GLA_H, GLA_DK, GLA_DV), lambda b, t: (b, 0, 0, 0))],
        out_shape=[jax.ShapeDtypeStruct((bsz * seq, hv), BF16),
                   jax.ShapeDtypeStruct((bsz, GLA_H, GLA_DK, GLA_DV), F32)],
        scratch_shapes=[pltpu.VMEM((GLA_H, GLA_DV, GLA_DK), F32)],
        compiler_params=_cparams(("parallel", "arbitrary")),
        name="gla_chunk",
    )(p2d, p2d, p2d, p2d, p2d, wa2p, ba.reshape(1, hk), gain.reshape(1, GLA_DV))


def _to_columns(x):
    bsz = x.shape[0]
    if bsz < LANES:
        x = jnp.concatenate([x, jnp.zeros((LANES - bsz, x.shape[1]), x.dtype)], axis=0)
    return x.T


def _decode_update(q, k, v, g, s_ref, so_ref, o_scr):
    bsz = q.shape[0]
    qt, kt, et = _to_columns(q), _to_columns(k), _to_columns(jnp.exp(g))
    for b in range(bsz):
        s_new = et[:, b:b + 1] * s_ref[b, 0] + kt[:, b:b + 1] * v[b:b + 1, :]
        so_ref[b, 0] = s_new
        o_scr[b:b + 1, :] = jnp.sum(qt[:, b:b + 1] * s_new, axis=0, keepdims=True)


def _hgrn_decode_kernel(q_ref, f_ref, i_ref, og_ref, lbl_ref, gain_ref, s_ref, o_ref, so_ref, o_scr):
    l = lbl_ref[0]
    e = jnp.exp(l - jnp.max(l, axis=0, keepdims=True))
    lb = e[0:1] / jnp.sum(e, axis=0, keepdims=True)
    f = lb + (1.0 - lb) * _sigmoid(f_ref[...])
    _decode_update(_silu(q_ref[...]), 1.0 - f, i_ref[...], jnp.log(f), s_ref, so_ref, o_scr)
    o_ref[...] = _rms(o_scr[...], gain_ref[...]) * _sigmoid(og_ref[...])


def _hgrn_decode(p2d, lb_logits, gain, state):
    bsz = p2d.shape[0]
    col = lambda c: pl.BlockSpec((bsz, HG_DK), lambda h, c=c: (0, c + h))
    lbl3 = lb_logits.reshape(lb_logits.shape[0], HG_H, HG_DK).transpose(1, 0, 2)
    st_spec = pl.BlockSpec((bsz, 1, HG_DK, HG_DV), lambda h: (0, h, 0, 0))
    return pl.pallas_call(
        _hgrn_decode_kernel,
        grid=(HG_H,),
        in_specs=[col(EV_Q // HG_DK), col(EV_F // HG_DK), col(EV_I // HG_DK), col(EV_OG // HG_DK),
                  pl.BlockSpec((1,) + lbl3.shape[1:], lambda h: (h, 0, 0)),
                  pl.BlockSpec((1, HG_DV), lambda h: (0, 0)),
                  st_spec],
        out_specs=[pl.BlockSpec((bsz, HG_DV), lambda h: (0, h)), st_spec],
        out_shape=[jax.ShapeDtypeStruct((bsz, HG_H * HG_DV), F32),
                   jax.ShapeDtypeStruct(state.shape, F32)],
        scratch_shapes=[pltpu.VMEM((bsz, HG_DV), F32)],
        compiler_params=_cparams(("arbitrary",)),
        name="hgrn_decode",
    )(p2d, p2d, p2d, p2d, lbl3, gain.reshape(1, HG_DV), state)


def _gla_decode_kernel(q_ref, k_ref, v_ref, r_ref, a_ref, wa_ref, ba_ref, gain_ref, s_ref, o_ref, so_ref, o_scr):
    gate = jnp.dot(a_ref[...].astype(BF16), wa_ref[...], preferred_element_type=F32) + ba_ref[...]
    logf = _log_sigmoid(gate) * (1.0 / GLA_TAU)
    _decode_update(q_ref[...] * (GLA_DK ** -0.5), k_ref[...], v_ref[...], logf, s_ref, so_ref, o_scr)
    o_ref[...] = _rms(o_scr[...], gain_ref[...]) * _silu(r_ref[...])


def _gla_decode(p2d, wa2p, ba, gain, state):
    bsz = p2d.shape[0]
    hk = GLA_H * GLA_DK
    st_spec = pl.BlockSpec((bsz, 1, GLA_DK, GLA_DV), lambda h: (0, h, 0, 0))
    return pl.pallas_call(
        _gla_decode_kernel,
        grid=(GLA_H,),
        in_specs=[pl.BlockSpec((bsz, GLA_DK), lambda h: (0, OD_Q // GLA_DK + h)),
                  pl.BlockSpec((bsz, GLA_DK), lambda h: (0, OD_K // GLA_DK + h)),
                  pl.BlockSpec((bsz, GLA_DV), lambda h: (0, OD_V // GLA_DV + h)),
                  pl.BlockSpec((bsz, GLA_DV), lambda h: (0, OD_R // GLA_DV + h)),
                  pl.BlockSpec((bsz, LANES), lambda h: (0, OD_A // LANES)),
                  pl.BlockSpec((LANES, GLA_DK), lambda h: (0, h)),
                  pl.BlockSpec((1, GLA_DK), lambda h: (0, h)),
                  pl.BlockSpec((1, GLA_DV), lambda h: (0, 0)),
                  st_spec],
        out_specs=[pl.BlockSpec((bsz, GLA_DV), lambda h: (0, h)), st_spec],
        out_shape=[jax.ShapeDtypeStruct((bsz, GLA_H * GLA_DV), F32),
                   jax.ShapeDtypeStruct(state.shape, F32)],
        scratch_shapes=[pltpu.VMEM((bsz, GLA_DV), F32)],
        compiler_params=_cparams(("arbitrary",)),
        name="gla_decode",
    )(p2d, p2d, p2d, p2d, p2d, wa2p, ba.reshape(1, hk), gain.reshape(1, GLA_DV), state)


def _compress_kernel(xa_ref, xb_ref, pe_ref, w1_ref, w2_ref, o_ref, acc_ref):
    nb = o_ref.shape[0]
    acc_ref[...] = jnp.zeros_like(acc_ref)
    for l in range(NSA_BLK):
        rows = pl.ds(l, nb, stride=NSA_BLK)
        x = jnp.concatenate([xa_ref[rows, :], xb_ref[rows, :]], axis=1) + pe_ref[l:l + 1, :]
        acc_ref[...] += jnp.dot(x.astype(BF16), w1_ref[l], preferred_element_type=F32)
    o_ref[...] = jnp.dot(_silu(acc_ref[...]).astype(BF16), w2_ref[...], preferred_element_type=F32)


def _compress_prompt(p2d, pe4, w1bd, w2bd):
    n = p2d.shape[0]
    rc = min(8192, n)
    cw = 4 * NSA_DH
    c0 = PM_CMP // LANES
    return pl.pallas_call(
        _compress_kernel,
        grid=(n // rc,),
        in_specs=[pl.BlockSpec((rc, LANES), lambda i: (i, c0)),
                  pl.BlockSpec((rc, LANES), lambda i: (i, c0 + 1)),
                  pl.BlockSpec(pe4.shape, lambda i: (0, 0)),
                  pl.BlockSpec(w1bd.shape, lambda i: (0, 0, 0)),
                  pl.BlockSpec((cw, cw), lambda i: (0, 0))],
        out_specs=pl.BlockSpec((rc // NSA_BLK, cw), lambda i: (i, 0)),
        out_shape=jax.ShapeDtypeStruct((n // NSA_BLK, cw), F32),
        scratch_shapes=[pltpu.VMEM((rc // NSA_BLK, cw), F32)],
        compiler_params=_cparams(("arbitrary",)),
        name="nsa_compress",
    )(p2d, p2d, pe4, w1bd, w2bd)


AQB = 256
AQB_SHIFT = 8
AUG = 128
ONES_ROWS = 16


def _slopes_lane(g, lanes):
    r = _iota((1, lanes), 1) >> AQB_SHIFT
    out = jnp.zeros((1, lanes), F32)
    for rr in range(NSA_R):
        out = jnp.where(r == rr, 2.0 ** (-(g * NSA_R + rr + 1)), out)
    return out


def _build_key_features(src_ref, row0, dst_ref, g, seq, with_onehot):
    r = _iota((NSA_DH, LANES), 0)
    for cb in range(seq // LANES):
        kpos = cb * LANES + _iota((NSA_DH, LANES), 1)
        feat = jnp.where(r == 32, (kpos >> 6).astype(F32),
                         jnp.where(r == 33, (kpos & 63).astype(F32),
                                   jnp.where((r == 34) | (r == 35), 1.0, 0.0)))
        if with_onehot:
            feat = jnp.where((kpos >> 6) == r, 1.0, feat)
        kt = src_ref[0, row0:row0 + NSA_DH, cb * LANES:(cb + 1) * LANES]
        dst_ref[g, cb * LANES:(cb + 1) * LANES, :] = jnp.concatenate([kt, feat], axis=0).T.astype(BF16)


def _attn_tiles(k_ref, k0, v_ref, v_row0, qa_ref, mask, m_ref, acc_ref, stream0, nkeys=AQB):
    groups = range(NSA_G)
    s = [jnp.dot(k_ref[g, pl.ds(k0, nkeys), :], qa_ref[g], preferred_element_type=F32) for g in groups]
    if mask is not None:
        s = [jnp.where(mask, sg, NEG) for sg in s]
    m_old = [m_ref[stream0 + g] for g in groups]
    m_new = [jnp.maximum(m_old[g], jnp.max(s[g], axis=0, keepdims=True)) for g in groups]
    p = [jnp.exp(s[g] - m_new[g]).astype(BF16) for g in groups]
    ones = jnp.ones((ONES_ROWS, nkeys), BF16)
    for g in groups:
        vt = v_ref[0, v_row0 + g * NSA_DH:v_row0 + (g + 1) * NSA_DH, pl.ds(k0, nkeys)].astype(BF16)
        pv = jnp.dot(jnp.concatenate([vt, ones], axis=0), p[g], preferred_element_type=F32)
        acc_ref[stream0 + g] = jnp.exp(m_old[g] - m_new[g]) * acc_ref[stream0 + g] + pv
        m_ref[stream0 + g] = m_new[g]


def _nsa_attn_kernel(qt_ref, rows_ref, win_ref, gt_ref, kc_ref, vct_ref, o_ref, ks_s, kw_s, qa_s, m_s, acc_s,
                     *, nblk):
    qb = pl.program_id(1)
    seq = rows_ref.shape[2]
    q0 = qb * AQB
    nq = NSA_R * AQB
    half = NSA_G * NSA_DH

    @pl.when(qb == 0)
    def _():
        for g in range(NSA_G):
            _build_key_features(rows_ref, 2 * half + g * NSA_DH, ks_s, g, seq, True)
            _build_key_features(win_ref, g * NSA_DH, kw_s, g, seq, False)

    pq = q0 + (_iota((1, nq), 1) & (AQB - 1))
    gates = _sigmoid(gt_ref[0])
    ksel = min(NSA_TOPK, nblk)
    scale = NSA_DH ** -0.5
    key_i = _iota((AQB, nq), 0)
    qry_t = _iota((AQB, nq), 1) & (AQB - 1)
    causal = key_i <= qry_t
    beyond = key_i > qry_t
    pq_f = q0 + _iota((32, AQB), 1)
    feat_row = _iota((32, AQB), 0)

    m_s[...] = jnp.full(m_s.shape, NEG, F32)
    acc_s[...] = jnp.zeros_like(acc_s)

    def result(i):
        return acc_s[i, 0:NSA_DH, :] / jnp.maximum(acc_s[i, NSA_DH:NSA_DH + 1, :], TINY)

    o_cmp = []
    for g in range(NSA_G):
        slope = _slopes_lane(g, nq)

        qt_g = [qt_ref[0, (g * NSA_R + r) * NSA_DH:(g * NSA_R + r + 1) * NSA_DH, :] * scale for r in range(NSA_R)]
        qs_t = jnp.concatenate(qt_g, axis=1).astype(BF16)
        sc = jnp.dot(kc_ref[0, g], qs_t, preferred_element_type=F32)
        dist_c = pq - (_iota((nblk, nq), 0) * NSA_BLK + NSA_BLK - 1)
        mask_c = dist_c >= 0
        sc = jnp.where(mask_c, sc - slope * dist_c.astype(F32), NEG)
        pc = jnp.where(mask_c, jnp.exp(sc - jnp.max(sc, axis=0, keepdims=True)), 0.0)
        pc = pc / jnp.maximum(jnp.sum(pc, axis=0, keepdims=True), TINY)
        o_cmp.append(jnp.dot(vct_ref[0, g], pc.astype(BF16), preferred_element_type=F32))
        imp = pc[:, 0:AQB]
        for r in range(1, NSA_R):
            imp = imp + pc[:, r * AQB:(r + 1) * AQB]

        n_io = _iota((nblk, AQB), 0)
        tpos = q0 + _iota((nblk, AQB), 1)
        forced = (n_io == (tpos >> 6)) | (n_io == 0)
        started = n_io * NSA_BLK <= tpos
        score = jnp.where(forced, jnp.inf, jnp.where(started, imp, -jnp.inf))
        rank = jnp.zeros((nblk, AQB), I32)
        for i in range(nblk):
            row = score[i:i + 1, :]
            rank = rank + ((row > score) | ((row == score) & (i < n_io))).astype(I32)
        sel_bias = jnp.where((rank < ksel) & started, 0.0, NEG)
        if nblk < 32:
            sel_bias = jnp.concatenate([sel_bias, jnp.zeros((32 - nblk, AQB), F32)], axis=0)

        cols = []
        for r in range(NSA_R):
            sl = 2.0 ** (-(g * NSA_R + r + 1))
            pos_feat = jnp.where(feat_row == 0, 64.0 * sl,
                                 jnp.where(feat_row == 1, sl,
                                           jnp.where(feat_row == 2, (-64.0 * sl) * (pq_f >> 6).astype(F32),
                                                     jnp.where(feat_row == 3, (-sl) * (pq_f & 63).astype(F32), 0.0))))
            cols.append(jnp.concatenate([qt_g[r], sel_bias, pos_feat], axis=0))
        qa_s[g] = jnp.concatenate(cols, axis=1).astype(BF16)

    def slc_tile(k0, mask, nkeys=AQB):
        _attn_tiles(ks_s, k0, rows_ref, 3 * half, qa_s, mask, m_s, acc_s, 0, nkeys)

    def win_tile(k0, mask):
        _attn_tiles(kw_s, k0, win_ref, half, qa_s, mask, m_s, acc_s, NSA_G)

    def slc_body(kt, c):
        slc_tile(pl.multiple_of(kt * 2 * AQB, 2 * AQB), None, 2 * AQB)
        return c

    lax.fori_loop(0, qb >> 1, slc_body, 0)

    @pl.when((qb & 1) == 1)
    def _():
        slc_tile(pl.multiple_of(q0 - AQB, AQB), None)

    @pl.when(qb >= 2)
    def _():
        win_tile(pl.multiple_of(q0 - 2 * AQB, AQB), beyond)

    @pl.when(qb >= 1)
    def _():
        win_tile(pl.multiple_of(q0 - AQB, AQB), None)

    slc_tile(pl.multiple_of(q0, AQB), causal)
    win_tile(pl.multiple_of(q0, AQB), causal)

    for g in range(NSA_G):
        def gate_row(j, g=g):
            return jnp.concatenate([gates[g * 12 + r * 3 + j:g * 12 + r * 3 + j + 1, :] for r in range(NSA_R)],
                                   axis=1)

        o_t = gate_row(0) * o_cmp[g] + gate_row(1) * result(g) + gate_row(2) * result(NSA_G + g)
        o_st = jnp.concatenate([o_t[:, r * AQB:(r + 1) * AQB] for r in range(NSA_R)], axis=0)
        o_ref[:, g * NSA_R * NSA_DH:(g + 1) * NSA_R * NSA_DH] = o_st.T.astype(o_ref.dtype)


def _nsa_attn_prompt(qt, rows_t, win_t, gate_t, kc, vct, bsz, seq):
    assert NSA_WINDOW == 2 * AQB and seq % AQB == 0 and seq // NSA_BLK <= 32
    nqb = seq // AQB
    nblk = seq // NSA_BLK
    qw = NSA_H * NSA_DH
    nq = NSA_R * AQB
    per_b = lambda a: pl.BlockSpec((1,) + a.shape[1:], lambda b, i: (b,) + (0,) * (a.ndim - 1))
    per_q = lambda a: pl.BlockSpec((1, a.shape[1], AQB), lambda b, i: (b, 0, i))
    return pl.pallas_call(
        functools.partial(_nsa_attn_kernel, nblk=nblk),
        grid=(bsz, nqb),
        in_specs=[per_q(qt), per_b(rows_t), per_b(win_t), per_q(gate_t), per_b(kc), per_b(vct)],
        out_specs=pl.BlockSpec((AQB, qw), lambda b, i: (b * nqb + i, 0)),
        out_shape=jax.ShapeDtypeStruct((bsz * seq, qw), BF16),
        scratch_shapes=[pltpu.VMEM((NSA_G, seq, AUG), BF16), pltpu.VMEM((NSA_G, seq, AUG), BF16),
                        pltpu.VMEM((NSA_G, AUG, nq), BF16),
                        pltpu.VMEM((2 * NSA_G, 1, nq), F32), pltpu.VMEM((2 * NSA_G, NSA_DH + ONES_ROWS, nq), F32)],
        compiler_params=_cparams(("parallel", "arbitrary")),
        name="nsa_attn",
    )(qt, rows_t, win_t, gate_t, kc, vct)


QROWS = 16


def _head_slopes(rows):
    r = _iota((rows, 1), 0)
    out = jnp.zeros((rows, 1), F32)
    for h in range(NSA_H):
        out = jnp.where(r == h, 2.0 ** (-(h + 1)), out)
    return out


def _softmax_rows(s, valid):
    s = jnp.where(valid, s, NEG)
    p = jnp.where(valid, jnp.exp(s - jnp.max(s, axis=1, keepdims=True)), 0.0)
    return p / jnp.maximum(jnp.sum(p, axis=1, keepdims=True), TINY)


def _padded_queries(q_ref):
    q = q_ref[0] * (NSA_DH ** -0.5)
    return jnp.concatenate([q, jnp.zeros((QROWS - NSA_H, NSA_DH), F32)], axis=0).astype(BF16)


PLANES = 4 * NSA_G
PAGE_ROWS = PLANES * NSA_DH
CMP_PLANES = 2 * NSA_G


def _dec_cmp_kernel(pt_ref, cache_ref, pe_ref, w1_ref, w2_ref, q_ref, o_ref, sel_ref, buf, acc_ref, cmp_ref, sem,
                    *, npages, nsel):
    b = pl.program_id(0)
    nb = npages * (PAGE // NSA_BLK)
    nrow = npages * CMP_PLANES
    crows = CMP_PLANES * NSA_DH

    def page_copy(pg):
        src = pl.multiple_of(pt_ref[b * npages + pg] * PAGE_ROWS, PAGE_ROWS)
        return pltpu.make_async_copy(cache_ref.at[pl.ds(src, crows), :],
                                     buf.at[pl.ds(pl.multiple_of(pg * crows, crows), crows), :], sem.at[0])

    def start(pg, c):
        page_copy(pg).start()
        return c

    def wait(pg, c):
        page_copy(pg).wait()
        return c

    lax.fori_loop(0, npages, start, 0)
    lax.fori_loop(0, npages, wait, 0)

    is_k = ((_iota((nrow, 1), 0) >> 1) & 1) == 0
    acc_ref[...] = jnp.zeros_like(acc_ref)
    for d in range(NSA_DH):
        x = buf[pl.ds(d, nrow, stride=NSA_DH), :]
        x = x + jnp.where(is_k, pe_ref[d, 0:1, :], pe_ref[d, 1:2, :])
        acc_ref[...] += jnp.dot(x.astype(BF16), w1_ref[d], preferred_element_type=F32)
    acc = acc_ref[...]
    h = _silu(jnp.where(is_k, acc[:, :LANES], acc[:, LANES:]))
    c2 = jnp.dot(h.astype(BF16), w2_ref[...], preferred_element_type=F32)
    cmp_ref[...] = jnp.where(is_k, c2[:, :LANES], c2[:, LANES:])

    pos = nb * NSA_BLK
    q16 = _padded_queries(q_ref)
    slope = _head_slopes(QROWS)
    row_grp = _iota((QROWS, 1), 0) >> 2
    lane = _iota((1, nb), 1)
    blk_of = jnp.where(lane < npages, 2 * lane, 2 * (lane - npages) + 1)
    dist = (pos - (blk_of * NSA_BLK + NSA_BLK - 1)).astype(F32)
    ri, ci = _iota((nb, nb), 0), _iota((nb, nb), 1)
    blk_r = jnp.where(ri < npages, 2 * ri, 2 * (ri - npages) + 1)
    blk_c = jnp.where(ci < npages, 2 * ci, 2 * (ci - npages) + 1)
    o_all = jnp.zeros((QROWS, NSA_DH), F32)
    for g in range(NSA_G):
        kc = cmp_ref[pl.ds(g, npages, stride=CMP_PLANES), :].astype(BF16)
        vc = cmp_ref[pl.ds(NSA_G + g, npages, stride=CMP_PLANES), :].astype(BF16)
        s = jnp.concatenate([lax.dot_general(q16, kc[:, j * NSA_DH:(j + 1) * NSA_DH], NT_DIMS,
                                             preferred_element_type=F32) for j in range(2)], axis=1)
        p = _softmax_rows(s - slope * dist, dist >= 0)
        pb = p.astype(BF16)
        o_g = sum(jnp.dot(pb[:, j * npages:(j + 1) * npages], vc[:, j * NSA_DH:(j + 1) * NSA_DH],
                          preferred_element_type=F32) for j in range(2))
        o_all = jnp.where(row_grp == g, o_g, o_all)
        imp = jnp.sum(jnp.where(row_grp == g, p, 0.0), axis=0, keepdims=True)
        score_row = jnp.where(blk_of == 0, jnp.inf, imp)
        score_col = jnp.sum(jnp.where(ri == ci, jnp.broadcast_to(score_row, (nb, nb)), 0.0),
                            axis=1, keepdims=True)
        beats = (score_col > score_row) | ((score_col == score_row) & (blk_r < blk_c))
        rank = jnp.sum(beats.astype(I32), axis=0, keepdims=True)
        sel_ref[0, g:g + 1, :] = (rank < nsel).astype(F32)
    o_ref[0] = o_all[:NSA_H]


def _dec_cmp(page_table, cache2, pe_dec, w1dec, w2dec, q3, nsel):
    bsz, npages = page_table.shape
    nb = npages * (PAGE // NSA_BLK)
    nrow = npages * CMP_PLANES
    grid_spec = pltpu.PrefetchScalarGridSpec(
        num_scalar_prefetch=1,
        grid=(bsz,),
        in_specs=[pl.BlockSpec(memory_space=pl.ANY),
                  pl.BlockSpec(pe_dec.shape, lambda b, pt: (0, 0, 0)),
                  pl.BlockSpec(w1dec.shape, lambda b, pt: (0, 0, 0)),
                  pl.BlockSpec(w2dec.shape, lambda b, pt: (0, 0)),
                  pl.BlockSpec((1, NSA_H, NSA_DH), lambda b, pt: (b, 0, 0))],
        out_specs=[pl.BlockSpec((1, NSA_H, NSA_DH), lambda b, pt: (b, 0, 0)),
                   pl.BlockSpec((1, NSA_G, nb), lambda b, pt: (b, 0, 0))],
        scratch_shapes=[pltpu.VMEM((nrow * NSA_DH, LANES), F32), pltpu.VMEM((nrow, 2 * LANES), F32),
                        pltpu.VMEM((nrow, LANES), F32), pltpu.SemaphoreType.DMA((1,))])
    return pl.pallas_call(
        functools.partial(_dec_cmp_kernel, npages=npages, nsel=nsel),
        grid_spec=grid_spec,
        out_shape=[jax.ShapeDtypeStruct((bsz, NSA_H, NSA_DH), F32),
                   jax.ShapeDtypeStruct((bsz, NSA_G, nb), F32)],
        compiler_params=_cparams(("arbitrary",)),
        name="nsa_dec_cmp",
    )(page_table.reshape(-1), cache2, pe_dec, w1dec, w2dec, q3)


def _attend_with_self(q16, slope, kt, vt, dist, valid, k_self, v_self):
    s = jnp.dot(q16, kt.astype(BF16), preferred_element_type=F32) - slope * dist
    s_self = jnp.sum(q16.astype(F32) * k_self.astype(BF16).astype(F32), axis=1, keepdims=True)
    m = jnp.maximum(jnp.max(jnp.where(valid, s, NEG), axis=1, keepdims=True), s_self)
    p = jnp.where(valid, jnp.exp(s - m), 0.0)
    p_self = jnp.exp(s_self - m)
    num = lax.dot_general(p.astype(BF16), vt.astype(BF16), NT_DIMS, preferred_element_type=F32) + p_self * v_self
    return num / jnp.maximum(jnp.sum(p, axis=1, keepdims=True) + p_self, TINY)


def _dec_attn_kernel(pt_ref, idx_ref, cache_ref, q_ref, oc_ref, kn_ref, win_ref, gt_ref, o_ref, kbuf, vbuf, sem,
                     *, npages, nsel, wlen):
    b = pl.program_id(0)
    pos = npages * PAGE

    copies = []
    for g in range(NSA_G):
        for j in range(nsel):
            blk = idx_ref[(b * NSA_G + g) * NSA_TOPK + j]
            base = pt_ref[b * npages + (blk >> 1)] * PAGE_ROWS
            dst = pl.ds(j * PAGE, PAGE)
            k_rows = pl.ds(pl.multiple_of(base + (2 * NSA_G + g) * NSA_DH, NSA_DH), NSA_DH)
            v_rows = pl.ds(pl.multiple_of(base + (3 * NSA_G + g) * NSA_DH, NSA_DH), NSA_DH)
            copies.append(pltpu.make_async_copy(cache_ref.at[k_rows, :], kbuf.at[g, :, dst], sem.at[0]))
            copies.append(pltpu.make_async_copy(cache_ref.at[v_rows, :], vbuf.at[g, :, dst], sem.at[0]))
    for cp in copies:
        cp.start()
    for cp in copies:
        cp.wait()

    q16 = _padded_queries(q_ref)
    slope = _head_slopes(QROWS)
    row_grp = _iota((QROWS, 1), 0) >> 2
    kn = kn_ref[0]
    lane = _iota((1, nsel * PAGE), 1)
    within = lane & (PAGE - 1)
    wdist = wlen - _iota((1, wlen), 1)
    o_slc = jnp.zeros((QROWS, NSA_DH), F32)
    o_win = jnp.zeros((QROWS, NSA_DH), F32)
    for g in range(NSA_G):
        gs = slice(g * NSA_DH, (g + 1) * NSA_DH)
        blk_lane = jnp.zeros((1, nsel * PAGE), I32)
        for j in range(nsel):
            blk_lane = jnp.where((lane >> 7) == j, idx_ref[(b * NSA_G + g) * NSA_TOPK + j], blk_lane)
        valid = (within >> 6) == (blk_lane & 1)
        kpos = (blk_lane >> 1) * PAGE + within
        o_g = _attend_with_self(q16, slope, kbuf[g], vbuf[g], (pos - kpos).astype(F32), valid,
                                kn[0:1, gs], kn[1:2, gs])
        o_slc = jnp.where(row_grp == g, o_g, o_slc)
        o_g = _attend_with_self(q16, slope, win_ref[0, g * NSA_DH:(g + 1) * NSA_DH, :],
                                win_ref[0, (NSA_G + g) * NSA_DH:(NSA_G + g + 1) * NSA_DH, :],
                                wdist.astype(F32), wdist < NSA_WINDOW, kn[2:3, gs], kn[3:4, gs])
        o_win = jnp.where(row_grp == g, o_g, o_win)

    gates = _sigmoid(gt_ref[pl.ds(b, 1), :])
    r_io = _iota((QROWS, LANES), 0)
    l_io = _iota((QROWS, LANES), 1)

    def gate_col(j):
        return jnp.sum(jnp.where(l_io == 3 * r_io + j, gates, 0.0), axis=1, keepdims=True)

    o_cmp = jnp.concatenate([oc_ref[0], jnp.zeros((QROWS - NSA_H, NSA_DH), F32)], axis=0)
    o = gate_col(0) * o_cmp + gate_col(1) * o_slc + gate_col(2) * o_win
    o_ref[0] = o[:NSA_H]


def _dec_attn(page_table, idx, cache2, q3, ocmp, knew, win_t, gates, nsel):
    bsz, npages = page_table.shape
    wlen = win_t.shape[2]
    hd = pl.BlockSpec((1, NSA_H, NSA_DH), lambda b, pt, ix: (b, 0, 0))
    grid_spec = pltpu.PrefetchScalarGridSpec(
        num_scalar_prefetch=2,
        grid=(bsz,),
        in_specs=[pl.BlockSpec(memory_space=pl.ANY), hd, hd,
                  pl.BlockSpec((1,) + knew.shape[1:], lambda b, pt, ix: (b, 0, 0)),
                  pl.BlockSpec((1,) + win_t.shape[1:], lambda b, pt, ix: (b, 0, 0)),
                  pl.BlockSpec(gates.shape, lambda b, pt, ix: (0, 0))],
        out_specs=hd,
        scratch_shapes=[pltpu.VMEM((NSA_G, NSA_DH, nsel * PAGE), F32), pltpu.VMEM((NSA_G, NSA_DH, nsel * PAGE), F32),
                        pltpu.SemaphoreType.DMA((1,))])
    return pl.pallas_call(
        functools.partial(_dec_attn_kernel, npages=npages, nsel=nsel, wlen=wlen),
        grid_spec=grid_spec,
        out_shape=jax.ShapeDtypeStruct((bsz, NSA_H, NSA_DH), F32),
        compiler_params=_cparams(("arbitrary",)),
        name="nsa_dec_attn",
    )(page_table.reshape(-1), idx.reshape(-1), cache2, q3, ocmp, knew, win_t, gates)


def _prep_weights(w_in_even, nsa_cmp_pos, nsa_cmp_w1, nsa_cmp_w2, w_out_even, w_in_odd, gla_wa2, w_out_odd,
                  ffn_w13, ffn_w2):
    d = D_MODEL
    wie = jnp.pad(w_in_even[0], ((0, 0), (0, PROJ_W - w_in_even.shape[2]))).astype(BF16)
    a0 = 2 * GLA_H * GLA_DK + GLA_H * GLA_DV
    wo = w_in_odd[0]
    wio = jnp.concatenate([wo[:, :a0], wo[:, a0 + GLA_RANK:], wo[:, a0:a0 + GLA_RANK]], axis=1)
    wio = jnp.pad(wio, ((0, 0), (0, ODD_W - wio.shape[1]))).astype(BF16)
    wa2p = jnp.pad(gla_wa2[0], ((0, LANES - GLA_RANK), (0, 0))).astype(BF16)
    hv = HG_H * HG_DV
    woe_h, woe_n = w_out_even[0, :hv].astype(BF16), w_out_even[0, hv:].astype(BF16)
    woo = w_out_odd[0].astype(BF16)
    nj = D_FF // FF_TILE
    w13t, w2b = [], []
    for l in range(ffn_w13.shape[0]):
        gate = ffn_w13[l, :, :D_FF].reshape(d, nj, FF_TILE)
        up = ffn_w13[l, :, D_FF:].reshape(d, nj, FF_TILE)
        w13t.append(jnp.concatenate([gate, up], axis=2).reshape(d, 2 * D_FF).astype(BF16))
        w2b.append(ffn_w2[l].astype(BF16))
    w1 = nsa_cmp_w1[0].reshape(2, NSA_BLK, NSA_DH, NSA_DH)
    w2 = nsa_cmp_w2[0]
    cw = 4 * NSA_DH
    dh = NSA_DH
    w1bd = jnp.concatenate([jnp.pad(w1[i // NSA_G], ((0, 0), (0, 0), (i * dh, cw - (i + 1) * dh)))
                            for i in range(2 * NSA_G)], axis=1)
    w2bd = jnp.concatenate([jnp.pad(w2[i // NSA_G], ((0, 0), (i * dh, cw - (i + 1) * dh)))
                            for i in range(2 * NSA_G)], axis=0)
    pe = nsa_cmp_pos[0]
    pe4 = jnp.concatenate([pe[0], pe[0], pe[1], pe[1]], axis=1)
    we = w_in_even[0]
    hgw = 2 * HG_H * HG_DK + 2 * HG_H * HG_DV
    nq0 = hgw + NSA_H * NSA_DH
    w_main = jnp.concatenate([we[:, :hgw], we[:, nq0:nq0 + cw]], axis=1).astype(BF16)
    w_tr = jnp.pad(we[:, hgw:].T, ((0, T_END - (we.shape[1] - hgw)), (0, 0))).astype(BF16)
    w1t = w1.transpose(0, 2, 1, 3)
    z1 = jnp.zeros_like(w1t[0])
    w1dec = jnp.concatenate([jnp.concatenate([w1t[0], z1, w1t[1], z1], axis=2),
                             jnp.concatenate([z1, w1t[0], z1, w1t[1]], axis=2)], axis=1)
    z2 = jnp.zeros_like(w2[0])
    w2dec = jnp.concatenate([jnp.concatenate([w2[0], z2, w2[1], z2], axis=1),
                             jnp.concatenate([z2, w2[0], z2, w2[1]], axis=1)], axis=0)
    pe_dec = jnp.tile(pe.transpose(2, 0, 1), (1, 1, 2))
    return dict(wie=wie, wio=wio, wa2p=wa2p, woe_h=woe_h, woe_n=woe_n, woo=woo, w13t=w13t, w2b=w2b,
                w1bd=w1bd.astype(BF16), w2bd=w2bd.astype(BF16), pe4=pe4, w_main=w_main, w_tr=w_tr,
                w1dec=w1dec.astype(BF16), w2dec=w2dec.astype(BF16), pe_dec=pe_dec)


def _forward_prompt(x, pw, hg_lb_logits, hg_norm, gla_ba, gla_norm, norm_mix, norm_ffn, norm_final):
    bsz, seq, d = x.shape
    n = bsz * seq
    x2d = x.reshape(n, d)
    tm = 512 if seq % 512 == 0 else 256
    tmf = next(t for t in (512, 256) if n % t == 0)
    tb = tm

    p, qt, rows_t, win_t, gate_t = _norm_proj_t(x2d, norm_mix[0], pw["w_main"], pw["w_tr"], bsz, seq, tm)
    o_h, hg_state = _hgrn_prompt(p, hg_lb_logits, hg_norm[0], bsz, seq, tb)
    cmp = _compress_prompt(p, pw["pe4"], pw["w1bd"], pw["w2bd"])
    nblk = seq // NSA_BLK
    cmp4 = cmp.reshape(bsz, nblk, 2 * NSA_G, NSA_DH).transpose(0, 2, 1, 3)
    kc = cmp4[:, :NSA_G].astype(BF16)
    vct = cmp4[:, NSA_G:].transpose(0, 1, 3, 2).astype(BF16)
    o_n = _nsa_attn_prompt(qt, rows_t, win_t, gate_t, kc, vct, bsz, seq)
    x1 = _out_ffn([o_h, o_n], x2d, [pw["woe_h"], pw["woe_n"]], norm_ffn[0], pw["w13t"][0], pw["w2b"][0], None, tmf)

    rows = rows_t.reshape(bsz, 4, NSA_G, NSA_DH, seq).transpose(0, 4, 1, 2, 3)[None]
    wn = min(NSA_WINDOW, seq)
    win = win_t[:, :, seq - wn:].reshape(bsz, 2, NSA_G, NSA_DH, wn).transpose(0, 4, 1, 2, 3)[None]

    p2 = _norm_proj(x1, norm_mix[1], pw["wio"], tm, ODD_W)
    o_g, gla_state = _gla_prompt(p2, pw["wa2p"], gla_ba[0], gla_norm[0], bsz, seq, tb)
    y = _out_ffn([o_g], x1, [pw["woo"]], norm_ffn[1], pw["w13t"][1], pw["w2b"][1], norm_final, tmf)
    return y.reshape(bsz, seq, d), rows, win, hg_state[None], gla_state[None]


def _forward_sample(x, pw, cache_nsa_kv, cache_win_kv, state_hgrn, state_gla, page_table, hg_lb_logits, hg_norm,
                    gla_ba, gla_norm, norm_mix, norm_ffn, norm_final):
    bsz, seq, d = x.shape
    x2d = x.reshape(bsz, d)
    npages = page_table.shape[1]
    nblk_total = npages * (PAGE // NSA_BLK) + 1
    nsel = min(NSA_TOPK, nblk_total) - 1

    p = _norm_proj(x2d, norm_mix[0], pw["wie"], bsz, PROJ_W // 2)
    o_h, hg_state = _hgrn_decode(p, hg_lb_logits, hg_norm[0], state_hgrn[0])

    cache2 = cache_nsa_kv[0].transpose(0, 2, 3, 4, 1).reshape(cache_nsa_kv.shape[1] * PAGE_ROWS, PAGE)
    q3 = p[:, EV_NQ:EV_CMP].reshape(bsz, NSA_H, NSA_DH)
    ocmp, sel = _dec_cmp(page_table, cache2, pw["pe_dec"], pw["w1dec"], pw["w2dec"], q3, nsel)
    sel = sel.reshape(bsz, NSA_G, 2, npages).transpose(0, 1, 3, 2).reshape(bsz, NSA_G, 2 * npages)
    idx = jnp.argsort(-sel, axis=-1, stable=True)[..., :NSA_TOPK].astype(I32)
    knew = p[:, EV_SLC:EV_GATE].reshape(bsz, 4, NSA_G * NSA_DH)
    wlen = cache_win_kv.shape[2]
    win_t = cache_win_kv[0].transpose(0, 2, 3, 4, 1).reshape(bsz, 2 * NSA_G * NSA_DH, wlen)
    o_n = _dec_attn(page_table, idx, cache2, q3, ocmp, knew, win_t, p[:, EV_GATE:EV_GATE + LANES], nsel)
    o_n = o_n.reshape(bsz, NSA_H * NSA_DH)
    x1 = _out_ffn([o_h, o_n], x2d, [pw["woe_h"], pw["woe_n"]], norm_ffn[0], pw["w13t"][0], pw["w2b"][0], None, bsz)

    rows = p[:, EV_CMP:EV_WIN].reshape(1, bsz, 1, 4, NSA_G, NSA_DH)
    win_new = p[:, EV_WIN:EV_GATE].reshape(1, bsz, 1, 2, NSA_G, NSA_DH)
    wk = jnp.concatenate([cache_win_kv[:1], win_new], axis=2)
    wn = min(NSA_WINDOW, npages * PAGE + 1)
    win = wk[:, :, wk.shape[2] - wn:]

    p2 = _norm_proj(x1, norm_mix[1], pw["wio"], bsz, ODD_W)
    o_g, gla_state = _gla_decode(p2, pw["wa2p"], gla_ba[0], gla_norm[0], state_gla[0])
    y = _out_ffn([o_g], x1, [pw["woo"]], norm_ffn[1], pw["w13t"][1], pw["w2b"][1], norm_final, bsz)
    return y.reshape(bsz, 1, d), rows, win, hg_state[None], gla_state[None]


def kernel(x_prompt, x_sample, cache_nsa_kv, cache_win_kv, state_hgrn, state_gla, page_table, w_in_even,
           hg_lb_logits, hg_norm, nsa_cmp_pos, nsa_cmp_w1, nsa_cmp_w2, w_out_even, w_in_odd, gla_wa2, gla_ba,
           gla_norm, w_out_odd, norm_mix, norm_ffn, norm_final, ffn_w13, ffn_w2):
    pw = _prep_weights(w_in_even, nsa_cmp_pos, nsa_cmp_w1, nsa_cmp_w2, w_out_even, w_in_odd, gla_wa2, w_out_odd,
                       ffn_w13, ffn_w2)
    y_p, kv_p, win_p, hg_p, gla_p = _forward_prompt(x_prompt, pw, hg_lb_logits, hg_norm, gla_ba, gla_norm,
                                                    norm_mix, norm_ffn, norm_final)
    y_s, kv_s, win_s, hg_s, gla_s = _forward_sample(x_sample, pw, cache_nsa_kv, cache_win_kv, state_hgrn,
                                                    state_gla, page_table, hg_lb_logits, hg_norm, gla_ba,
                                                    gla_norm, norm_mix, norm_ffn, norm_final)
    return (y_p, y_s, kv_p, kv_s, win_p, win_s, hg_p, hg_s, gla_p, gla_s)
```

```python
import functools

import jax
import jax.numpy as jnp
from jax import lax
from jax.experimental import pallas as pl
from jax.experimental.pallas import tpu as pltpu

F32 = jnp.float32
BF16 = jnp.bfloat16
I32 = jnp.int32

D_MODEL = 1024
HG_H, HG_DK, HG_DV = 4, 128, 128
NSA_H, NSA_DH, NSA_G, NSA_R = 8, 64, 2, 4
NSA_BLK = 64
NSA_TOPK = 16
NSA_WINDOW = 512
NSA_QB = 128
GLA_H, GLA_DK, GLA_DV = 4, 128, 256
GLA_RANK = 16
GLA_TAU = 16.0
D_FF = 2816
EPS = 1e-6
NEG = -1e30
TINY = 1e-30
PAGE = 128

PROJ_W = 3584
ODD_W = 3200
EV_Q, EV_F, EV_I, EV_OG = 0, 512, 1024, 1536
EV_NQ, EV_CMP, EV_SLC, EV_WIN, EV_GATE = 2048, 2560, 2816, 3072, 3328
OD_Q, OD_K, OD_V, OD_R, OD_A = 0, 512, 1024, 2048, 3072

LANES = 128
V7X_VMEM_LIMIT = 56 * 1024 * 1024
FF_TILE = 1408
CHUNK = 128
SUB = 16

NT_DIMS = (((1,), (1,)), ((), ()))
TN_DIMS = (((0,), (0,)), ((), ()))


def _cparams(sem):
    return pltpu.CompilerParams(dimension_semantics=sem, vmem_limit_bytes=V7X_VMEM_LIMIT)


def _rms(x, g):
    return x * lax.rsqrt(jnp.mean(x * x, axis=-1, keepdims=True) + EPS) * g


def _sigmoid(x):
    return 0.5 * jnp.tanh(0.5 * x) + 0.5


def _silu(x):
    return x * _sigmoid(x)


def _iota(shape, dim):
    return lax.broadcasted_iota(I32, shape, dim)


def _norm_proj_kernel(x_ref, g_ref, w_ref, o_ref, xn_ref):
    @pl.when(pl.program_id(1) == 0)
    def _():
        xn_ref[...] = _rms(x_ref[...], g_ref[...]).astype(BF16)

    o_ref[...] = jnp.dot(xn_ref[...], w_ref[...], preferred_element_type=F32)


def _norm_proj(x2d, gain, w, tm, tn):
    m, k = x2d.shape
    n = w.shape[1]
    return pl.pallas_call(
        _norm_proj_kernel,
        grid=(m // tm, n // tn),
        in_specs=[pl.BlockSpec((tm, k), lambda i, j: (i, 0)),
                  pl.BlockSpec((1, k), lambda i, j: (0, 0)),
                  pl.BlockSpec((k, tn), lambda i, j: (0, j))],
        out_specs=pl.BlockSpec((tm, tn), lambda i, j: (i, j)),
        out_shape=jax.ShapeDtypeStruct((m, n), F32),
        scratch_shapes=[pltpu.VMEM((tm, k), BF16)],
        compiler_params=_cparams(("parallel", "arbitrary")),
        name="norm_proj",
    )(x2d, gain.reshape(1, k), w)


PM_W = 2304
PM_CMP = 2048
T_Q, T_ROWS, T_WIN, T_GATE, T_END = 0, 512, 1024, 1280, 1312


def _norm_proj_t_kernel(x_ref, g_ref, w_ref, wt_ref, o_ref, q_ref, rows_ref, win_ref, gate_ref):
    xn = _rms(x_ref[...], g_ref[...]).astype(BF16)
    o_ref[...] = jnp.dot(xn, w_ref[...], preferred_element_type=F32)
    t = lax.dot_general(wt_ref[...], xn, NT_DIMS, preferred_element_type=F32)
    q_ref[0] = t[T_Q:T_ROWS]
    rows_ref[0] = t[T_ROWS:T_WIN]
    win_ref[0] = t[T_WIN:T_GATE]
    gate_ref[0] = t[T_GATE:T_END]


def _norm_proj_t(x2d, gain, w, wt, bsz, seq, tm):
    k = x2d.shape[1]
    nt = seq // tm
    tr = lambda rows: pl.BlockSpec((1, rows, tm), lambda b, t: (b, 0, t))
    sizes = (T_ROWS - T_Q, T_WIN - T_ROWS, T_GATE - T_WIN, T_END - T_GATE)
    return pl.pallas_call(
        _norm_proj_t_kernel,
        grid=(bsz, nt),
        in_specs=[pl.BlockSpec((tm, k), lambda b, t: (b * nt + t, 0)),
                  pl.BlockSpec((1, k), lambda b, t: (0, 0)),
                  pl.BlockSpec(w.shape, lambda b, t: (0, 0)),
                  pl.BlockSpec(wt.shape, lambda b, t: (0, 0))],
        out_specs=[pl.BlockSpec((tm, PM_W), lambda b, t: (b * nt + t, 0))] + [tr(r) for r in sizes],
        out_shape=[jax.ShapeDtypeStruct((bsz * seq, PM_W), F32)]
        + [jax.ShapeDtypeStruct((bsz, r, seq), F32) for r in sizes],
        compiler_params=_cparams(("parallel", "arbitrary")),
        name="norm_proj_t",
    )(x2d, gain.reshape(1, k), w, wt)


def _out_ffn_kernel(*refs, n_mix, final_norm):
    mix_refs = refs[:n_mix]
    res_ref = refs[n_mix]
    wo_refs = refs[n_mix + 1:2 * n_mix + 1]
    g_ref, w13_ref, w2_ref = refs[2 * n_mix + 1:2 * n_mix + 4]
    pos = 2 * n_mix + 4
    gf_ref = refs[pos] if final_norm else None
    pos += 1 if final_norm else 0
    o_ref, x1_ref, h_ref, acc_ref = refs[pos:pos + 4]
    j = pl.program_id(1)

    @pl.when(j == 0)
    def _():
        x1 = res_ref[...]
        for m_ref, w_ref in zip(mix_refs, wo_refs):
            x1 = x1 + jnp.dot(m_ref[...].astype(BF16), w_ref[...], preferred_element_type=F32)
        x1_ref[...] = x1
        h_ref[...] = _rms(x1, g_ref[...]).astype(BF16)
        acc_ref[...] = jnp.zeros_like(acc_ref)

    gu = jnp.dot(h_ref[...], w13_ref[...], preferred_element_type=F32)
    act = _silu(gu[:, :FF_TILE]) * gu[:, FF_TILE:]
    acc_ref[...] += jnp.dot(act.astype(BF16), w2_ref[...], preferred_element_type=F32)

    @pl.when(j == pl.num_programs(1) - 1)
    def _():
        y = x1_ref[...] + acc_ref[...]
        if final_norm:
            y = _rms(y, gf_ref[...])
        o_ref[...] = y


def _out_ffn(mixes, res, wos, g_ffn, w13t, w2, g_final, tm):
    m, d = res.shape
    n_mix = len(mixes)
    nj = D_FF // FF_TILE
    final_norm = g_final is not None
    in_specs = [pl.BlockSpec((tm, mx.shape[1]), lambda i, j: (i, 0)) for mx in mixes]
    in_specs.append(pl.BlockSpec((tm, d), lambda i, j: (i, 0)))
    in_specs += [pl.BlockSpec(w.shape, lambda i, j: (0, 0)) for w in wos]
    in_specs += [pl.BlockSpec((1, d), lambda i, j: (0, 0)),
                 pl.BlockSpec((d, 2 * FF_TILE), lambda i, j: (0, j)),
                 pl.BlockSpec((FF_TILE, d), lambda i, j: (j, 0))]
    args = list(mixes) + [res] + list(wos) + [g_ffn.reshape(1, d), w13t, w2]
    if final_norm:
        in_specs.append(pl.BlockSpec((1, d), lambda i, j: (0, 0)))
        args.append(g_final.reshape(1, d))
    return pl.pallas_call(
        functools.partial(_out_ffn_kernel, n_mix=n_mix, final_norm=final_norm),
        grid=(m // tm, nj),
        in_specs=in_specs,
        out_specs=pl.BlockSpec((tm, d), lambda i, j: (i, 0)),
        out_shape=jax.ShapeDtypeStruct((m, d), F32),
        scratch_shapes=[pltpu.VMEM((tm, d), F32), pltpu.VMEM((tm, d), BF16), pltpu.VMEM((tm, d), F32)],
        compiler_params=_cparams(("parallel", "arbitrary")),
        name="out_ffn",
    )(*args)


def _gla_chunk(q, k, v, g, st):
    c = q.shape[0]
    tri = (_iota((c, c), 0) >= _iota((c, c), 1)).astype(BF16)
    g_hi = g.astype(BF16)
    g_r1 = g - g_hi.astype(F32)
    g_mid = g_r1.astype(BF16)
    g_lo = (g_r1 - g_mid.astype(F32)).astype(BF16)
    b = (jnp.dot(tri, g_hi, preferred_element_type=F32) + jnp.dot(tri, g_mid, preferred_element_type=F32)
         + jnp.dot(tri, g_lo, preferred_element_type=F32))
    o = lax.dot_general((q * jnp.exp(b)).astype(BF16), st.astype(BF16), NT_DIMS, preferred_element_type=F32)
    lane = _iota((SUB, c), 1)
    row = _iota((SUB, c), 0)
    hs = SUB // 2
    rows = []
    for blk in range(c // SUB):
        lo = blk * SUB
        b_i, q_i, k_i = b[lo:lo + SUB], q[lo:lo + SUB], k[lo:lo + SUB]
        a_top = jnp.zeros((hs, c), F32)
        a_bot = jnp.zeros((hs, c), F32)
        for s in range(SUB):
            if s < hs:
                e = jnp.exp(b_i - b_i[s:s + 1])
                a = jnp.sum(q_i * e * k_i[s:s + 1], axis=1, keepdims=True)
                a_top = jnp.where(lane[:hs] == lo + s, a[:hs], a_top)
                a_bot = jnp.where(lane[:hs] == lo + s, a[hs:], a_bot)
            else:
                e = jnp.exp(b_i[hs:] - b_i[s:s + 1])
                a = jnp.sum(q_i[hs:] * e * k_i[s:s + 1], axis=1, keepdims=True)
                a_bot = jnp.where(lane[:hs] == lo + s, a, a_bot)
        a_blk = jnp.where(lane <= lo + row, jnp.concatenate([a_top, a_bot], axis=0), 0.0)
        if blk > 0:
            r = b[lo - 1:lo]
            qt = q_i * jnp.exp(b_i - r)
            kt = k * jnp.exp(jnp.minimum(r - b, 0.0))
            a_off = lax.dot_general(qt.astype(BF16), kt.astype(BF16), NT_DIMS, preferred_element_type=F32)
            a_blk = jnp.where(lane < lo, a_off, a_blk)
        rows.append(a_blk)
    a_full = jnp.concatenate(rows, axis=0)
    o = o + jnp.dot(a_full.astype(BF16), v.astype(BF16), preferred_element_type=F32)
    b_last = b[c - 1:c]
    kd = k * jnp.exp(b_last - b)
    st_new = st * jnp.exp(b_last) + lax.dot_general(v.astype(BF16), kd.astype(BF16), TN_DIMS,
                                                    preferred_element_type=F32)
    return o, st_new


def _lower_bound(lbl_ref, col):
    l = lbl_ref[:, col:col + HG_DK]
    e = jnp.exp(l - jnp.max(l, axis=0, keepdims=True))
    return e[0:1] / jnp.sum(e, axis=0, keepdims=True)


def _hgrn_kernel(q_ref, f_ref, i_ref, og_ref, lbl_ref, gain_ref, o_ref, s_ref, st_ref, *, nch):
    t = pl.program_id(1)

    @pl.when(t == 0)
    def _():
        st_ref[...] = jnp.zeros_like(st_ref)

    def body(ci, carry):
        r0 = pl.multiple_of(ci * CHUNK, CHUNK)
        rs = pl.ds(r0, CHUNK)
        for h in range(HG_H):
            cs = slice(h * HG_DK, (h + 1) * HG_DK)
            lb = _lower_bound(lbl_ref, h * HG_DK)
            q = _silu(q_ref[rs, cs])
            f = lb + (1.0 - lb) * _sigmoid(f_ref[rs, cs])
            o, st_new = _gla_chunk(q, 1.0 - f, i_ref[rs, cs], jnp.log(f), st_ref[h])
            st_ref[h] = st_new
            o = _rms(o, gain_ref[...]) * _sigmoid(og_ref[rs, cs])
            o_ref[rs, cs] = o.astype(o_ref.dtype)
        return carry

    lax.fori_loop(0, nch, body, 0)

    @pl.when(t == pl.num_programs(1) - 1)
    def _():
        for h in range(HG_H):
            s_ref[0, h] = st_ref[h].T


def _hgrn_prompt(p2d, lb_logits, gain, bsz, seq, tb):
    nt = seq // tb
    hk = HG_H * HG_DK
    row = lambda b, t: b * nt + t
    col_spec = lambda c: pl.BlockSpec((tb, hk), lambda b, t, c=c: (row(b, t), c))
    return pl.pallas_call(
        functools.partial(_hgrn_kernel, nch=tb // CHUNK),
        grid=(bsz, nt),
        in_specs=[col_spec(EV_Q // hk), col_spec(EV_F // hk), col_spec(EV_I // hk), col_spec(EV_OG // hk),
                  pl.BlockSpec(lb_logits.shape, lambda b, t: (0, 0)),
                  pl.BlockSpec((1, HG_DV), lambda b, t: (0, 0))],
        out_specs=[pl.BlockSpec((tb, hk), lambda b, t: (row(b, t), 0)),
                   pl.BlockSpec((1, HG_H, HG_DK, HG_DV), lambda b, t: (b, 0, 0, 0))],
        out_shape=[jax.ShapeDtypeStruct((bsz * seq, hk), BF16),
                   jax.ShapeDtypeStruct((bsz, HG_H, HG_DK, HG_DV), F32)],
        scratch_shapes=[pltpu.VMEM((HG_H, HG_DV, HG_DK), F32)],
        compiler_params=_cparams(("parallel", "arbitrary")),
        name="hgrn_chunk",
    )(p2d, p2d, p2d, p2d, lb_logits, gain.reshape(1, HG_DV))


def _log_sigmoid(x):
    return jnp.minimum(x, 0.0) - jnp.log(1.0 + jnp.exp(-jnp.abs(x)))


def _gla_kernel(q_ref, k_ref, v_ref, r_ref, a_ref, wa_ref, ba_ref, gain_ref, o_ref, s_ref, st_ref, *, nch):
    t = pl.program_id(1)

    @pl.when(t == 0)
    def _():
        st_ref[...] = jnp.zeros_like(st_ref)

    def body(ci, carry):
        r0 = pl.multiple_of(ci * CHUNK, CHUNK)
        rs = pl.ds(r0, CHUNK)
        gate = jnp.dot(a_ref[rs, :].astype(BF16), wa_ref[...], preferred_element_type=F32) + ba_ref[...]
        logf = _log_sigmoid(gate) * (1.0 / GLA_TAU)
        for h in range(GLA_H):
            ks = slice(h * GLA_DK, (h + 1) * GLA_DK)
            vs = slice(h * GLA_DV, (h + 1) * GLA_DV)
            q = q_ref[rs, ks] * (GLA_DK ** -0.5)
            o, st_new = _gla_chunk(q, k_ref[rs, ks], v_ref[rs, vs], logf[:, ks], st_ref[h])
            st_ref[h] = st_new
            o = _rms(o, gain_ref[...]) * _silu(r_ref[rs, vs])
            o_ref[rs, vs] = o.astype(o_ref.dtype)
        return carry

    lax.fori_loop(0, nch, body, 0)

    @pl.when(t == pl.num_programs(1) - 1)
    def _():
        for h in range(GLA_H):
            s_ref[0, h] = st_ref[h].T


def _gla_prompt(p2d, wa2p, ba, gain, bsz, seq, tb):
    nt = seq // tb
    hk, hv = GLA_H * GLA_DK, GLA_H * GLA_DV
    row = lambda b, t: b * nt + t
    return pl.pallas_call(
        functools.partial(_gla_kernel, nch=tb // CHUNK),
        grid=(bsz, nt),
        in_specs=[pl.BlockSpec((tb, hk), lambda b, t: (row(b, t), OD_Q // hk)),
                  pl.BlockSpec((tb, hk), lambda b, t: (row(b, t), OD_K // hk)),
                  pl.BlockSpec((tb, hv), lambda b, t: (row(b, t), OD_V // hv)),
                  pl.BlockSpec((tb, hv), lambda b, t: (row(b, t), OD_R // hv)),
                  pl.BlockSpec((tb, LANES), lambda b, t: (row(b, t), OD_A // LANES)),
                  pl.BlockSpec(wa2p.shape, lambda b, t: (0, 0)),
                  pl.BlockSpec((1, hk), lambda b, t: (0, 0)),
                  pl.BlockSpec((1, GLA_DV), lambda b, t: (0, 0))],
        out_specs=[pl.BlockSpec((tb, hv), lambda b, t: (row(b, t), 0)),
                   pl.BlockSpec((1, GLA_H, GLA_DK, GLA_DV), lambda b, t: (b, 0, 0, 0))],
        out_shape=[jax.ShapeDtypeStruct((bsz * seq, hv), BF16),
                   jax.ShapeDtypeStruct((bsz, GLA_H, GLA_DK, GLA_DV), F32)],
        scratch_shapes=[pltpu.VMEM((GLA_H, GLA_DV, GLA_DK), F32)],
        compiler_params=_cparams(("parallel", "arbitrary")),
        name="gla_chunk",
    )(p2d, p2d, p2d, p2d, p2d, wa2p, ba.reshape(1, hk), gain.reshape(1, GLA_DV))


def _to_columns(x):
    bsz = x.shape[0]
    if bsz < LANES:
        x = jnp.concatenate([x, jnp.zeros((LANES - bsz, x.shape[1]), x.dtype)], axis=0)
    return x.T


def _decode_update(q, k, v, g, s_ref, so_ref, o_scr):
    bsz = q.shape[0]
    qt, kt, et = _to_columns(q), _to_columns(k), _to_columns(jnp.exp(g))
    for b in range(bsz):
        s_new = et[:, b:b + 1] * s_ref[b, 0] + kt[:, b:b + 1] * v[b:b + 1, :]
        so_ref[b, 0] = s_new
        o_scr[b:b + 1, :] = jnp.sum(qt[:, b:b + 1] * s_new, axis=0, keepdims=True)


def _hgrn_decode_kernel(q_ref, f_ref, i_ref, og_ref, lbl_ref, gain_ref, s_ref, o_ref, so_ref, o_scr):
    l = lbl_ref[0]
    e = jnp.exp(l - jnp.max(l, axis=0, keepdims=True))
    lb = e[0:1] / jnp.sum(e, axis=0, keepdims=True)
    f = lb + (1.0 - lb) * _sigmoid(f_ref[...])
    _decode_update(_silu(q_ref[...]), 1.0 - f, i_ref[...], jnp.log(f), s_ref, so_ref, o_scr)
    o_ref[...] = _rms(o_scr[...], gain_ref[...]) * _sigmoid(og_ref[...])


def _hgrn_decode(p2d, lb_logits, gain, state):
    bsz = p2d.shape[0]
    col = lambda c: pl.BlockSpec((bsz, HG_DK), lambda h, c=c: (0, c + h))
    lbl3 = lb_logits.reshape(lb_logits.shape[0], HG_H, HG_DK).transpose(1, 0, 2)
    st_spec = pl.BlockSpec((bsz, 1, HG_DK, HG_DV), lambda h: (0, h, 0, 0))
    return pl.pallas_call(
        _hgrn_decode_kernel,
        grid=(HG_H,),
        in_specs=[col(EV_Q // HG_DK), col(EV_F // HG_DK), col(EV_I // HG_DK), col(EV_OG // HG_DK),
                  pl.BlockSpec((1,) + lbl3.shape[1:], lambda h: (h, 0, 0)),
                  pl.BlockSpec((1, HG_DV), lambda h: (0, 0)),
                  st_spec],
        out_specs=[pl.BlockSpec((bsz, HG_DV), lambda h: (0, h)), st_spec],
        out_shape=[jax.ShapeDtypeStruct((bsz, HG_H * HG_DV), F32),
                   jax.ShapeDtypeStruct(state.shape, F32)],
        scratch_shapes=[pltpu.VMEM((bsz, HG_DV), F32)],
        compiler_params=_cparams(("arbitrary",)),
        name="hgrn_decode",
    )(p2d, p2d, p2d, p2d, lbl3, gain.reshape(1, HG_DV), state)


def _gla_decode_kernel(q_ref, k_ref, v_ref, r_ref, a_ref, wa_ref, ba_ref, gain_ref, s_ref, o_ref, so_ref, o_scr):
    gate = jnp.dot(a_ref[...].astype(BF16), wa_ref[...], preferred_element_type=F32) + ba_ref[...]
    logf = _log_sigmoid(gate) * (1.0 / GLA_TAU)
    _decode_update(q_ref[...] * (GLA_DK ** -0.5), k_ref[...], v_ref[...], logf, s_ref, so_ref, o_scr)
    o_ref[...] = _rms(o_scr[...], gain_ref[...]) * _silu(r_ref[...])


def _gla_decode(p2d, wa2p, ba, gain, state):
    bsz = p2d.shape[0]
    hk = GLA_H * GLA_DK
    st_spec = pl.BlockSpec((bsz, 1, GLA_DK, GLA_DV), lambda h: (0, h, 0, 0))
    return pl.pallas_call(
        _gla_decode_kernel,
        grid=(GLA_H,),
        in_specs=[pl.BlockSpec((bsz, GLA_DK), lambda h: (0, OD_Q // GLA_DK + h)),
                  pl.BlockSpec((bsz, GLA_DK), lambda h: (0, OD_K // GLA_DK + h)),
                  pl.BlockSpec((bsz, GLA_DV), lambda h: (0, OD_V // GLA_DV + h)),
                  pl.BlockSpec((bsz, GLA_DV), lambda h: (0, OD_R // GLA_DV + h)),
                  pl.BlockSpec((bsz, LANES), lambda h: (0, OD_A // LANES)),
                  pl.BlockSpec((LANES, GLA_DK), lambda h: (0, h)),
                  pl.BlockSpec((1, GLA_DK), lambda h: (0, h)),
                  pl.BlockSpec((1, GLA_DV), lambda h: (0, 0)),
                  st_spec],
        out_specs=[pl.BlockSpec((bsz, GLA_DV), lambda h: (0, h)), st_spec],
        out_shape=[jax.ShapeDtypeStruct((bsz, GLA_H * GLA_DV), F32),
                   jax.ShapeDtypeStruct(state.shape, F32)],
        scratch_shapes=[pltpu.VMEM((bsz, GLA_DV), F32)],
        compiler_params=_cparams(("arbitrary",)),
        name="gla_decode",
    )(p2d, p2d, p2d, p2d, p2d, wa2p, ba.reshape(1, hk), gain.reshape(1, GLA_DV), state)


def _compress_kernel(xa_ref, xb_ref, pe_ref, w1_ref, w2_ref, o_ref, acc_ref):
    nb = o_ref.shape[0]
    acc_ref[...] = jnp.zeros_like(acc_ref)
    for l in range(NSA_BLK):
        rows = pl.ds(l, nb, stride=NSA_BLK)
        x = jnp.concatenate([xa_ref[rows, :], xb_ref[rows, :]], axis=1) + pe_ref[l:l + 1, :]
        acc_ref[...] += jnp.dot(x.astype(BF16), w1_ref[l], preferred_element_type=F32)
    o_ref[...] = jnp.dot(_silu(acc_ref[...]).astype(BF16), w2_ref[...], preferred_element_type=F32)


def _compress_prompt(p2d, pe4, w1bd, w2bd):
    n = p2d.shape[0]
    rc = min(8192, n)
    cw = 4 * NSA_DH
    c0 = PM_CMP // LANES
    return pl.pallas_call(
        _compress_kernel,
        grid=(n // rc,),
        in_specs=[pl.BlockSpec((rc, LANES), lambda i: (i, c0)),
                  pl.BlockSpec((rc, LANES), lambda i: (i, c0 + 1)),
                  pl.BlockSpec(pe4.shape, lambda i: (0, 0)),
                  pl.BlockSpec(w1bd.shape, lambda i: (0, 0, 0)),
                  pl.BlockSpec((cw, cw), lambda i: (0, 0))],
        out_specs=pl.BlockSpec((rc // NSA_BLK, cw), lambda i: (i, 0)),
        out_shape=jax.ShapeDtypeStruct((n // NSA_BLK, cw), F32),
        scratch_shapes=[pltpu.VMEM((rc // NSA_BLK, cw), F32)],
        compiler_params=_cparams(("arbitrary",)),
        name="nsa_compress",
    )(p2d, p2d, pe4, w1bd, w2bd)


AQB = 256
AQB_SHIFT = 8
AUG = 128
ONES_ROWS = 16


def _slopes_lane(g, lanes):
    r = _iota((1, lanes), 1) >> AQB_SHIFT
    out = jnp.zeros((1, lanes), F32)
    for rr in range(NSA_R):
        out = jnp.where(r == rr, 2.0 ** (-(g * NSA_R + rr + 1)), out)
    return out


def _build_key_features(src_ref, row0, dst_ref, g, seq, with_onehot):
    r = _iota((NSA_DH, LANES), 0)
    for cb in range(seq // LANES):
        kpos = cb * LANES + _iota((NSA_DH, LANES), 1)
        feat = jnp.where(r == 32, (kpos >> 6).astype(F32),
                         jnp.where(r == 33, (kpos & 63).astype(F32),
                                   jnp.where((r == 34) | (r == 35), 1.0, 0.0)))
        if with_onehot:
            feat = jnp.where((kpos >> 6) == r, 1.0, feat)
        kt = src_ref[0, row0:row0 + NSA_DH, cb * LANES:(cb + 1) * LANES]
        dst_ref[g, cb * LANES:(cb + 1) * LANES, :] = jnp.concatenate([kt, feat], axis=0).T.astype(BF16)


def _attn_tiles(branches, k0, qa_ref, mask, m_ref, acc_ref, nkeys=AQB):
    jobs = [(k_ref, v_ref, v_row0 + g * NSA_DH, stream0 + g, g)
            for k_ref, v_ref, v_row0, stream0 in branches for g in range(NSA_G)]
    s = [jnp.dot(k_ref[g, pl.ds(k0, nkeys), :], qa_ref[g], preferred_element_type=F32)
         for k_ref, _, _, _, g in jobs]
    if mask is not None:
        s = [jnp.where(mask, sj, NEG) for sj in s]
    m_old = [m_ref[st] for _, _, _, st, _ in jobs]
    m_new = [jnp.maximum(mo, jnp.max(sj, axis=0, keepdims=True)) for mo, sj in zip(m_old, s)]
    p = [jnp.exp(sj - mn).astype(BF16) for sj, mn in zip(s, m_new)]
    ones = jnp.ones((ONES_ROWS, nkeys), BF16)
    for i, (_, v_ref, v_row, st, _) in enumerate(jobs):
        vt = v_ref[0, v_row:v_row + NSA_DH, pl.ds(k0, nkeys)].astype(BF16)
        pv = jnp.dot(jnp.concatenate([vt, ones], axis=0), p[i], preferred_element_type=F32)
        acc_ref[st] = jnp.exp(m_old[i] - m_new[i]) * acc_ref[st] + pv
        m_ref[st] = m_new[i]


def _nsa_attn_kernel(qt_ref, rows_ref, win_ref, gt_ref, kc_ref, vct_ref, o_ref, ks_s, kw_s, qa_s, m_s, acc_s,
                     *, nblk):
    qb = pl.program_id(1)
    seq = rows_ref.shape[2]
    q0 = qb * AQB
    nq = NSA_R * AQB
    half = NSA_G * NSA_DH

    @pl.when(qb == 0)
    def _():
        for g in range(NSA_G):
            _build_key_features(rows_ref, 2 * half + g * NSA_DH, ks_s, g, seq, True)
            _build_key_features(win_ref, g * NSA_DH, kw_s, g, seq, False)

    pq = q0 + (_iota((1, nq), 1) & (AQB - 1))
    gates = _sigmoid(gt_ref[0])
    ksel = min(NSA_TOPK, nblk)
    scale = NSA_DH ** -0.5
    key_i = _iota((AQB, nq), 0)
    qry_t = _iota((AQB, nq), 1) & (AQB - 1)
    causal = key_i <= qry_t
    beyond = key_i > qry_t
    pq_f = q0 + _iota((32, AQB), 1)
    feat_row = _iota((32, AQB), 0)

    m_s[...] = jnp.full(m_s.shape, NEG, F32)
    acc_s[...] = jnp.zeros_like(acc_s)

    def result(i):
        return acc_s[i, 0:NSA_DH, :] / jnp.maximum(acc_s[i, NSA_DH:NSA_DH + 1, :], TINY)

    o_cmp = []
    for g in range(NSA_G):
        slope = _slopes_lane(g, nq)

        qt_g = [qt_ref[0, (g * NSA_R + r) * NSA_DH:(g * NSA_R + r + 1) * NSA_DH, :] * scale for r in range(NSA_R)]
        qs_t = jnp.concatenate(qt_g, axis=1).astype(BF16)
        sc = jnp.dot(kc_ref[0, g], qs_t, preferred_element_type=F32)
        dist_c = pq - (_iota((nblk, nq), 0) * NSA_BLK + NSA_BLK - 1)
        mask_c = dist_c >= 0
        sc = jnp.where(mask_c, sc - slope * dist_c.astype(F32), NEG)
        pc = jnp.where(mask_c, jnp.exp(sc - jnp.max(sc, axis=0, keepdims=True)), 0.0)
        pc = pc / jnp.maximum(jnp.sum(pc, axis=0, keepdims=True), TINY)
        o_cmp.append(jnp.dot(vct_ref[0, g], pc.astype(BF16), preferred_element_type=F32))
        imp = pc[:, 0:AQB]
        for r in range(1, NSA_R):
            imp = imp + pc[:, r * AQB:(r + 1) * AQB]

        n_io = _iota((nblk, AQB), 0)
        tpos = q0 + _iota((nblk, AQB), 1)
        forced = (n_io == (tpos >> 6)) | (n_io == 0)
        started = n_io * NSA_BLK <= tpos
        score = jnp.where(forced, jnp.inf, jnp.where(started, imp, -jnp.inf))
        rank = jnp.zeros((nblk, AQB), I32)
        for i in range(nblk):
            row = score[i:i + 1, :]
            rank = rank + ((row > score) | ((row == score) & (i < n_io))).astype(I32)
        sel_bias = jnp.where((rank < ksel) & started, 0.0, NEG)
        if nblk < 32:
            sel_bias = jnp.concatenate([sel_bias, jnp.zeros((32 - nblk, AQB), F32)], axis=0)

        cols = []
        for r in range(NSA_R):
            sl = 2.0 ** (-(g * NSA_R + r + 1))
            pos_feat = jnp.where(feat_row == 0, 64.0 * sl,
                                 jnp.where(feat_row == 1, sl,
                                           jnp.where(feat_row == 2, (-64.0 * sl) * (pq_f >> 6).astype(F32),
                                                     jnp.where(feat_row == 3, (-sl) * (pq_f & 63).astype(F32), 0.0))))
            cols.append(jnp.concatenate([qt_g[r], sel_bias, pos_feat], axis=0))
        qa_s[g] = jnp.concatenate(cols, axis=1).astype(BF16)

    slc_branch = (ks_s, rows_ref, 3 * half, 0)
    win_branch = (kw_s, win_ref, half, NSA_G)

    def slc_tile(k0, mask, nkeys=AQB):
        _attn_tiles([slc_branch], k0, qa_s, mask, m_s, acc_s, nkeys)

    def win_tile(k0, mask, nkeys=AQB):
        _attn_tiles([win_branch], k0, qa_s, mask, m_s, acc_s, nkeys)

    def slc_body(kt, c):
        slc_tile(pl.multiple_of(kt * 2 * AQB, 2 * AQB), None, 2 * AQB)
        return c

    lax.fori_loop(0, qb >> 1, slc_body, 0)

    @pl.when((qb & 1) == 1)
    def _():
        slc_tile(pl.multiple_of(q0 - AQB, AQB), None)

    @pl.when(qb >= 2)
    def _():
        win_tile(pl.multiple_of(q0 - 2 * AQB, AQB), jnp.concatenate([beyond, jnp.ones_like(beyond)], axis=0), 2 * AQB)

    @pl.when(qb == 1)
    def _():
        win_tile(0, None)

    _attn_tiles([slc_branch, win_branch], pl.multiple_of(q0, AQB), qa_s, causal, m_s, acc_s)

    for g in range(NSA_G):
        def gate_row(j, g=g):
            return jnp.concatenate([gates[g * 12 + r * 3 + j:g * 12 + r * 3 + j + 1, :] for r in range(NSA_R)],
                                   axis=1)

        o_t = gate_row(0) * o_cmp[g] + gate_row(1) * result(g) + gate_row(2) * result(NSA_G + g)
        o_st = jnp.concatenate([o_t[:, r * AQB:(r + 1) * AQB] for r in range(NSA_R)], axis=0)
        o_ref[:, g * NSA_R * NSA_DH:(g + 1) * NSA_R * NSA_DH] = o_st.T.astype(o_ref.dtype)


def _nsa_attn_prompt(qt, rows_t, win_t, gate_t, kc, vct, bsz, seq):
    assert NSA_WINDOW == 2 * AQB and seq % AQB == 0 and seq // NSA_BLK <= 32
    nqb = seq // AQB
    nblk = seq // NSA_BLK
    qw = NSA_H * NSA_DH
    nq = NSA_R * AQB
    per_b = lambda a: pl.BlockSpec((1,) + a.shape[1:], lambda b, i: (b,) + (0,) * (a.ndim - 1))
    per_q = lambda a: pl.BlockSpec((1, a.shape[1], AQB), lambda b, i: (b, 0, i))
    return pl.pallas_call(
        functools.partial(_nsa_attn_kernel, nblk=nblk),
        grid=(bsz, nqb),
        in_specs=[per_q(qt), per_b(rows_t), per_b(win_t), per_q(gate_t), per_b(kc), per_b(vct)],
        out_specs=pl.BlockSpec((AQB, qw), lambda b, i: (b * nqb + i, 0)),
        out_shape=jax.ShapeDtypeStruct((bsz * seq, qw), BF16),
        scratch_shapes=[pltpu.VMEM((NSA_G, seq, AUG), BF16), pltpu.VMEM((NSA_G, seq, AUG), BF16),
                        pltpu.VMEM((NSA_G, AUG, nq), BF16),
                        pltpu.VMEM((2 * NSA_G, 1, nq), F32), pltpu.VMEM((2 * NSA_G, NSA_DH + ONES_ROWS, nq), F32)],
        compiler_params=_cparams(("parallel", "arbitrary")),
        name="nsa_attn",
    )(qt, rows_t, win_t, gate_t, kc, vct)


QROWS = 16


def _head_slopes(rows):
    r = _iota((rows, 1), 0)
    out = jnp.zeros((rows, 1), F32)
    for h in range(NSA_H):
        out = jnp.where(r == h, 2.0 ** (-(h + 1)), out)
    return out


def _softmax_rows(s, valid):
    s = jnp.where(valid, s, NEG)
    p = jnp.where(valid, jnp.exp(s - jnp.max(s, axis=1, keepdims=True)), 0.0)
    return p / jnp.maximum(jnp.sum(p, axis=1, keepdims=True), TINY)


def _padded_queries(q_ref):
    q = q_ref[0] * (NSA_DH ** -0.5)
    return jnp.concatenate([q, jnp.zeros((QROWS - NSA_H, NSA_DH), F32)], axis=0).astype(BF16)


PLANES = 4 * NSA_G
PAGE_ROWS = PLANES * NSA_DH
CMP_PLANES = 2 * NSA_G


def _dec_cmp_kernel(pt_ref, cache_ref, pe_ref, w1_ref, w2_ref, q_ref, o_ref, sel_ref, buf, acc_ref, cmp_ref, sem,
                    *, npages, nsel):
    b = pl.program_id(0)
    nb = npages * (PAGE // NSA_BLK)
    nrow = npages * CMP_PLANES
    crows = CMP_PLANES * NSA_DH

    def page_copy(pg):
        src = pl.multiple_of(pt_ref[b * npages + pg] * PAGE_ROWS, PAGE_ROWS)
        return pltpu.make_async_copy(cache_ref.at[pl.ds(src, crows), :],
                                     buf.at[pl.ds(pl.multiple_of(pg * crows, crows), crows), :], sem.at[0])

    def start(pg, c):
        page_copy(pg).start()
        return c

    def wait(pg, c):
        page_copy(pg).wait()
        return c

    lax.fori_loop(0, npages, start, 0)
    lax.fori_loop(0, npages, wait, 0)

    is_k = ((_iota((nrow, 1), 0) >> 1) & 1) == 0
    acc_ref[...] = jnp.zeros_like(acc_ref)
    for d in range(NSA_DH):
        x = buf[pl.ds(d, nrow, stride=NSA_DH), :]
        x = x + jnp.where(is_k, pe_ref[d, 0:1, :], pe_ref[d, 1:2, :])
        acc_ref[...] += jnp.dot(x.astype(BF16), w1_ref[d], preferred_element_type=F32)
    acc = acc_ref[...]
    h = _silu(jnp.where(is_k, acc[:, :LANES], acc[:, LANES:]))
    c2 = jnp.dot(h.astype(BF16), w2_ref[...], preferred_element_type=F32)
    cmp_ref[...] = jnp.where(is_k, c2[:, :LANES], c2[:, LANES:])

    pos = nb * NSA_BLK
    q16 = _padded_queries(q_ref)
    slope = _head_slopes(QROWS)
    row_grp = _iota((QROWS, 1), 0) >> 2
    lane = _iota((1, nb), 1)
    blk_of = jnp.where(lane < npages, 2 * lane, 2 * (lane - npages) + 1)
    dist = (pos - (blk_of * NSA_BLK + NSA_BLK - 1)).astype(F32)
    ri, ci = _iota((nb, nb), 0), _iota((nb, nb), 1)
    blk_r = jnp.where(ri < npages, 2 * ri, 2 * (ri - npages) + 1)
    blk_c = jnp.where(ci < npages, 2 * ci, 2 * (ci - npages) + 1)
    o_all = jnp.zeros((QROWS, NSA_DH), F32)
    for g in range(NSA_G):
        kc = cmp_ref[pl.ds(g, npages, stride=CMP_PLANES), :].astype(BF16)
        vc = cmp_ref[pl.ds(NSA_G + g, npages, stride=CMP_PLANES), :].astype(BF16)
        s = jnp.concatenate([lax.dot_general(q16, kc[:, j * NSA_DH:(j + 1) * NSA_DH], NT_DIMS,
                                             preferred_element_type=F32) for j in range(2)], axis=1)
        p = _softmax_rows(s - slope * dist, dist >= 0)
        pb = p.astype(BF16)
        o_g = sum(jnp.dot(pb[:, j * npages:(j + 1) * npages], vc[:, j * NSA_DH:(j + 1) * NSA_DH],
                          preferred_element_type=F32) for j in range(2))
        o_all = jnp.where(row_grp == g, o_g, o_all)
        imp = jnp.sum(jnp.where(row_grp == g, p, 0.0), axis=0, keepdims=True)
        score_row = jnp.where(blk_of == 0, jnp.inf, imp)
        score_col = jnp.sum(jnp.where(ri == ci, jnp.broadcast_to(score_row, (nb, nb)), 0.0),
                            axis=1, keepdims=True)
        beats = (score_col > score_row) | ((score_col == score_row) & (blk_r < blk_c))
        rank = jnp.sum(beats.astype(I32), axis=0, keepdims=True)
        sel_ref[0, g:g + 1, :] = (rank < nsel).astype(F32)
    o_ref[0] = o_all[:NSA_H]


def _dec_cmp(page_table, cache2, pe_dec, w1dec, w2dec, q3, nsel):
    bsz, npages = page_table.shape
    nb = npages * (PAGE // NSA_BLK)
    nrow = npages * CMP_PLANES
    grid_spec = pltpu.PrefetchScalarGridSpec(
        num_scalar_prefetch=1,
        grid=(bsz,),
        in_specs=[pl.BlockSpec(memory_space=pl.ANY),
                  pl.BlockSpec(pe_dec.shape, lambda b, pt: (0, 0, 0)),
                  pl.BlockSpec(w1dec.shape, lambda b, pt: (0, 0, 0)),
                  pl.BlockSpec(w2dec.shape, lambda b, pt: (0, 0)),
                  pl.BlockSpec((1, NSA_H, NSA_DH), lambda b, pt: (b, 0, 0))],
        out_specs=[pl.BlockSpec((1, NSA_H, NSA_DH), lambda b, pt: (b, 0, 0)),
                   pl.BlockSpec((1, NSA_G, nb), lambda b, pt: (b, 0, 0))],
        scratch_shapes=[pltpu.VMEM((nrow * NSA_DH, LANES), F32), pltpu.VMEM((nrow, 2 * LANES), F32),
                        pltpu.VMEM((nrow, LANES), F32), pltpu.SemaphoreType.DMA((1,))])
    return pl.pallas_call(
        functools.partial(_dec_cmp_kernel, npages=npages, nsel=nsel),
        grid_spec=grid_spec,
        out_shape=[jax.ShapeDtypeStruct((bsz, NSA_H, NSA_DH), F32),
                   jax.ShapeDtypeStruct((bsz, NSA_G, nb), F32)],
        compiler_params=_cparams(("arbitrary",)),
        name="nsa_dec_cmp",
    )(page_table.reshape(-1), cache2, pe_dec, w1dec, w2dec, q3)


def _attend_with_self(q16, slope, kt, vt, dist, valid, k_self, v_self):
    s = jnp.dot(q16, kt.astype(BF16), preferred_element_type=F32) - slope * dist
    s_self = jnp.sum(q16.astype(F32) * k_self.astype(BF16).astype(F32), axis=1, keepdims=True)
    m = jnp.maximum(jnp.max(jnp.where(valid, s, NEG), axis=1, keepdims=True), s_self)
    p = jnp.where(valid, jnp.exp(s - m), 0.0)
    p_self = jnp.exp(s_self - m)
    num = lax.dot_general(p.astype(BF16), vt.astype(BF16), NT_DIMS, preferred_element_type=F32) + p_self * v_self
    return num / jnp.maximum(jnp.sum(p, axis=1, keepdims=True) + p_self, TINY)


def _dec_attn_kernel(pt_ref, idx_ref, cache_ref, q_ref, oc_ref, kn_ref, win_ref, gt_ref, o_ref, kbuf, vbuf, sem,
                     *, npages, nsel, wlen):
    b = pl.program_id(0)
    pos = npages * PAGE

    copies = []
    for g in range(NSA_G):
        for j in range(nsel):
            blk = idx_ref[(b * NSA_G + g) * NSA_TOPK + j]
            base = pt_ref[b * npages + (blk >> 1)] * PAGE_ROWS
            dst = pl.ds(j * PAGE, PAGE)
            k_rows = pl.ds(pl.multiple_of(base + (2 * NSA_G + g) * NSA_DH, NSA_DH), NSA_DH)
            v_rows = pl.ds(pl.multiple_of(base + (3 * NSA_G + g) * NSA_DH, NSA_DH), NSA_DH)
            copies.append(pltpu.make_async_copy(cache_ref.at[k_rows, :], kbuf.at[g, :, dst], sem.at[0]))
            copies.append(pltpu.make_async_copy(cache_ref.at[v_rows, :], vbuf.at[g, :, dst], sem.at[0]))
    for cp in copies:
        cp.start()
    for cp in copies:
        cp.wait()

    q16 = _padded_queries(q_ref)
    slope = _head_slopes(QROWS)
    row_grp = _iota((QROWS, 1), 0) >> 2
    kn = kn_ref[0]
    lane = _iota((1, nsel * PAGE), 1)
    within = lane & (PAGE - 1)
    wdist = wlen - _iota((1, wlen), 1)
    o_slc = jnp.zeros((QROWS, NSA_DH), F32)
    o_win = jnp.zeros((QROWS, NSA_DH), F32)
    for g in range(NSA_G):
        gs = slice(g * NSA_DH, (g + 1) * NSA_DH)
        blk_lane = jnp.zeros((1, nsel * PAGE), I32)
        for j in range(nsel):
            blk_lane = jnp.where((lane >> 7) == j, idx_ref[(b * NSA_G + g) * NSA_TOPK + j], blk_lane)
        valid = (within >> 6) == (blk_lane & 1)
        kpos = (blk_lane >> 1) * PAGE + within
        o_g = _attend_with_self(q16, slope, kbuf[g], vbuf[g], (pos - kpos).astype(F32), valid,
                                kn[0:1, gs], kn[1:2, gs])
        o_slc = jnp.where(row_grp == g, o_g, o_slc)
        o_g = _attend_with_self(q16, slope, win_ref[0, g * NSA_DH:(g + 1) * NSA_DH, :],
                                win_ref[0, (NSA_G + g) * NSA_DH:(NSA_G + g + 1) * NSA_DH, :],
                                wdist.astype(F32), wdist < NSA_WINDOW, kn[2:3, gs], kn[3:4, gs])
        o_win = jnp.where(row_grp == g, o_g, o_win)

    gates = _sigmoid(gt_ref[pl.ds(b, 1), :])
    r_io = _iota((QROWS, LANES), 0)
    l_io = _iota((QROWS, LANES), 1)

    def gate_col(j):
        return jnp.sum(jnp.where(l_io == 3 * r_io + j, gates, 0.0), axis=1, keepdims=True)

    o_cmp = jnp.concatenate([oc_ref[0], jnp.zeros((QROWS - NSA_H, NSA_DH), F32)], axis=0)
    o = gate_col(0) * o_cmp + gate_col(1) * o_slc + gate_col(2) * o_win
    o_ref[0] = o[:NSA_H]


def _dec_attn(page_table, idx, cache2, q3, ocmp, knew, win_t, gates, nsel):
    bsz, npages = page_table.shape
    wlen = win_t.shape[2]
    hd = pl.BlockSpec((1, NSA_H, NSA_DH), lambda b, pt, ix: (b, 0, 0))
    grid_spec = pltpu.PrefetchScalarGridSpec(
        num_scalar_prefetch=2,
        grid=(bsz,),
        in_specs=[pl.BlockSpec(memory_space=pl.ANY), hd, hd,
                  pl.BlockSpec((1,) + knew.shape[1:], lambda b, pt, ix: (b, 0, 0)),
                  pl.BlockSpec((1,) + win_t.shape[1:], lambda b, pt, ix: (b, 0, 0)),
                  pl.BlockSpec(gates.shape, lambda b, pt, ix: (0, 0))],
        out_specs=hd,
        scratch_shapes=[pltpu.VMEM((NSA_G, NSA_DH, nsel * PAGE), F32), pltpu.VMEM((NSA_G, NSA_DH, nsel * PAGE), F32),
                        pltpu.SemaphoreType.DMA((1,))])
    return pl.pallas_call(
        functools.partial(_dec_attn_kernel, npages=npages, nsel=nsel, wlen=wlen),
        grid_spec=grid_spec,
        out_shape=jax.ShapeDtypeStruct((bsz, NSA_H, NSA_DH), F32),
        compiler_params=_cparams(("arbitrary",)),
        name="nsa_dec_attn",
    )(page_table.reshape(-1), idx.reshape(-1), cache2, q3, ocmp, knew, win_t, gates)


def _prep_weights(w_in_even, nsa_cmp_pos, nsa_cmp_w1, nsa_cmp_w2, w_out_even, w_in_odd, gla_wa2, w_out_odd,
                  ffn_w13, ffn_w2):
    d = D_MODEL
    wie = jnp.pad(w_in_even[0], ((0, 0), (0, PROJ_W - w_in_even.shape[2]))).astype(BF16)
    a0 = 2 * GLA_H * GLA_DK + GLA_H * GLA_DV
    wo = w_in_odd[0]
    wio = jnp.concatenate([wo[:, :a0], wo[:, a0 + GLA_RANK:], wo[:, a0:a0 + GLA_RANK]], axis=1)
    wio = jnp.pad(wio, ((0, 0), (0, ODD_W - wio.shape[1]))).astype(BF16)
    wa2p = jnp.pad(gla_wa2[0], ((0, LANES - GLA_RANK), (0, 0))).astype(BF16)
    hv = HG_H * HG_DV
    woe_h, woe_n = w_out_even[0, :hv].astype(BF16), w_out_even[0, hv:].astype(BF16)
    woo = w_out_odd[0].astype(BF16)
    nj = D_FF // FF_TILE
    w13t, w2b = [], []
    for l in range(ffn_w13.shape[0]):
        gate = ffn_w13[l, :, :D_FF].reshape(d, nj, FF_TILE)
        up = ffn_w13[l, :, D_FF:].reshape(d, nj, FF_TILE)
        w13t.append(jnp.concatenate([gate, up], axis=2).reshape(d, 2 * D_FF).astype(BF16))
        w2b.append(ffn_w2[l].astype(BF16))
    w1 = nsa_cmp_w1[0].reshape(2, NSA_BLK, NSA_DH, NSA_DH)
    w2 = nsa_cmp_w2[0]
    cw = 4 * NSA_DH
    dh = NSA_DH
    w1bd = jnp.concatenate([jnp.pad(w1[i // NSA_G], ((0, 0), (0, 0), (i * dh, cw - (i + 1) * dh)))
                            for i in range(2 * NSA_G)], axis=1)
    w2bd = jnp.concatenate([jnp.pad(w2[i // NSA_G], ((0, 0), (i * dh, cw - (i + 1) * dh)))
                            for i in range(2 * NSA_G)], axis=0)
    pe = nsa_cmp_pos[0]
    pe4 = jnp.concatenate([pe[0], pe[0], pe[1], pe[1]], axis=1)
    we = w_in_even[0]
    hgw = 2 * HG_H * HG_DK + 2 * HG_H * HG_DV
    nq0 = hgw + NSA_H * NSA_DH
    w_main = jnp.concatenate([we[:, :hgw], we[:, nq0:nq0 + cw]], axis=1).astype(BF16)
    w_tr = jnp.pad(we[:, hgw:].T, ((0, T_END - (we.shape[1] - hgw)), (0, 0))).astype(BF16)
    w1t = w1.transpose(0, 2, 1, 3)
    z1 = jnp.zeros_like(w1t[0])
    w1dec = jnp.concatenate([jnp.concatenate([w1t[0], z1, w1t[1], z1], axis=2),
                             jnp.concatenate([z1, w1t[0], z1, w1t[1]], axis=2)], axis=1)
    z2 = jnp.zeros_like(w2[0])
    w2dec = jnp.concatenate([jnp.concatenate([w2[0], z2, w2[1], z2], axis=1),
                             jnp.concatenate([z2, w2[0], z2, w2[1]], axis=1)], axis=0)
    pe_dec = jnp.tile(pe.transpose(2, 0, 1), (1, 1, 2))
    return dict(wie=wie, wio=wio, wa2p=wa2p, woe_h=woe_h, woe_n=woe_n, woo=woo, w13t=w13t, w2b=w2b,
                w1bd=w1bd.astype(BF16), w2bd=w2bd.astype(BF16), pe4=pe4, w_main=w_main, w_tr=w_tr,
                w1dec=w1dec.astype(BF16), w2dec=w2dec.astype(BF16), pe_dec=pe_dec)


def _forward_prompt(x, pw, hg_lb_logits, hg_norm, gla_ba, gla_norm, norm_mix, norm_ffn, norm_final):
    bsz, seq, d = x.shape
    n = bsz * seq
    x2d = x.reshape(n, d)
    tm = 512 if seq % 512 == 0 else 256
    tmf = next(t for t in (512, 256) if n % t == 0)
    tb = tm

    p, qt, rows_t, win_t, gate_t = _norm_proj_t(x2d, norm_mix[0], pw["w_main"], pw["w_tr"], bsz, seq, tm)
    o_h, hg_state = _hgrn_prompt(p, hg_lb_logits, hg_norm[0], bsz, seq, tb)
    cmp = _compress_prompt(p, pw["pe4"], pw["w1bd"], pw["w2bd"])
    nblk = seq // NSA_BLK
    cmp4 = cmp.reshape(bsz, nblk, 2 * NSA_G, NSA_DH).transpose(0, 2, 1, 3)
    kc = cmp4[:, :NSA_G].astype(BF16)
    vct = cmp4[:, NSA_G:].transpose(0, 1, 3, 2).astype(BF16)
    o_n = _nsa_attn_prompt(qt, rows_t, win_t, gate_t, kc, vct, bsz, seq)
    x1 = _out_ffn([o_h, o_n], x2d, [pw["woe_h"], pw["woe_n"]], norm_ffn[0], pw["w13t"][0], pw["w2b"][0], None, tmf)

    rows = rows_t.reshape(bsz, 4, NSA_G, NSA_DH, seq).transpose(0, 4, 1, 2, 3)[None]
    wn = min(NSA_WINDOW, seq)
    win = win_t[:, :, seq - wn:].reshape(bsz, 2, NSA_G, NSA_DH, wn).transpose(0, 4, 1, 2, 3)[None]

    p2 = _norm_proj(x1, norm_mix[1], pw["wio"], tm, ODD_W)
    o_g, gla_state = _gla_prompt(p2, pw["wa2p"], gla_ba[0], gla_norm[0], bsz, seq, tb)
    y = _out_ffn([o_g], x1, [pw["woo"]], norm_ffn[1], pw["w13t"][1], pw["w2b"][1], norm_final, tmf)
    return y.reshape(bsz, seq, d), rows, win, hg_state[None], gla_state[None]


def _forward_sample(x, pw, cache_nsa_kv, cache_win_kv, state_hgrn, state_gla, page_table, hg_lb_logits, hg_norm,
                    gla_ba, gla_norm, norm_mix, norm_ffn, norm_final):
    bsz, seq, d = x.shape
    x2d = x.reshape(bsz, d)
    npages = page_table.shape[1]
    nblk_total = npages * (PAGE // NSA_BLK) + 1
    nsel = min(NSA_TOPK, nblk_total) - 1

    p = _norm_proj(x2d, norm_mix[0], pw["wie"], bsz, PROJ_W // 2)
    o_h, hg_state = _hgrn_decode(p, hg_lb_logits, hg_norm[0], state_hgrn[0])

    cache2 = cache_nsa_kv[0].transpose(0, 2, 3, 4, 1).reshape(cache_nsa_kv.shape[1] * PAGE_ROWS, PAGE)
    q3 = p[:, EV_NQ:EV_CMP].reshape(bsz, NSA_H, NSA_DH)
    ocmp, sel = _dec_cmp(page_table, cache2, pw["pe_dec"], pw["w1dec"], pw["w2dec"], q3, nsel)
    sel = sel.reshape(bsz, NSA_G, 2, npages).transpose(0, 1, 3, 2).reshape(bsz, NSA_G, 2 * npages)
    idx = jnp.argsort(-sel, axis=-1, stable=True)[..., :NSA_TOPK].astype(I32)
    knew = p[:, EV_SLC:EV_GATE].reshape(bsz, 4, NSA_G * NSA_DH)
    wlen = cache_win_kv.shape[2]
    win_t = cache_win_kv[0].transpose(0, 2, 3, 4, 1).reshape(bsz, 2 * NSA_G * NSA_DH, wlen)
    o_n = _dec_attn(page_table, idx, cache2, q3, ocmp, knew, win_t, p[:, EV_GATE:EV_GATE + LANES], nsel)
    o_n = o_n.reshape(bsz, NSA_H * NSA_DH)
    x1 = _out_ffn([o_h, o_n], x2d, [pw["woe_h"], pw["woe_n"]], norm_ffn[0], pw["w13t"][0], pw["w2b"][0], None, bsz)

    rows = p[:, EV_CMP:EV_WIN].reshape(1, bsz, 1, 4, NSA_G, NSA_DH)
    win_new = p[:, EV_WIN:EV_GATE].reshape(1, bsz, 1, 2, NSA_G, NSA_DH)
    wk = jnp.concatenate([cache_win_kv[:1], win_new], axis=2)
    wn = min(NSA_WINDOW, npages * PAGE + 1)
    win = wk[:, :, wk.shape[2] - wn:]

    p2 = _norm_proj(x1, norm_mix[1], pw["wio"], bsz, ODD_W)
    o_g, gla_state = _gla_decode(p2, pw["wa2p"], gla_ba[0], gla_norm[0], state_gla[0])
    y = _out_ffn([o_g], x1, [pw["woo"]], norm_ffn[1], pw["w13t"][1], pw["w2b"][1], norm_final, bsz)
    return y.reshape(bsz, 1, d), rows, win, hg_state[None], gla_state[None]


def kernel(x_prompt, x_sample, cache_nsa_kv, cache_win_kv, state_hgrn, state_gla, page_table, w_in_even,
           hg_lb_logits, hg_norm, nsa_cmp_pos, nsa_cmp_w1, nsa_cmp_w2, w_out_even, w_in_odd, gla_wa2, gla_ba,
           gla_norm, w_out_odd, norm_mix, norm_ffn, norm_final, ffn_w13, ffn_w2):
    pw = _prep_weights(w_in_even, nsa_cmp_pos, nsa_cmp_w1, nsa_cmp_w2, w_out_even, w_in_odd, gla_wa2, w_out_odd,
                       ffn_w13, ffn_w2)
    y_p, kv_p, win_p, hg_p, gla_p = _forward_prompt(x_prompt, pw, hg_lb_logits, hg_norm, gla_ba, gla_norm,
                                                    norm_mix, norm_ffn, norm_final)
    y_s, kv_s, win_s, hg_s, gla_s = _forward_sample(x_sample, pw, cache_nsa_kv, cache_win_kv, state_hgrn,
                                                    state_gla, page_table, hg_lb_logits, hg_norm, gla_ba,
                                                    gla_norm, norm_mix, norm_ffn, norm_final)
    return (y_p, y_s, kv_p, kv_s, win_p, win_s, hg_p, hg_s, gla_p, gla_s)
```

```python
import functools

import jax
import jax.numpy as jnp
from jax import lax
from jax.experimental import pallas as pl
from jax.experimental.pallas import tpu as pltpu

F32 = jnp.float32
BF16 = jnp.bfloat16
I32 = jnp.int32

D_MODEL = 1024
HG_H, HG_DK, HG_DV = 4, 128, 128
NSA_H, NSA_DH, NSA_G, NSA_R = 8, 64, 2, 4
NSA_BLK = 64
NSA_TOPK = 16
NSA_WINDOW = 512
NSA_QB = 128
GLA_H, GLA_DK, GLA_DV = 4, 128, 256
GLA_RANK = 16
GLA_TAU = 16.0
D_FF = 2816
EPS = 1e-6
NEG = -1e30
TINY = 1e-30
PAGE = 128

PROJ_W = 3584
ODD_W = 3200
EV_Q, EV_F, EV_I, EV_OG = 0, 512, 1024, 1536
EV_NQ, EV_CMP, EV_SLC, EV_WIN, EV_GATE = 2048, 2560, 2816, 3072, 3328
OD_Q, OD_K, OD_V, OD_R, OD_A = 0, 512, 1024, 2048, 3072

LANES = 128
V7X_VMEM_LIMIT = 56 * 1024 * 1024
FF_TILE = 1408
CHUNK = 128
SUB = 16

NT_DIMS = (((1,), (1,)), ((), ()))
TN_DIMS = (((0,), (0,)), ((), ()))


def _cparams(sem):
    return pltpu.CompilerParams(dimension_semantics=sem, vmem_limit_bytes=V7X_VMEM_LIMIT)


def _rms(x, g):
    return x * lax.rsqrt(jnp.mean(x * x, axis=-1, keepdims=True) + EPS) * g


def _sigmoid(x):
    return 0.5 * jnp.tanh(0.5 * x) + 0.5


def _silu(x):
    return x * _sigmoid(x)


def _iota(shape, dim):
    return lax.broadcasted_iota(I32, shape, dim)


def _norm_proj_kernel(x_ref, g_ref, w_ref, o_ref, xn_ref):
    @pl.when(pl.program_id(1) == 0)
    def _():
        xn_ref[...] = _rms(x_ref[...], g_ref[...]).astype(BF16)

    o_ref[...] = jnp.dot(xn_ref[...], w_ref[...], preferred_element_type=F32)


def _norm_proj(x2d, gain, w, tm, tn):
    m, k = x2d.shape
    n = w.shape[1]
    return pl.pallas_call(
        _norm_proj_kernel,
        grid=(m // tm, n // tn),
        in_specs=[pl.BlockSpec((tm, k), lambda i, j: (i, 0)),
                  pl.BlockSpec((1, k), lambda i, j: (0, 0)),
                  pl.BlockSpec((k, tn), lambda i, j: (0, j))],
        out_specs=pl.BlockSpec((tm, tn), lambda i, j: (i, j)),
        out_shape=jax.ShapeDtypeStruct((m, n), F32),
        scratch_shapes=[pltpu.VMEM((tm, k), BF16)],
        compiler_params=_cparams(("parallel", "arbitrary")),
        name="norm_proj",
    )(x2d, gain.reshape(1, k), w)


PM_W = 2304
PM_CMP = 2048
T_Q, T_ROWS, T_WIN, T_GATE, T_END = 0, 512, 1024, 1280, 1312


def _norm_proj_t_kernel(x_ref, g_ref, w_ref, wt_ref, o_ref, q_ref, rows_ref, win_ref, gate_ref):
    xn = _rms(x_ref[...], g_ref[...]).astype(BF16)
    o_ref[...] = jnp.dot(xn, w_ref[...], preferred_element_type=F32)
    t = lax.dot_general(wt_ref[...], xn, NT_DIMS, preferred_element_type=F32)
    q_ref[0] = t[T_Q:T_ROWS]
    rows_ref[0] = t[T_ROWS:T_WIN]
    win_ref[0] = t[T_WIN:T_GATE]
    gate_ref[0] = t[T_GATE:T_END]


def _norm_proj_t(x2d, gain, w, wt, bsz, seq, tm):
    k = x2d.shape[1]
    nt = seq // tm
    tr = lambda rows: pl.BlockSpec((1, rows, tm), lambda b, t: (b, 0, t))
    sizes = (T_ROWS - T_Q, T_WIN - T_ROWS, T_GATE - T_WIN, T_END - T_GATE)
    return pl.pallas_call(
        _norm_proj_t_kernel,
        grid=(bsz, nt),
        in_specs=[pl.BlockSpec((tm, k), lambda b, t: (b * nt + t, 0)),
                  pl.BlockSpec((1, k), lambda b, t: (0, 0)),
                  pl.BlockSpec(w.shape, lambda b, t: (0, 0)),
                  pl.BlockSpec(wt.shape, lambda b, t: (0, 0))],
        out_specs=[pl.BlockSpec((tm, PM_W), lambda b, t: (b * nt + t, 0))] + [tr(r) for r in sizes],
        out_shape=[jax.ShapeDtypeStruct((bsz * seq, PM_W), F32)]
        + [jax.ShapeDtypeStruct((bsz, r, seq), F32) for r in sizes],
        compiler_params=_cparams(("parallel", "arbitrary")),
        name="norm_proj_t",
    )(x2d, gain.reshape(1, k), w, wt)


def _out_ffn_kernel(*refs, n_mix, final_norm):
    mix_refs = refs[:n_mix]
    res_ref = refs[n_mix]
    wo_refs = refs[n_mix + 1:2 * n_mix + 1]
    g_ref, w13_ref, w2_ref = refs[2 * n_mix + 1:2 * n_mix + 4]
    pos = 2 * n_mix + 4
    gf_ref = refs[pos] if final_norm else None
    pos += 1 if final_norm else 0
    o_ref, x1_ref, h_ref, acc_ref = refs[pos:pos + 4]
    j = pl.program_id(1)

    @pl.when(j == 0)
    def _():
        x1 = res_ref[...]
        for m_ref, w_ref in zip(mix_refs, wo_refs):
            x1 = x1 + jnp.dot(m_ref[...].astype(BF16), w_ref[...], preferred_element_type=F32)
        x1_ref[...] = x1
        h_ref[...] = _rms(x1, g_ref[...]).astype(BF16)
        acc_ref[...] = jnp.zeros_like(acc_ref)

    gu = jnp.dot(h_ref[...], w13_ref[...], preferred_element_type=F32)
    act = _silu(gu[:, :FF_TILE]) * gu[:, FF_TILE:]
    acc_ref[...] += jnp.dot(act.astype(BF16), w2_ref[...], preferred_element_type=F32)

    @pl.when(j == pl.num_programs(1) - 1)
    def _():
        y = x1_ref[...] + acc_ref[...]
        if final_norm:
            y = _rms(y, gf_ref[...])
        o_ref[...] = y


def _out_ffn(mixes, res, wos, g_ffn, w13t, w2, g_final, tm):
    m, d = res.shape
    n_mix = len(mixes)
    nj = D_FF // FF_TILE
    final_norm = g_final is not None
    in_specs = [pl.BlockSpec((tm, mx.shape[1]), lambda i, j: (i, 0)) for mx in mixes]
    in_specs.append(pl.BlockSpec((tm, d), lambda i, j: (i, 0)))
    in_specs += [pl.BlockSpec(w.shape, lambda i, j: (0, 0)) for w in wos]
    in_specs += [pl.BlockSpec((1, d), lambda i, j: (0, 0)),
                 pl.BlockSpec((d, 2 * FF_TILE), lambda i, j: (0, j)),
                 pl.BlockSpec((FF_TILE, d), lambda i, j: (j, 0))]
    args = list(mixes) + [res] + list(wos) + [g_ffn.reshape(1, d), w13t, w2]
    if final_norm:
        in_specs.append(pl.BlockSpec((1, d), lambda i, j: (0, 0)))
        args.append(g_final.reshape(1, d))
    return pl.pallas_call(
        functools.partial(_out_ffn_kernel, n_mix=n_mix, final_norm=final_norm),
        grid=(m // tm, nj),
        in_specs=in_specs,
        out_specs=pl.BlockSpec((tm, d), lambda i, j: (i, 0)),
        out_shape=jax.ShapeDtypeStruct((m, d), F32),
        scratch_shapes=[pltpu.VMEM((tm, d), F32), pltpu.VMEM((tm, d), BF16), pltpu.VMEM((tm, d), F32)],
        compiler_params=_cparams(("parallel", "arbitrary")),
        name="out_ffn",
    )(*args)


def _gla_chunk(q, k, v, g, st):
    c = q.shape[0]
    tri = (_iota((c, c), 0) >= _iota((c, c), 1)).astype(BF16)
    g_hi = g.astype(BF16)
    g_r1 = g - g_hi.astype(F32)
    g_mid = g_r1.astype(BF16)
    g_lo = (g_r1 - g_mid.astype(F32)).astype(BF16)
    b = (jnp.dot(tri, g_hi, preferred_element_type=F32) + jnp.dot(tri, g_mid, preferred_element_type=F32)
         + jnp.dot(tri, g_lo, preferred_element_type=F32))
    o = lax.dot_general((q * jnp.exp(b)).astype(BF16), st.astype(BF16), NT_DIMS, preferred_element_type=F32)
    lane = _iota((SUB, c), 1)
    row = _iota((SUB, c), 0)
    hs = SUB // 2
    rows = []
    for blk in range(c // SUB):
        lo = blk * SUB
        b_i, q_i, k_i = b[lo:lo + SUB], q[lo:lo + SUB], k[lo:lo + SUB]
        a_top = jnp.zeros((hs, c), F32)
        a_bot = jnp.zeros((hs, c), F32)
        for s in range(SUB):
            if s < hs:
                e = jnp.exp(b_i - b_i[s:s + 1])
                a = jnp.sum(q_i * e * k_i[s:s + 1], axis=1, keepdims=True)
                a_top = jnp.where(lane[:hs] == lo + s, a[:hs], a_top)
                a_bot = jnp.where(lane[:hs] == lo + s, a[hs:], a_bot)
            else:
                e = jnp.exp(b_i[hs:] - b_i[s:s + 1])
                a = jnp.sum(q_i[hs:] * e * k_i[s:s + 1], axis=1, keepdims=True)
                a_bot = jnp.where(lane[:hs] == lo + s, a, a_bot)
        a_blk = jnp.where(lane <= lo + row, jnp.concatenate([a_top, a_bot], axis=0), 0.0)
        if blk > 0:
            r = b[lo - 1:lo]
            qt = q_i * jnp.exp(b_i - r)
            kt = k * jnp.exp(jnp.minimum(r - b, 0.0))
            a_off = lax.dot_general(qt.astype(BF16), kt.astype(BF16), NT_DIMS, preferred_element_type=F32)
            a_blk = jnp.where(lane < lo, a_off, a_blk)
        rows.append(a_blk)
    a_full = jnp.concatenate(rows, axis=0)
    o = o + jnp.dot(a_full.astype(BF16), v.astype(BF16), preferred_element_type=F32)
    b_last = b[c - 1:c]
    kd = k * jnp.exp(b_last - b)
    st_new = st * jnp.exp(b_last) + lax.dot_general(v.astype(BF16), kd.astype(BF16), TN_DIMS,
                                                    preferred_element_type=F32)
    return o, st_new


def _lower_bound(lbl_ref, col):
    l = lbl_ref[:, col:col + HG_DK]
    e = jnp.exp(l - jnp.max(l, axis=0, keepdims=True))
    return e[0:1] / jnp.sum(e, axis=0, keepdims=True)


def _hgrn_kernel(q_ref, f_ref, i_ref, og_ref, lbl_ref, gain_ref, o_ref, s_ref, st_ref, *, nch):
    t = pl.program_id(1)

    @pl.when(t == 0)
    def _():
        st_ref[...] = jnp.zeros_like(st_ref)

    def body(ci, carry):
        r0 = pl.multiple_of(ci * CHUNK, CHUNK)
        rs = pl.ds(r0, CHUNK)
        for h in range(HG_H):
            cs = slice(h * HG_DK, (h + 1) * HG_DK)
            lb = _lower_bound(lbl_ref, h * HG_DK)
            q = _silu(q_ref[rs, cs])
            f = lb + (1.0 - lb) * _sigmoid(f_ref[rs, cs])
            o, st_new = _gla_chunk(q, 1.0 - f, i_ref[rs, cs], jnp.log(f), st_ref[h])
            st_ref[h] = st_new
            o = _rms(o, gain_ref[...]) * _sigmoid(og_ref[rs, cs])
            o_ref[rs, cs] = o.astype(o_ref.dtype)
        return carry

    lax.fori_loop(0, nch, body, 0)

    @pl.when(t == pl.num_programs(1) - 1)
    def _():
        for h in range(HG_H):
            s_ref[0, h] = st_ref[h].T


def _hgrn_prompt(p2d, lb_logits, gain, bsz, seq, tb):
    nt = seq // tb
    hk = HG_H * HG_DK
    row = lambda b, t: b * nt + t
    col_spec = lambda c: pl.BlockSpec((tb, hk), lambda b, t, c=c: (row(b, t), c))
    return pl.pallas_call(
        functools.partial(_hgrn_kernel, nch=tb // CHUNK),
        grid=(bsz, nt),
        in_specs=[col_spec(EV_Q // hk), col_spec(EV_F // hk), col_spec(EV_I // hk), col_spec(EV_OG // hk),
                  pl.BlockSpec(lb_logits.shape, lambda b, t: (0, 0)),
                  pl.BlockSpec((1, HG_DV), lambda b, t: (0, 0))],
        out_specs=[pl.BlockSpec((tb, hk), lambda b, t: (row(b, t), 0)),
                   pl.BlockSpec((1, HG_H, HG_DK, HG_DV), lambda b, t: (b, 0, 0, 0))],
        out_shape=[jax.ShapeDtypeStruct((bsz * seq, hk), BF16),
                   jax.ShapeDtypeStruct((bsz, HG_H, HG_DK, HG_DV), F32)],
        scratch_shapes=[pltpu.VMEM((HG_H, HG_DV, HG_DK), F32)],
        compiler_params=_cparams(("parallel", "arbitrary")),
        name="hgrn_chunk",
    )(p2d, p2d, p2d, p2d, lb_logits, gain.reshape(1, HG_DV))


def _log_sigmoid(x):
    return jnp.minimum(x, 0.0) - jnp.log(1.0 + jnp.exp(-jnp.abs(x)))


def _gla_kernel(q_ref, k_ref, v_ref, r_ref, a_ref, wa_ref, ba_ref, gain_ref, o_ref, s_ref, st_ref, *, nch):
    t = pl.program_id(1)

    @pl.when(t == 0)
    def _():
        st_ref[...] = jnp.zeros_like(st_ref)

    def body(ci, carry):
        r0 = pl.multiple_of(ci * CHUNK, CHUNK)
        rs = pl.ds(r0, CHUNK)
        gate = jnp.dot(a_ref[rs, :].astype(BF16), wa_ref[...], preferred_element_type=F32) + ba_ref[...]
        logf = _log_sigmoid(gate) * (1.0 / GLA_TAU)
        for h in range(GLA_H):
            ks = slice(h * GLA_DK, (h + 1) * GLA_DK)
            vs = slice(h * GLA_DV, (h + 1) * GLA_DV)
            q = q_ref[rs, ks] * (GLA_DK ** -0.5)
            o, st_new = _gla_chunk(q, k_ref[rs, ks], v_ref[rs, vs], logf[:, ks], st_ref[h])
            st_ref[h] = st_new
            o = _rms(o, gain_ref[...]) * _silu(r_ref[rs, vs])
            o_ref[rs, vs] = o.astype(o_ref.dtype)
        return carry

    lax.fori_loop(0, nch, body, 0)

    @pl.when(t == pl.num_programs(1) - 1)
    def _():
        for h in range(GLA_H):
            s_ref[0, h] = st_ref[h].T


def _gla_prompt(p2d, wa2p, ba, gain, bsz, seq, tb):
    nt = seq // tb
    hk, hv = GLA_H * GLA_DK, GLA_H * GLA_DV
    row = lambda b, t: b * nt + t
    return pl.pallas_call(
        functools.partial(_gla_kernel, nch=tb // CHUNK),
        grid=(bsz, nt),
        in_specs=[pl.BlockSpec((tb, hk), lambda b, t: (row(b, t), OD_Q // hk)),
                  pl.BlockSpec((tb, hk), lambda b, t: (row(b, t), OD_K // hk)),
                  pl.BlockSpec((tb, hv), lambda b, t: (row(b, t), OD_V // hv)),
                  pl.BlockSpec((tb, hv), lambda b, t: (row(b, t), OD_R // hv)),
                  pl.BlockSpec((tb, LANES), lambda b, t: (row(b, t), OD_A // LANES)),
                  pl.BlockSpec(wa2p.shape, lambda b, t: (0, 0)),
                  pl.BlockSpec((1, hk), lambda b, t: (0, 0)),
                  pl.BlockSpec((1, GLA_DV), lambda b, t: (0, 0))],
        out_specs=[pl.BlockSpec((tb, hv), lambda b, t: (row(b, t), 0)),
                   pl.BlockSpec((1, GLA_H, GLA_DK, GLA_DV), lambda b, t: (b, 0, 0, 0))],
        out_shape=[jax.ShapeDtypeStruct((bsz * seq, hv), BF16),
                   jax.ShapeDtypeStruct((bsz, GLA_H, GLA_DK, GLA_DV), F32)],
        scratch_shapes=[pltpu.VMEM((GLA_H, GLA_DV, GLA_DK), F32)],
        compiler_params=_cparams(("parallel", "arbitrary")),
        name="gla_chunk",
    )(p2d, p2d, p2d, p2d, p2d, wa2p, ba.reshape(1, hk), gain.reshape(1, GLA_DV))


def _to_columns(x):
    bsz = x.shape[0]
    if bsz < LANES:
        x = jnp.concatenate([x, jnp.zeros((LANES - bsz, x.shape[1]), x.dtype)], axis=0)
    return x.T


def _decode_update(q, k, v, g, s_ref, so_ref, o_scr):
    bsz = q.shape[0]
    qt, kt, et = _to_columns(q), _to_columns(k), _to_columns(jnp.exp(g))
    for b in range(bsz):
        s_new = et[:, b:b + 1] * s_ref[b, 0] + kt[:, b:b + 1] * v[b:b + 1, :]
        so_ref[b, 0] = s_new
        o_scr[b:b + 1, :] = jnp.sum(qt[:, b:b + 1] * s_new, axis=0, keepdims=True)


def _hgrn_decode_kernel(q_ref, f_ref, i_ref, og_ref, lbl_ref, gain_ref, s_ref, o_ref, so_ref, o_scr):
    l = lbl_ref[0]
    e = jnp.exp(l - jnp.max(l, axis=0, keepdims=True))
    lb = e[0:1] / jnp.sum(e, axis=0, keepdims=True)
    f = lb + (1.0 - lb) * _sigmoid(f_ref[...])
    _decode_update(_silu(q_ref[...]), 1.0 - f, i_ref[...], jnp.log(f), s_ref, so_ref, o_scr)
    o_ref[...] = _rms(o_scr[...], gain_ref[...]) * _sigmoid(og_ref[...])


def _hgrn_decode(p2d, lb_logits, gain, state):
    bsz = p2d.shape[0]
    col = lambda c: pl.BlockSpec((bsz, HG_DK), lambda h, c=c: (0, c + h))
    lbl3 = lb_logits.reshape(lb_logits.shape[0], HG_H, HG_DK).transpose(1, 0, 2)
    st_spec = pl.BlockSpec((bsz, 1, HG_DK, HG_DV), lambda h: (0, h, 0, 0))
    return pl.pallas_call(
        _hgrn_decode_kernel,
        grid=(HG_H,),
        in_specs=[col(EV_Q // HG_DK), col(EV_F // HG_DK), col(EV_I // HG_DK), col(EV_OG // HG_DK),
                  pl.BlockSpec((1,) + lbl3.shape[1:], lambda h: (h, 0, 0)),
                  pl.BlockSpec((1, HG_DV), lambda h: (0, 0)),
                  st_spec],
        out_specs=[pl.BlockSpec((bsz, HG_DV), lambda h: (0, h)), st_spec],
        out_shape=[jax.ShapeDtypeStruct((bsz, HG_H * HG_DV), F32),
                   jax.ShapeDtypeStruct(state.shape, F32)],
        scratch_shapes=[pltpu.VMEM((bsz, HG_DV), F32)],
        compiler_params=_cparams(("arbitrary",)),
        name="hgrn_decode",
    )(p2d, p2d, p2d, p2d, lbl3, gain.reshape(1, HG_DV), state)


def _gla_decode_kernel(q_ref, k_ref, v_ref, r_ref, a_ref, wa_ref, ba_ref, gain_ref, s_ref, o_ref, so_ref, o_scr):
    gate = jnp.dot(a_ref[...].astype(BF16), wa_ref[...], preferred_element_type=F32) + ba_ref[...]
    logf = _log_sigmoid(gate) * (1.0 / GLA_TAU)
    _decode_update(q_ref[...] * (GLA_DK ** -0.5), k_ref[...], v_ref[...], logf, s_ref, so_ref, o_scr)
    o_ref[...] = _rms(o_scr[...], gain_ref[...]) * _silu(r_ref[...])


def _gla_decode(p2d, wa2p, ba, gain, state):
    bsz = p2d.shape[0]
    hk = GLA_H * GLA_DK
    st_spec = pl.BlockSpec((bsz, 1, GLA_DK, GLA_DV), lambda h: (0, h, 0, 0))
    return pl.pallas_call(
        _gla_decode_kernel,
        grid=(GLA_H,),
        in_specs=[pl.BlockSpec((bsz, GLA_DK), lambda h: (0, OD_Q // GLA_DK + h)),
                  pl.BlockSpec((bsz, GLA_DK), lambda h: (0, OD_K // GLA_DK + h)),
                  pl.BlockSpec((bsz, GLA_DV), lambda h: (0, OD_V // GLA_DV + h)),
                  pl.BlockSpec((bsz, GLA_DV), lambda h: (0, OD_R // GLA_DV + h)),
                  pl.BlockSpec((bsz, LANES), lambda h: (0, OD_A // LANES)),
                  pl.BlockSpec((LANES, GLA_DK), lambda h: (0, h)),
                  pl.BlockSpec((1, GLA_DK), lambda h: (0, h)),
                  pl.BlockSpec((1, GLA_DV), lambda h: (0, 0)),
                  st_spec],
        out_specs=[pl.BlockSpec((bsz, GLA_DV), lambda h: (0, h)), st_spec],
        out_shape=[jax.ShapeDtypeStruct((bsz, GLA_H * GLA_DV), F32),
                   jax.ShapeDtypeStruct(state.shape, F32)],
        scratch_shapes=[pltpu.VMEM((bsz, GLA_DV), F32)],
        compiler_params=_cparams(("arbitrary",)),
        name="gla_decode",
    )(p2d, p2d, p2d, p2d, p2d, wa2p, ba.reshape(1, hk), gain.reshape(1, GLA_DV), state)


def _compress_kernel(xa_ref, xb_ref, pe_ref, w1_ref, w2_ref, o_ref, acc_ref):
    nb = o_ref.shape[0]
    acc_ref[...] = jnp.zeros_like(acc_ref)
    for l in range(NSA_BLK):
        rows = pl.ds(l, nb, stride=NSA_BLK)
        x = jnp.concatenate([xa_ref[rows, :], xb_ref[rows, :]], axis=1) + pe_ref[l:l + 1, :]
        acc_ref[...] += jnp.dot(x.astype(BF16), w1_ref[l], preferred_element_type=F32)
    o_ref[...] = jnp.dot(_silu(acc_ref[...]).astype(BF16), w2_ref[...], preferred_element_type=F32)


def _compress_prompt(p2d, pe4, w1bd, w2bd):
    n = p2d.shape[0]
    rc = min(8192, n)
    cw = 4 * NSA_DH
    c0 = PM_CMP // LANES
    return pl.pallas_call(
        _compress_kernel,
        grid=(n // rc,),
        in_specs=[pl.BlockSpec((rc, LANES), lambda i: (i, c0)),
                  pl.BlockSpec((rc, LANES), lambda i: (i, c0 + 1)),
                  pl.BlockSpec(pe4.shape, lambda i: (0, 0)),
                  pl.BlockSpec(w1bd.shape, lambda i: (0, 0, 0)),
                  pl.BlockSpec((cw, cw), lambda i: (0, 0))],
        out_specs=pl.BlockSpec((rc // NSA_BLK, cw), lambda i: (i, 0)),
        out_shape=jax.ShapeDtypeStruct((n // NSA_BLK, cw), F32),
        scratch_shapes=[pltpu.VMEM((rc // NSA_BLK, cw), F32)],
        compiler_params=_cparams(("arbitrary",)),
        name="nsa_compress",
    )(p2d, p2d, pe4, w1bd, w2bd)


AQB = 256
AQB_SHIFT = 8
AUG = 128
ONES_ROWS = 16


def _slopes_lane(g, lanes):
    r = _iota((1, lanes), 1) >> AQB_SHIFT
    out = jnp.zeros((1, lanes), F32)
    for rr in range(NSA_R):
        out = jnp.where(r == rr, 2.0 ** (-(g * NSA_R + rr + 1)), out)
    return out


def _build_key_features(src_ref, row0, dst_ref, g, seq, with_onehot):
    r = _iota((NSA_DH, LANES), 0)
    for cb in range(seq // LANES):
        kpos = cb * LANES + _iota((NSA_DH, LANES), 1)
        feat = jnp.where(r == 32, (kpos >> 6).astype(F32),
                         jnp.where(r == 33, (kpos & 63).astype(F32),
                                   jnp.where((r == 34) | (r == 35), 1.0, 0.0)))
        if with_onehot:
            feat = jnp.where((kpos >> 6) == r, 1.0, feat)
        kt = src_ref[0, row0:row0 + NSA_DH, cb * LANES:(cb + 1) * LANES]
        dst_ref[g, cb * LANES:(cb + 1) * LANES, :] = jnp.concatenate([kt, feat], axis=0).T.astype(BF16)


def _attn_tiles(branches, k0, qa_ref, mask, m_ref, acc_ref, nkeys=AQB):
    jobs = [(k_ref, v_ref, v_row0 + g * NSA_DH, stream0 + g, g)
            for k_ref, v_ref, v_row0, stream0 in branches for g in range(NSA_G)]
    s = [jnp.dot(k_ref[g, pl.ds(k0, nkeys), :], qa_ref[g], preferred_element_type=F32)
         for k_ref, _, _, _, g in jobs]
    if mask is not None:
        s = [jnp.where(mask, sj, NEG) for sj in s]
    m_old = [m_ref[st] for _, _, _, st, _ in jobs]
    m_new = [jnp.maximum(mo, jnp.max(sj, axis=0, keepdims=True)) for mo, sj in zip(m_old, s)]
    p = [jnp.exp(sj - mn).astype(BF16) for sj, mn in zip(s, m_new)]
    ones = jnp.ones((ONES_ROWS, nkeys), BF16)
    for i, (_, v_ref, v_row, st, _) in enumerate(jobs):
        vt = v_ref[0, v_row:v_row + NSA_DH, pl.ds(k0, nkeys)].astype(BF16)
        pv = jnp.dot(jnp.concatenate([vt, ones], axis=0), p[i], preferred_element_type=F32)
        acc_ref[st] = jnp.exp(m_old[i] - m_new[i]) * acc_ref[st] + pv
        m_ref[st] = m_new[i]


def _nsa_attn_kernel(qt_ref, rows_ref, win_ref, gt_ref, kc_ref, vct_ref, o_ref, ks_s, kw_s, qa_s, m_s, acc_s,
                     *, nblk):
    qb = pl.program_id(1)
    seq = rows_ref.shape[2]
    q0 = qb * AQB
    nq = NSA_R * AQB
    half = NSA_G * NSA_DH

    @pl.when(qb == 0)
    def _():
        for g in range(NSA_G):
            _build_key_features(rows_ref, 2 * half + g * NSA_DH, ks_s, g, seq, True)
            _build_key_features(win_ref, g * NSA_DH, kw_s, g, seq, False)

    pq = q0 + (_iota((1, nq), 1) & (AQB - 1))
    gates = _sigmoid(gt_ref[0])
    ksel = min(NSA_TOPK, nblk)
    scale = NSA_DH ** -0.5
    key_i = _iota((AQB, nq), 0)
    qry_t = _iota((AQB, nq), 1) & (AQB - 1)
    causal = key_i <= qry_t
    beyond = key_i > qry_t
    pq_f = q0 + _iota((32, AQB), 1)
    feat_row = _iota((32, AQB), 0)

    m_s[...] = jnp.full(m_s.shape, NEG, F32)
    acc_s[...] = jnp.zeros_like(acc_s)

    def result(i):
        return acc_s[i, 0:NSA_DH, :] / jnp.maximum(acc_s[i, NSA_DH:NSA_DH + 1, :], TINY)

    o_cmp = []
    for g in range(NSA_G):
        slope = _slopes_lane(g, nq)

        qt_g = [qt_ref[0, (g * NSA_R + r) * NSA_DH:(g * NSA_R + r + 1) * NSA_DH, :] * scale for r in range(NSA_R)]
        qs_t = jnp.concatenate(qt_g, axis=1).astype(BF16)
        sc = jnp.dot(kc_ref[0, g], qs_t, preferred_element_type=F32)
        dist_c = pq - (_iota((nblk, nq), 0) * NSA_BLK + NSA_BLK - 1)
        mask_c = dist_c >= 0
        sc = jnp.where(mask_c, sc - slope * dist_c.astype(F32), NEG)
        pc = jnp.where(mask_c, jnp.exp(sc - jnp.max(sc, axis=0, keepdims=True)), 0.0)
        pc = pc / jnp.maximum(jnp.sum(pc, axis=0, keepdims=True), TINY)
        o_cmp.append(jnp.dot(vct_ref[0, g], pc.astype(BF16), preferred_element_type=F32))
        imp = pc[:, 0:AQB]
        for r in range(1, NSA_R):
            imp = imp + pc[:, r * AQB:(r + 1) * AQB]

        n_io = _iota((nblk, AQB), 0)
        tpos = q0 + _iota((nblk, AQB), 1)
        forced = (n_io == (tpos >> 6)) | (n_io == 0)
        started = n_io * NSA_BLK <= tpos
        score = jnp.where(forced, jnp.inf, jnp.where(started, imp, -jnp.inf))
        rank = jnp.zeros((nblk, AQB), I32)
        for i in range(nblk):
            row = score[i:i + 1, :]
            rank = rank + ((row > score) | ((row == score) & (i < n_io))).astype(I32)
        sel_bias = jnp.where((rank < ksel) & started, 0.0, NEG)
        if nblk < 32:
            sel_bias = jnp.concatenate([sel_bias, jnp.zeros((32 - nblk, AQB), F32)], axis=0)

        cols = []
        for r in range(NSA_R):
            sl = 2.0 ** (-(g * NSA_R + r + 1))
            pos_feat = jnp.where(feat_row == 0, 64.0 * sl,
                                 jnp.where(feat_row == 1, sl,
                                           jnp.where(feat_row == 2, (-64.0 * sl) * (pq_f >> 6).astype(F32),
                                                     jnp.where(feat_row == 3, (-sl) * (pq_f & 63).astype(F32), 0.0))))
            cols.append(jnp.concatenate([qt_g[r], sel_bias, pos_feat], axis=0))
        qa_s[g] = jnp.concatenate(cols, axis=1).astype(BF16)

    slc_branch = (ks_s, rows_ref, 3 * half, 0)
    win_branch = (kw_s, win_ref, half, NSA_G)

    def slc_tile(k0, mask, nkeys=AQB):
        _attn_tiles([slc_branch], k0, qa_s, mask, m_s, acc_s, nkeys)

    def win_tile(k0, mask, nkeys=AQB):
        _attn_tiles([win_branch], k0, qa_s, mask, m_s, acc_s, nkeys)

    def slc_body(kt, c):
        slc_tile(pl.multiple_of(kt * 2 * AQB, 2 * AQB), None, 2 * AQB)
        return c

    lax.fori_loop(0, qb >> 1, slc_body, 0)

    @pl.when((qb & 1) == 1)
    def _():
        slc_tile(pl.multiple_of(q0 - AQB, AQB), None)

    @pl.when(qb >= 2)
    def _():
        win_tile(pl.multiple_of(q0 - 2 * AQB, AQB), jnp.concatenate([beyond, jnp.ones_like(beyond)], axis=0), 2 * AQB)

    @pl.when(qb == 1)
    def _():
        win_tile(0, None)

    _attn_tiles([slc_branch, win_branch], pl.multiple_of(q0, AQB), qa_s, causal, m_s, acc_s)

    for g in range(NSA_G):
        def gate_row(j, g=g):
            return jnp.concatenate([gates[g * 12 + r * 3 + j:g * 12 + r * 3 + j + 1, :] for r in range(NSA_R)],
                                   axis=1)

        o_t = gate_row(0) * o_cmp[g] + gate_row(1) * result(g) + gate_row(2) * result(NSA_G + g)
        o_st = jnp.concatenate([o_t[:, r * AQB:(r + 1) * AQB] for r in range(NSA_R)], axis=0)
        o_ref[:, g * NSA_R * NSA_DH:(g + 1) * NSA_R * NSA_DH] = o_st.T.astype(o_ref.dtype)


def _nsa_attn_prompt(qt, rows_t, win_t, gate_t, kc, vct, bsz, seq):
    assert NSA_WINDOW == 2 * AQB and seq % AQB == 0 and seq // NSA_BLK <= 32
    nqb = seq // AQB
    nblk = seq // NSA_BLK
    qw = NSA_H * NSA_DH
    nq = NSA_R * AQB
    per_b = lambda a: pl.BlockSpec((1,) + a.shape[1:], lambda b, i: (b,) + (0,) * (a.ndim - 1))
    per_q = lambda a: pl.BlockSpec((1, a.shape[1], AQB), lambda b, i: (b, 0, i))
    return pl.pallas_call(
        functools.partial(_nsa_attn_kernel, nblk=nblk),
        grid=(bsz, nqb),
        in_specs=[per_q(qt), per_b(rows_t), per_b(win_t), per_q(gate_t), per_b(kc), per_b(vct)],
        out_specs=pl.BlockSpec((AQB, qw), lambda b, i: (b * nqb + i, 0)),
        out_shape=jax.ShapeDtypeStruct((bsz * seq, qw), BF16),
        scratch_shapes=[pltpu.VMEM((NSA_G, seq, AUG), BF16), pltpu.VMEM((NSA_G, seq, AUG), BF16),
                        pltpu.VMEM((NSA_G, AUG, nq), BF16),
                        pltpu.VMEM((2 * NSA_G, 1, nq), F32), pltpu.VMEM((2 * NSA_G, NSA_DH + ONES_ROWS, nq), F32)],
        compiler_params=_cparams(("parallel", "arbitrary")),
        name="nsa_attn",
    )(qt, rows_t, win_t, gate_t, kc, vct)


QROWS = 16


def _head_slopes(rows):
    r = _iota((rows, 1), 0)
    out = jnp.zeros((rows, 1), F32)
    for h in range(NSA_H):
        out = jnp.where(r == h, 2.0 ** (-(h + 1)), out)
    return out


def _softmax_rows(s, valid):
    s = jnp.where(valid, s, NEG)
    p = jnp.where(valid, jnp.exp(s - jnp.max(s, axis=1, keepdims=True)), 0.0)
    return p / jnp.maximum(jnp.sum(p, axis=1, keepdims=True), TINY)


def _padded_queries(q_ref):
    q = q_ref[0] * (NSA_DH ** -0.5)
    return jnp.concatenate([q, jnp.zeros((QROWS - NSA_H, NSA_DH), F32)], axis=0).astype(BF16)


PLANES = 4 * NSA_G
PAGE_ROWS = PLANES * NSA_DH
CMP_PLANES = 2 * NSA_G


def _dec_cmp_kernel(pt_ref, cache_ref, pe_ref, w1_ref, w2_ref, q_ref, o_ref, sel_ref, buf, acc_ref, cmp_ref, sem,
                    *, npages, nsel):
    b = pl.program_id(0)
    nb = npages * (PAGE // NSA_BLK)
    nrow = npages * CMP_PLANES
    crows = CMP_PLANES * NSA_DH

    def page_copy(pg):
        src = pl.multiple_of(pt_ref[b * npages + pg] * PAGE_ROWS, PAGE_ROWS)
        return pltpu.make_async_copy(cache_ref.at[pl.ds(src, crows), :],
                                     buf.at[pl.ds(pl.multiple_of(pg * crows, crows), crows), :], sem.at[0])

    def start(pg, c):
        page_copy(pg).start()
        return c

    def wait(pg, c):
        page_copy(pg).wait()
        return c

    lax.fori_loop(0, npages, start, 0)
    lax.fori_loop(0, npages, wait, 0)

    is_k = ((_iota((nrow, 1), 0) >> 1) & 1) == 0
    acc_ref[...] = jnp.zeros_like(acc_ref)
    for d in range(NSA_DH):
        x = buf[pl.ds(d, nrow, stride=NSA_DH), :]
        x = x + jnp.where(is_k, pe_ref[d, 0:1, :], pe_ref[d, 1:2, :])
        acc_ref[...] += jnp.dot(x.astype(BF16), w1_ref[d], preferred_element_type=F32)
    acc = acc_ref[...]
    h = _silu(jnp.where(is_k, acc[:, :LANES], acc[:, LANES:]))
    c2 = jnp.dot(h.astype(BF16), w2_ref[...], preferred_element_type=F32)
    cmp_ref[...] = jnp.where(is_k, c2[:, :LANES], c2[:, LANES:])

    pos = nb * NSA_BLK
    q16 = _padded_queries(q_ref)
    slope = _head_slopes(QROWS)
    row_grp = _iota((QROWS, 1), 0) >> 2
    lane = _iota((1, nb), 1)
    blk_of = jnp.where(lane < npages, 2 * lane, 2 * (lane - npages) + 1)
    dist = (pos - (blk_of * NSA_BLK + NSA_BLK - 1)).astype(F32)
    ri, ci = _iota((nb, nb), 0), _iota((nb, nb), 1)
    blk_r = jnp.where(ri < npages, 2 * ri, 2 * (ri - npages) + 1)
    blk_c = jnp.where(ci < npages, 2 * ci, 2 * (ci - npages) + 1)
    o_all = jnp.zeros((QROWS, NSA_DH), F32)
    for g in range(NSA_G):
        kc = cmp_ref[pl.ds(g, npages, stride=CMP_PLANES), :].astype(BF16)
        vc = cmp_ref[pl.ds(NSA_G + g, npages, stride=CMP_PLANES), :].astype(BF16)
        s = jnp.concatenate([lax.dot_general(q16, kc[:, j * NSA_DH:(j + 1) * NSA_DH], NT_DIMS,
                                             preferred_element_type=F32) for j in range(2)], axis=1)
        p = _softmax_rows(s - slope * dist, dist >= 0)
        pb = p.astype(BF16)
        o_g = sum(jnp.dot(pb[:, j * npages:(j + 1) * npages], vc[:, j * NSA_DH:(j + 1) * NSA_DH],
                          preferred_element_type=F32) for j in range(2))
        o_all = jnp.where(row_grp == g, o_g, o_all)
        imp = jnp.sum(jnp.where(row_grp == g, p, 0.0), axis=0, keepdims=True)
        score_row = jnp.where(blk_of == 0, jnp.inf, imp)
        score_col = jnp.sum(jnp.where(ri == ci, jnp.broadcast_to(score_row, (nb, nb)), 0.0),
                            axis=1, keepdims=True)
        beats = (score_col > score_row) | ((score_col == score_row) & (blk_r < blk_c))
        rank = jnp.sum(beats.astype(I32), axis=0, keepdims=True)
        sel_ref[0, g:g + 1, :] = (rank < nsel).astype(F32)
    o_ref[0] = o_all[:NSA_H]


def _dec_cmp(page_table, cache2, pe_dec, w1dec, w2dec, q3, nsel):
    bsz, npages = page_table.shape
    nb = npages * (PAGE // NSA_BLK)
    nrow = npages * CMP_PLANES
    grid_spec = pltpu.PrefetchScalarGridSpec(
        num_scalar_prefetch=1,
        grid=(bsz,),
        in_specs=[pl.BlockSpec(memory_space=pl.ANY),
                  pl.BlockSpec(pe_dec.shape, lambda b, pt: (0, 0, 0)),
                  pl.BlockSpec(w1dec.shape, lambda b, pt: (0, 0, 0)),
                  pl.BlockSpec(w2dec.shape, lambda b, pt: (0, 0)),
                  pl.BlockSpec((1, NSA_H, NSA_DH), lambda b, pt: (b, 0, 0))],
        out_specs=[pl.BlockSpec((1, NSA_H, NSA_DH), lambda b, pt: (b, 0, 0)),
                   pl.BlockSpec((1, NSA_G, nb), lambda b, pt: (b, 0, 0))],
        scratch_shapes=[pltpu.VMEM((nrow * NSA_DH, LANES), F32), pltpu.VMEM((nrow, 2 * LANES), F32),
                        pltpu.VMEM((nrow, LANES), F32), pltpu.SemaphoreType.DMA((1,))])
    return pl.pallas_call(
        functools.partial(_dec_cmp_kernel, npages=npages, nsel=nsel),
        grid_spec=grid_spec,
        out_shape=[jax.ShapeDtypeStruct((bsz, NSA_H, NSA_DH), F32),
                   jax.ShapeDtypeStruct((bsz, NSA_G, nb), F32)],
        compiler_params=_cparams(("arbitrary",)),
        name="nsa_dec_cmp",
    )(page_table.reshape(-1), cache2, pe_dec, w1dec, w2dec, q3)


def _attend_with_self(q16, slope, kt, vt, dist, valid, k_self, v_self):
    s = jnp.dot(q16, kt.astype(BF16), preferred_element_type=F32) - slope * dist
    s_self = jnp.sum(q16.astype(F32) * k_self.astype(BF16).astype(F32), axis=1, keepdims=True)
    m = jnp.maximum(jnp.max(jnp.where(valid, s, NEG), axis=1, keepdims=True), s_self)
    p = jnp.where(valid, jnp.exp(s - m), 0.0)
    p_self = jnp.exp(s_self - m)
    num = lax.dot_general(p.astype(BF16), vt.astype(BF16), NT_DIMS, preferred_element_type=F32) + p_self * v_self
    return num / jnp.maximum(jnp.sum(p, axis=1, keepdims=True) + p_self, TINY)


def _dec_attn_kernel(pt_ref, idx_ref, cache_ref, q_ref, oc_ref, kn_ref, win_ref, gt_ref, o_ref, kbuf, vbuf, sem,
                     *, npages, nsel, wlen):
    b = pl.program_id(0)
    pos = npages * PAGE

    copies = []
    for g in range(NSA_G):
        for j in range(nsel):
            blk = idx_ref[(b * NSA_G + g) * NSA_TOPK + j]
            base = pt_ref[b * npages + (blk >> 1)] * PAGE_ROWS
            dst = pl.ds(j * PAGE, PAGE)
            k_rows = pl.ds(pl.multiple_of(base + (2 * NSA_G + g) * NSA_DH, NSA_DH), NSA_DH)
            v_rows = pl.ds(pl.multiple_of(base + (3 * NSA_G + g) * NSA_DH, NSA_DH), NSA_DH)
            copies.append(pltpu.make_async_copy(cache_ref.at[k_rows, :], kbuf.at[g, :, dst], sem.at[0]))
            copies.append(pltpu.make_async_copy(cache_ref.at[v_rows, :], vbuf.at[g, :, dst], sem.at[0]))
    for i, cp in enumerate(copies):
        cp.start(priority=i % 2)
    for cp in copies:
        cp.wait()

    q16 = _padded_queries(q_ref)
    slope = _head_slopes(QROWS)
    row_grp = _iota((QROWS, 1), 0) >> 2
    kn = kn_ref[0]
    lane = _iota((1, nsel * PAGE), 1)
    within = lane & (PAGE - 1)
    wdist = wlen - _iota((1, wlen), 1)
    o_slc = jnp.zeros((QROWS, NSA_DH), F32)
    o_win = jnp.zeros((QROWS, NSA_DH), F32)
    for g in range(NSA_G):
        gs = slice(g * NSA_DH, (g + 1) * NSA_DH)
        blk_lane = jnp.zeros((1, nsel * PAGE), I32)
        for j in range(nsel):
            blk_lane = jnp.where((lane >> 7) == j, idx_ref[(b * NSA_G + g) * NSA_TOPK + j], blk_lane)
        valid = (within >> 6) == (blk_lane & 1)
        kpos = (blk_lane >> 1) * PAGE + within
        o_g = _attend_with_self(q16, slope, kbuf[g], vbuf[g], (pos - kpos).astype(F32), valid,
                                kn[0:1, gs], kn[1:2, gs])
        o_slc = jnp.where(row_grp == g, o_g, o_slc)
        o_g = _attend_with_self(q16, slope, win_ref[0, g * NSA_DH:(g + 1) * NSA_DH, :],
                                win_ref[0, (NSA_G + g) * NSA_DH:(NSA_G + g + 1) * NSA_DH, :],
                                wdist.astype(F32), wdist < NSA_WINDOW, kn[2:3, gs], kn[3:4, gs])
        o_win = jnp.where(row_grp == g, o_g, o_win)

    gates = _sigmoid(gt_ref[pl.ds(b, 1), :])
    r_io = _iota((QROWS, LANES), 0)
    l_io = _iota((QROWS, LANES), 1)

    def gate_col(j):
        return jnp.sum(jnp.where(l_io == 3 * r_io + j, gates, 0.0), axis=1, keepdims=True)

    o_cmp = jnp.concatenate([oc_ref[0], jnp.zeros((QROWS - NSA_H, NSA_DH), F32)], axis=0)
    o = gate_col(0) * o_cmp + gate_col(1) * o_slc + gate_col(2) * o_win
    o_ref[0] = o[:NSA_H]


def _dec_attn(page_table, idx, cache2, q3, ocmp, knew, win_t, gates, nsel):
    bsz, npages = page_table.shape
    wlen = win_t.shape[2]
    hd = pl.BlockSpec((1, NSA_H, NSA_DH), lambda b, pt, ix: (b, 0, 0))
    grid_spec = pltpu.PrefetchScalarGridSpec(
        num_scalar_prefetch=2,
        grid=(bsz,),
        in_specs=[pl.BlockSpec(memory_space=pl.ANY), hd, hd,
                  pl.BlockSpec((1,) + knew.shape[1:], lambda b, pt, ix: (b, 0, 0)),
                  pl.BlockSpec((1,) + win_t.shape[1:], lambda b, pt, ix: (b, 0, 0)),
                  pl.BlockSpec(gates.shape, lambda b, pt, ix: (0, 0))],
        out_specs=hd,
        scratch_shapes=[pltpu.VMEM((NSA_G, NSA_DH, nsel * PAGE), F32), pltpu.VMEM((NSA_G, NSA_DH, nsel * PAGE), F32),
                        pltpu.SemaphoreType.DMA((1,))])
    return pl.pallas_call(
        functools.partial(_dec_attn_kernel, npages=npages, nsel=nsel, wlen=wlen),
        grid_spec=grid_spec,
        out_shape=jax.ShapeDtypeStruct((bsz, NSA_H, NSA_DH), F32),
        compiler_params=_cparams(("arbitrary",)),
        name="nsa_dec_attn",
    )(page_table.reshape(-1), idx.reshape(-1), cache2, q3, ocmp, knew, win_t, gates)


def _prep_weights(w_in_even, nsa_cmp_pos, nsa_cmp_w1, nsa_cmp_w2, w_out_even, w_in_odd, gla_wa2, w_out_odd,
                  ffn_w13, ffn_w2):
    d = D_MODEL
    wie = jnp.pad(w_in_even[0], ((0, 0), (0, PROJ_W - w_in_even.shape[2]))).astype(BF16)
    a0 = 2 * GLA_H * GLA_DK + GLA_H * GLA_DV
    wo = w_in_odd[0]
    wio = jnp.concatenate([wo[:, :a0], wo[:, a0 + GLA_RANK:], wo[:, a0:a0 + GLA_RANK]], axis=1)
    wio = jnp.pad(wio, ((0, 0), (0, ODD_W - wio.shape[1]))).astype(BF16)
    wa2p = jnp.pad(gla_wa2[0], ((0, LANES - GLA_RANK), (0, 0))).astype(BF16)
    hv = HG_H * HG_DV
    woe_h, woe_n = w_out_even[0, :hv].astype(BF16), w_out_even[0, hv:].astype(BF16)
    woo = w_out_odd[0].astype(BF16)
    nj = D_FF // FF_TILE
    w13t, w2b = [], []
    for l in range(ffn_w13.shape[0]):
        gate = ffn_w13[l, :, :D_FF].reshape(d, nj, FF_TILE)
        up = ffn_w13[l, :, D_FF:].reshape(d, nj, FF_TILE)
        w13t.append(jnp.concatenate([gate, up], axis=2).reshape(d, 2 * D_FF).astype(BF16))
        w2b.append(ffn_w2[l].astype(BF16))
    w1 = nsa_cmp_w1[0].reshape(2, NSA_BLK, NSA_DH, NSA_DH)
    w2 = nsa_cmp_w2[0]
    cw = 4 * NSA_DH
    dh = NSA_DH
    w1bd = jnp.concatenate([jnp.pad(w1[i // NSA_G], ((0, 0), (0, 0), (i * dh, cw - (i + 1) * dh)))
                            for i in range(2 * NSA_G)], axis=1)
    w2bd = jnp.concatenate([jnp.pad(w2[i // NSA_G], ((0, 0), (i * dh, cw - (i + 1) * dh)))
                            for i in range(2 * NSA_G)], axis=0)
    pe = nsa_cmp_pos[0]
    pe4 = jnp.concatenate([pe[0], pe[0], pe[1], pe[1]], axis=1)
    we = w_in_even[0]
    hgw = 2 * HG_H * HG_DK + 2 * HG_H * HG_DV
    nq0 = hgw + NSA_H * NSA_DH
    w_main = jnp.concatenate([we[:, :hgw], we[:, nq0:nq0 + cw]], axis=1).astype(BF16)
    w_tr = jnp.pad(we[:, hgw:].T, ((0, T_END - (we.shape[1] - hgw)), (0, 0))).astype(BF16)
    w1t = w1.transpose(0, 2, 1, 3)
    z1 = jnp.zeros_like(w1t[0])
    w1dec = jnp.concatenate([jnp.concatenate([w1t[0], z1, w1t[1], z1], axis=2),
                             jnp.concatenate([z1, w1t[0], z1, w1t[1]], axis=2)], axis=1)
    z2 = jnp.zeros_like(w2[0])
    w2dec = jnp.concatenate([jnp.concatenate([w2[0], z2, w2[1], z2], axis=1),
                             jnp.concatenate([z2, w2[0], z2, w2[1]], axis=1)], axis=0)
    pe_dec = jnp.tile(pe.transpose(2, 0, 1), (1, 1, 2))
    return dict(wie=wie, wio=wio, wa2p=wa2p, woe_h=woe_h, woe_n=woe_n, woo=woo, w13t=w13t, w2b=w2b,
                w1bd=w1bd.astype(BF16), w2bd=w2bd.astype(BF16), pe4=pe4, w_main=w_main, w_tr=w_tr,
                w1dec=w1dec.astype(BF16), w2dec=w2dec.astype(BF16), pe_dec=pe_dec)


def _forward_prompt(x, pw, hg_lb_logits, hg_norm, gla_ba, gla_norm, norm_mix, norm_ffn, norm_final):
    bsz, seq, d = x.shape
    n = bsz * seq
    x2d = x.reshape(n, d)
    tm = 512 if seq % 512 == 0 else 256
    tmf = next(t for t in (512, 256) if n % t == 0)
    tb = tm

    p, qt, rows_t, win_t, gate_t = _norm_proj_t(x2d, norm_mix[0], pw["w_main"], pw["w_tr"], bsz, seq, tm)
    o_h, hg_state = _hgrn_prompt(p, hg_lb_logits, hg_norm[0], bsz, seq, tb)
    cmp = _compress_prompt(p, pw["pe4"], pw["w1bd"], pw["w2bd"])
    nblk = seq // NSA_BLK
    cmp4 = cmp.reshape(bsz, nblk, 2 * NSA_G, NSA_DH).transpose(0, 2, 1, 3)
    kc = cmp4[:, :NSA_G].astype(BF16)
    vct = cmp4[:, NSA_G:].transpose(0, 1, 3, 2).astype(BF16)
    o_n = _nsa_attn_prompt(qt, rows_t, win_t, gate_t, kc, vct, bsz, seq)
    x1 = _out_ffn([o_h, o_n], x2d, [pw["woe_h"], pw["woe_n"]], norm_ffn[0], pw["w13t"][0], pw["w2b"][0], None, tmf)

    rows = rows_t.reshape(bsz, 4, NSA_G, NSA_DH, seq).transpose(0, 4, 1, 2, 3)[None]
    wn = min(NSA_WINDOW, seq)
    win = win_t[:, :, seq - wn:].reshape(bsz, 2, NSA_G, NSA_DH, wn).transpose(0, 4, 1, 2, 3)[None]

    p2 = _norm_proj(x1, norm_mix[1], pw["wio"], tm, ODD_W)
    o_g, gla_state = _gla_prompt(p2, pw["wa2p"], gla_ba[0], gla_norm[0], bsz, seq, tb)
    y = _out_ffn([o_g], x1, [pw["woo"]], norm_ffn[1], pw["w13t"][1], pw["w2b"][1], norm_final, tmf)
    return y.reshape(bsz, seq, d), rows, win, hg_state[None], gla_state[None]


def _forward_sample(x, pw, cache_nsa_kv, cache_win_kv, state_hgrn, state_gla, page_table, hg_lb_logits, hg_norm,
                    gla_ba, gla_norm, norm_mix, norm_ffn, norm_final):
    bsz, seq, d = x.shape
    x2d = x.reshape(bsz, d)
    npages = page_table.shape[1]
    nblk_total = npages * (PAGE // NSA_BLK) + 1
    nsel = min(NSA_TOPK, nblk_total) - 1

    p = _norm_proj(x2d, norm_mix[0], pw["wie"], bsz, PROJ_W // 2)
    o_h, hg_state = _hgrn_decode(p, hg_lb_logits, hg_norm[0], state_hgrn[0])

    cache2 = cache_nsa_kv[0].transpose(0, 2, 3, 4, 1).reshape(cache_nsa_kv.shape[1] * PAGE_ROWS, PAGE)
    q3 = p[:, EV_NQ:EV_CMP].reshape(bsz, NSA_H, NSA_DH)
    ocmp, sel = _dec_cmp(page_table, cache2, pw["pe_dec"], pw["w1dec"], pw["w2dec"], q3, nsel)
    sel = sel.reshape(bsz, NSA_G, 2, npages).transpose(0, 1, 3, 2).reshape(bsz, NSA_G, 2 * npages)
    idx = jnp.argsort(-sel, axis=-1, stable=True)[..., :NSA_TOPK].astype(I32)
    knew = p[:, EV_SLC:EV_GATE].reshape(bsz, 4, NSA_G * NSA_DH)
    wlen = cache_win_kv.shape[2]
    win_t = cache_win_kv[0].transpose(0, 2, 3, 4, 1).reshape(bsz, 2 * NSA_G * NSA_DH, wlen)
    o_n = _dec_attn(page_table, idx, cache2, q3, ocmp, knew, win_t, p[:, EV_GATE:EV_GATE + LANES], nsel)
    o_n = o_n.reshape(bsz, NSA_H * NSA_DH)
    x1 = _out_ffn([o_h, o_n], x2d, [pw["woe_h"], pw["woe_n"]], norm_ffn[0], pw["w13t"][0], pw["w2b"][0], None, bsz)

    rows = p[:, EV_CMP:EV_WIN].reshape(1, bsz, 1, 4, NSA_G, NSA_DH)
    win_new = p[:, EV_WIN:EV_GATE].reshape(1, bsz, 1, 2, NSA_G, NSA_DH)
    wk = jnp.concatenate([cache_win_kv[:1], win_new], axis=2)
    wn = min(NSA_WINDOW, npages * PAGE + 1)
    win = wk[:, :, wk.shape[2] - wn:]

    p2 = _norm_proj(x1, norm_mix[1], pw["wio"], bsz, ODD_W)
    o_g, gla_state = _gla_decode(p2, pw["wa2p"], gla_ba[0], gla_norm[0], state_gla[0])
    y = _out_ffn([o_g], x1, [pw["woo"]], norm_ffn[1], pw["w13t"][1], pw["w2b"][1], norm_final, bsz)
    return y.reshape(bsz, 1, d), rows, win, hg_state[None], gla_state[None]


def kernel(x_prompt, x_sample, cache_nsa_kv, cache_win_kv, state_hgrn, state_gla, page_table, w_in_even,
           hg_lb_logits, hg_norm, nsa_cmp_pos, nsa_cmp_w1, nsa_cmp_w2, w_out_even, w_in_odd, gla_wa2, gla_ba,
           gla_norm, w_out_odd, norm_mix, norm_ffn, norm_final, ffn_w13, ffn_w2):
    pw = _prep_weights(w_in_even, nsa_cmp_pos, nsa_cmp_w1, nsa_cmp_w2, w_out_even, w_in_odd, gla_wa2, w_out_odd,
                       ffn_w13, ffn_w2)
    y_p, kv_p, win_p, hg_p, gla_p = _forward_prompt(x_prompt, pw, hg_lb_logits, hg_norm, gla_ba, gla_norm,
                                                    norm_mix, norm_ffn, norm_final)
    y_s, kv_s, win_s, hg_s, gla_s = _forward_sample(x_sample, pw, cache_nsa_kv, cache_win_kv, state_hgrn,
                                                    state_gla, page_table, hg_lb_logits, hg_norm, gla_ba,
                                                    gla_norm, norm_mix, norm_ffn, norm_final)
    return (y_p, y_s, kv_p, kv_s, win_p, win_s, hg_p, hg_s, gla_p, gla_s)
```
